```python
import jax, jax.numpy as jnp
from jax import lax
import numpy as np

D_MODEL = 1024
BATCH = 32
SEQ = 2048
DEPTH = 1

GRID_W = 64
MIX_WIDTH = D_MODEL
NA_HEADS = 8
NA_HEAD_DIM = 64
NA_WIDTH = NA_HEADS * NA_HEAD_DIM
NA_WIN_R = 8
NA_WIN_C = 16
NA_QCB = 16
NA_KCB = 32
NA_NCB = GRID_W // NA_QCB
HG_HEADS = 4
HG_DK = 128
HG_DV = 128
HG_FDIM = HG_HEADS * HG_DK
HG_WIDTH = HG_HEADS * HG_DV
HG_CHUNK = 64
D_FF = 4 * D_MODEL
RMS_EPS = 1e-6
SPLIT_SIZES = (NA_WIDTH, NA_WIDTH, NA_WIDTH, HG_FDIM, HG_FDIM, HG_FDIM, HG_WIDTH, HG_WIDTH)
D_IN_PROJ = 3 * NA_WIDTH + 3 * HG_FDIM + 2 * HG_WIDTH

kernel_name = "hybrid_natten_hgrn2_encoder_block"


def _rmsnorm(x, w):
    xf = x.astype(jnp.float32)
    y = xf * lax.rsqrt(jnp.mean(xf * xf, axis=-1, keepdims=True) + RMS_EPS)
    return (y * w.astype(jnp.float32)).astype(x.dtype)


def _neighbourhood_attention(qa, ka, va, qn_w, kn_w, rpb, rows):
    B, S, _ = qa.shape
    wr = min(NA_WIN_R, rows)

    def to_grid(t, w):
        t = _rmsnorm(t.reshape(B, S, NA_HEADS, NA_HEAD_DIM).astype(jnp.float32), w) if w is not None \
            else t.reshape(B, S, NA_HEADS, NA_HEAD_DIM).astype(jnp.float32)
        return t.reshape(B, rows, GRID_W, NA_HEADS, NA_HEAD_DIM).transpose(0, 3, 1, 2, 4)

    q = to_grid(qa, qn_w)
    k = to_grid(ka, kn_w)
    v = to_grid(va, None)

    qcol = np.arange(GRID_W).reshape(NA_NCB, NA_QCB)
    band_start = np.clip(np.arange(NA_NCB) * NA_QCB - NA_WIN_C // 2, 0, GRID_W - NA_KCB)
    band_cols = band_start[:, None] + np.arange(NA_KCB)[None, :]
    col_start = np.clip(qcol - NA_WIN_C // 2, 0, GRID_W - NA_WIN_C)
    kc = band_cols[:, None, :]
    col_mask = (kc >= col_start[..., None]) & (kc < col_start[..., None] + NA_WIN_C)
    col_idx = np.clip(kc - qcol[..., None] + NA_WIN_C - 1, 0, 2 * NA_WIN_C - 2)

    k_band = k[:, :, :, band_cols]
    v_band = v[:, :, :, band_cols]
    q_blk = jnp.moveaxis(q.reshape(B, NA_HEADS, rows, NA_NCB, NA_QCB, NA_HEAD_DIM), 2, 0)
    rpb_cols = rpb.astype(jnp.float32)[:, :, col_idx]
    mask = jnp.asarray(col_mask)[:, :, None, :]
    scale = NA_HEAD_DIM ** -0.5

    def row_block(inp):
        r, q_r = inp
        start = jnp.clip(r - wr // 2, 0, rows - wr)
        k_r = lax.dynamic_slice_in_dim(k_band, start, wr, axis=2)
        v_r = lax.dynamic_slice_in_dim(v_band, start, wr, axis=2)
        s = jnp.einsum('bhnqd,bhrnkd->bhnqrk', q_r, k_r) * scale
        ridx = start + jnp.arange(wr, dtype=jnp.int32) - r + NA_WIN_R - 1
        bias = jnp.transpose(rpb_cols[:, ridx], (0, 2, 3, 1, 4))
        s = jnp.where(mask, s + bias, -jnp.inf)
        p = jax.nn.softmax(s, axis=(-2, -1))
        return jnp.einsum('bhnqrk,bhrnkd->bhnqd', p, v_r)

    o = lax.map(row_block, (jnp.arange(rows, dtype=jnp.int32), q_blk))
    o = jnp.moveaxis(o, 0, 2).reshape(B, NA_HEADS, S, NA_HEAD_DIM)
    return o.transpose(0, 2, 1, 3).reshape(B, S, NA_WIDTH)


def _gla_chunk_scan(q, k, log_f, v):
    B, H, S, dk = q.shape
    dv = v.shape[-1]
    n = S // HG_CHUNK

    def chunks(t):
        return jnp.moveaxis(t.reshape(B, H, n, HG_CHUNK, t.shape[-1]), 2, 0)

    qc, kc, vc = chunks(q), chunks(k), chunks(v)
    bc = jnp.cumsum(chunks(log_f), axis=3)
    incl = jnp.tril(jnp.ones((HG_CHUNK, HG_CHUNK), dtype=bool))

    def step(state, inp):
        qi, ki, bi, vi = inp
        o_inter = jnp.einsum('bhid,bhdv->bhiv', qi * jnp.exp(bi), state)
        diff = bi[:, :, :, None, :] - bi[:, :, None, :, :]
        decay = jnp.exp(jnp.where(incl[:, :, None], diff, -jnp.inf))
        a = jnp.einsum('bhid,bhjd,bhijd->bhij', qi, ki, decay)
        o = o_inter + jnp.einsum('bhij,bhjv->bhiv', a, vi)
        b_last = bi[:, :, -1:, :]
        state = jnp.exp(b_last[:, :, 0, :])[..., None] * state + \
            jnp.einsum('bhjd,bhjv->bhdv', ki * jnp.exp(b_last - bi), vi)
        return state, o

    s0 = jnp.zeros((B, H, dk, dv), jnp.float32)
    _, o = lax.scan(step, s0, (qc, kc, bc, vc))
    return jnp.moveaxis(o, 0, 2).reshape(B, H, S, dv)


def _hgrn2_bidirectional(hq, hf_fwd, hf_bwd, hi, hg, lb_f, lb_b, gnorm_w):
    B, S, _ = hq.shape

    def heads(t, d):
        return t.reshape(B, S, HG_HEADS, d).transpose(0, 2, 1, 3)

    q = heads(hq.astype(jnp.float32), HG_DK)
    v = heads(hi.astype(jnp.float32), HG_DV)

    def gate(hf, lb):
        f = lb + (1.0 - lb) * jax.nn.sigmoid(hf.astype(jnp.float32))
        return heads(1.0 - f, HG_DK), heads(jnp.log(f), HG_DK)

    k_f, lf_f = gate(hf_fwd, lb_f)
    k_b, lf_b = gate(hf_bwd, lb_b)
    o_fwd = _gla_chunk_scan(q, k_f, lf_f, v)
    flip = lambda t: jnp.flip(t, axis=2)
    o_bwd = flip(_gla_chunk_scan(flip(q), flip(k_b), flip(lf_b), flip(v)))
    o = (o_fwd + o_bwd).transpose(0, 2, 1, 3)
    g = hg.astype(jnp.float32).reshape(B, S, HG_HEADS, HG_DV)
    o = _rmsnorm(o, gnorm_w) * jax.nn.silu(g)
    return o.reshape(B, S, HG_WIDTH)


def setup_inputs(seed: int = 0) -> dict:
    key = jax.random.key(seed)
    ks = jax.random.split(key, 14)
    f32 = jnp.float32
    nrm = lambda k, shape, s: jax.random.normal(k, shape, f32) * s
    return {
        "x": jax.random.normal(ks[0], (BATCH, SEQ, D_MODEL), f32),
        "w_in": nrm(ks[1], (DEPTH, D_MODEL, D_IN_PROJ), D_MODEL ** -0.5),
        "w_out": nrm(ks[2], (DEPTH, MIX_WIDTH, D_MODEL), MIX_WIDTH ** -0.5),
        "attn_norm_w": 1.0 + nrm(ks[3], (DEPTH, D_MODEL), 0.02),
        "mlp_norm_w": 1.0 + nrm(ks[4], (DEPTH, D_MODEL), 0.02),
        "q_norm_w": 1.0 + nrm(ks[5], (DEPTH, NA_HEAD_DIM), 0.02),
        "k_norm_w": 1.0 + nrm(ks[6], (DEPTH, NA_HEAD_DIM), 0.02),
        "rpb": nrm(ks[7], (DEPTH, NA_HEADS, 2 * NA_WIN_R - 1, 2 * NA_WIN_C - 1), 0.02),
        "hg_norm_w": 1.0 + nrm(ks[8], (DEPTH, HG_DV), 0.02),
        "lb_fwd": 1.0 + nrm(ks[9], (DEPTH + 1, HG_FDIM), 0.1),
        "lb_bwd": 1.0 + nrm(ks[10], (DEPTH + 1, HG_FDIM), 0.1),
        "w_up": nrm(ks[11], (DEPTH, D_MODEL, D_FF), D_MODEL ** -0.5),
        "w_down": nrm(ks[12], (DEPTH, D_FF, D_MODEL), D_FF ** -0.5),
    }


def reference(x, w_in, w_out, attn_norm_w, mlp_norm_w, q_norm_w, k_norm_w, rpb,
              hg_norm_w, lb_fwd, lb_bwd, w_up, w_down):
    B, S, D = x.shape
    rows = S // GRID_W
    split_idx = [int(i) for i in np.cumsum(SPLIT_SIZES)[:-1]]
    lbf_all = jnp.cumsum(jax.nn.softmax(lb_fwd.astype(jnp.float32), axis=0), axis=0)
    lbb_all = jnp.cumsum(jax.nn.softmax(lb_bwd.astype(jnp.float32), axis=0), axis=0)
    h = x
    for layer in range(DEPTH):
        u = _rmsnorm(h, attn_norm_w[layer])
        proj = jnp.einsum('bsd,de->bse', u, w_in[layer])
        qa, ka, va, hq, hff, hfb, hi, hg = jnp.split(proj, split_idx, axis=-1)
        y_na = _neighbourhood_attention(qa, ka, va, q_norm_w[layer], k_norm_w[layer], rpb[layer], rows)
        y_hg = _hgrn2_bidirectional(hq, hff, hfb, hi, hg, lbf_all[layer], lbb_all[layer], hg_norm_w[layer])
        mix = jnp.concatenate([y_na, y_hg], axis=-1).astype(h.dtype)
        h = h + jnp.einsum('bse,ed->bsd', mix, w_out[layer])
        u = _rmsnorm(h, mlp_norm_w[layer])
        a = jnp.square(jax.nn.relu(jnp.einsum('bsd,df->bsf', u, w_up[layer])))
        h = h + jnp.einsum('bsf,fd->bsd', a, w_down[layer])
    return h
```

```python
import functools

import numpy as np
import jax
import jax.numpy as jnp
from jax import lax
from jax.experimental import pallas as pl
from jax.experimental.pallas import tpu as pltpu

F32 = jnp.float32
BF16 = jnp.bfloat16

D_MODEL = 1024
GRID_W = 64
NA_HEADS = 8
NA_HEAD_DIM = 64
NA_WIDTH = NA_HEADS * NA_HEAD_DIM
NA_WIN_R = 8
NA_WIN_C = 16
HG_HEADS = 4
HG_DK = 128
HG_WIDTH = HG_HEADS * HG_DK
D_FF = 4 * D_MODEL
D_IN_PROJ = 3 * NA_WIDTH + 5 * HG_WIDTH
RMS_EPS = 1e-6

V7X_LANES = 128
V7X_VMEM_LIMIT_BYTES = 56 * 1024 * 1024

TOKEN_TILE = 512
FF_CHUNK = 1024
NORM_ROWS = 256
HG_CHUNK = 128
HG_LEVELS = 7
NA_QT = 16
NA_KT = 32
NA_TILES = GRID_W // NA_QT
MASK_NEG = -1e30

_NA_QCOLS = [list(range(0, 8)) + list(range(56, 64)),
             list(range(8, 24)), list(range(24, 40)), list(range(40, 56))]
_NA_KSEGS = [[(0, 16), (48, 16)], [(0, 32)], [(16, 32)], [(32, 32)]]


def _nt_dot(a, b):
    return lax.dot_general(a, b, (((1,), (1,)), ((), ())), preferred_element_type=F32)


def _dot(a, b):
    return jnp.dot(a, b, preferred_element_type=F32)


def _split_bf16(x):
    hi = x.astype(BF16)
    lo = (x - hi.astype(F32)).astype(BF16)
    return hi, lo


def _in_proj_kernel(x_ref, nw_ref, w_ref, o_ref):
    x = x_ref[...]
    ms = jnp.mean(x * x, axis=-1, keepdims=True)
    u = (x * lax.rsqrt(ms + RMS_EPS) * nw_ref[...]).astype(BF16)
    o_ref[...] = _dot(u, w_ref[...]).astype(o_ref.dtype)


def _in_proj(x2, norm_w, w_in):
    m = x2.shape[0]
    return pl.pallas_call(
        _in_proj_kernel,
        grid=(m // TOKEN_TILE,),
        in_specs=[
            pl.BlockSpec((TOKEN_TILE, D_MODEL), lambda i: (i, 0)),
            pl.BlockSpec((1, D_MODEL), lambda i: (0, 0)),
            pl.BlockSpec((D_MODEL, D_IN_PROJ), lambda i: (0, 0), pipeline_mode=pl.Buffered(1)),
        ],
        out_specs=pl.BlockSpec((TOKEN_TILE, D_IN_PROJ), lambda i: (i, 0)),
        out_shape=jax.ShapeDtypeStruct((m, D_IN_PROJ), BF16),
        compiler_params=pltpu.CompilerParams(
            dimension_semantics=("arbitrary",), vmem_limit_bytes=V7X_VMEM_LIMIT_BYTES),
        name="in_proj",
    )(x2, norm_w, w_in)


def _na_bias_table(rpb):
    tiles = []
    for t in range(NA_TILES):
        qc = np.array(_NA_QCOLS[t])
        kc = np.concatenate([np.arange(c0, c0 + n) for c0, n in _NA_KSEGS[t]])
        cs = np.clip(qc - NA_WIN_C // 2, 0, GRID_W - NA_WIN_C)
        valid = (kc[None, :] >= cs[:, None]) & (kc[None, :] < cs[:, None] + NA_WIN_C)
        cidx = np.clip(kc[None, :] - qc[:, None] + NA_WIN_C - 1, 0, 2 * NA_WIN_C - 2)
        tiles.append((valid, cidx))
    valid = np.stack([v for v, _ in tiles])
    cidx = np.stack([c for _, c in tiles])
    rs = np.arange(NA_WIN_R)[:, None]
    kr = np.arange(NA_WIN_R)[None, :]
    ridx = kr - rs + NA_WIN_R - 1
    b = rpb.astype(F32)[:, ridx]
    b = b[:, :, :, cidx]
    b = jnp.where(jnp.asarray(valid)[None, None, None], b, MASK_NEG)
    b = jnp.transpose(b, (1, 0, 3, 4, 2, 5))
    return b.reshape(NA_WIN_R, NA_HEADS, NA_TILES, NA_QT, NA_WIN_R * NA_KT)


def _natten_kernel(q_ref, k_ref, v_ref, qw_ref, kw_ref, bias_ref, hm_ref, o_ref, qn_s, kn_s,
                   *, rows):
    seq = rows * GRID_W
    hm = hm_ref[...]

    def head_rmsnorm(x, w):
        x2 = x * x
        hi, lo = _split_bf16(x2)
        ms = _dot(hi, hm) + _dot(lo, hm)
        return x * lax.rsqrt(ms + RMS_EPS) * w

    def norm_body(i, carry):
        sl = pl.ds(pl.multiple_of(i * NORM_ROWS, NORM_ROWS), NORM_ROWS)
        qn = head_rmsnorm(q_ref[sl, :].astype(F32), qw_ref[...])
        qn_s[sl, :] = qn * (NA_HEAD_DIM ** -0.5)
        kn_s[sl, :] = head_rmsnorm(k_ref[sl, :].astype(F32), kw_ref[...]).astype(BF16)
        return carry

    lax.fori_loop(0, seq // NORM_ROWS, norm_body, 0)

    lane = lax.broadcasted_iota(jnp.int32, (NA_QT, V7X_LANES), 1)
    first_head = lane < NA_HEAD_DIM
    wr = min(NA_WIN_R, rows)

    def row_body(r, carry):
        start = jnp.clip(r - wr // 2, 0, rows - wr)
        rs = r - start
        qrow = qn_s[pl.ds(pl.multiple_of(r * GRID_W, GRID_W), GRID_W), :]
        base = start * GRID_W
        o_tiles = []
        for t in range(NA_TILES):
            qc = _NA_QCOLS[t]
            if t == 0:
                qt = jnp.concatenate([qrow[0:8], qrow[56:64]], axis=0)
            else:
                qt = qrow[qc[0]:qc[0] + NA_QT]
            ksl = [pl.ds(pl.multiple_of(base + kr * GRID_W + c0, 16), n)
                   for kr in range(wr) for (c0, n) in _NA_KSEGS[t]]
            kb = jnp.concatenate([kn_s[s, :] for s in ksl], axis=0)
            vb = jnp.concatenate([v_ref[s, :] for s in ksl], axis=0)
            o_heads = []
            for hh in range(2):
                sel = first_head if hh == 0 else jnp.logical_not(first_head)
                qh = jnp.where(sel, qt, 0.0).astype(BF16)
                s = _nt_dot(qh, kb) + bias_ref[rs, hh, t]
                m = jnp.max(s, axis=-1, keepdims=True)
                p = jnp.exp(s - m)
                l = jnp.sum(p, axis=-1, keepdims=True)
                o_heads.append(_dot(p.astype(BF16), vb) / l)
            o_tiles.append(jnp.where(first_head, o_heads[0], o_heads[1]))
        orow = jnp.concatenate(
            [o_tiles[0][0:8], o_tiles[1], o_tiles[2], o_tiles[3], o_tiles[0][8:16]], axis=0)
        o_ref[pl.ds(pl.multiple_of(r * GRID_W, GRID_W), GRID_W), :] = orow.astype(o_ref.dtype)
        return carry

    lax.fori_loop(0, rows, row_body, 0)


def _natten(proj, q_norm_w, k_norm_w, rpb, batch, seq):
    rows = seq // GRID_W
    assert rows >= NA_WIN_R and GRID_W == 64 and NA_HEAD_DIM * 2 == V7X_LANES
    pairs = NA_HEADS // 2
    bias = _na_bias_table(rpb)
    qw = jnp.tile(q_norm_w.astype(F32).reshape(1, NA_HEAD_DIM), (1, 2))
    kw = jnp.tile(k_norm_w.astype(F32).reshape(1, NA_HEAD_DIM), (1, 2))
    head_id = np.arange(V7X_LANES) // NA_HEAD_DIM
    hm = jnp.asarray((head_id[:, None] == head_id[None, :]) / NA_HEAD_DIM, BF16)
    col = lambda off: pl.BlockSpec((seq, V7X_LANES), lambda b, p, off=off: (b, off + p))
    const2 = lambda shape: pl.BlockSpec(shape, lambda b, p: (0, 0))
    return pl.pallas_call(
        functools.partial(_natten_kernel, rows=rows),
        grid=(batch, pairs),
        in_specs=[
            col(0), col(pairs), col(2 * pairs),
            const2((1, V7X_LANES)), const2((1, V7X_LANES)),
            pl.BlockSpec((NA_WIN_R, 2, NA_TILES, NA_QT, NA_WIN_R * NA_KT),
                         lambda b, p: (0, p, 0, 0, 0)),
            const2((V7X_LANES, V7X_LANES)),
        ],
        out_specs=pl.BlockSpec((seq, V7X_LANES), lambda b, p: (b, p)),
        out_shape=jax.ShapeDtypeStruct((batch * seq, NA_WIDTH), BF16),
        scratch_shapes=[pltpu.VMEM((seq, V7X_LANES), F32), pltpu.VMEM((seq, V7X_LANES), BF16)],
        compiler_params=pltpu.CompilerParams(
            dimension_semantics=("arbitrary", "arbitrary"),
            vmem_limit_bytes=V7X_VMEM_LIMIT_BYTES),
        name="natten",
    )(proj, proj, proj, qw, kw, bias, hm)


def _hg_tables(reverse):
    c = HG_CHUNK
    idx = np.arange(c)
    blocks = [(idx[None, :] <= idx[:, None]), (idx[None, :] > idx[:, None])]
    masks = []
    for lvl in range(HG_LEVELS):
        s = 1 << lvl
        mid = (idx // (2 * s)) * (2 * s) + s
        later = idx >= mid
        t = idx[None, :]
        blk = np.where(later[:, None], (t >= mid[:, None]) & (t <= idx[:, None]),
                       (t > idx[:, None]) & (t < mid[:, None]))
        blocks.append(blk)
        same_pair = (idx[:, None] // (2 * s)) == (idx[None, :] // (2 * s))
        masks.append(same_pair & later[:, None] & (~later)[None, :])
    masks.append(np.eye(c, dtype=bool))
    tmat = np.concatenate(blocks, axis=0).astype(np.float32)
    mask = np.stack(masks).astype(np.float32)
    if reverse:
        tmat = tmat.reshape(-1, c, c)[:, ::-1, ::-1].reshape(-1, c)
        mask = mask[:, ::-1, ::-1]
    return jnp.asarray(tmat, BF16), jnp.asarray(mask, F32)


def _hgrn_kernel(q_ref, ff_ref, fb_ref, v_ref, g_ref, lbf_ref, lbb_ref, gw_ref,
                 tf_ref, tb_ref, mf_ref, mb_ref, o_ref, acc_s, *, seq):
    c = HG_CHUNK
    n_chunks = seq // c

    def lower_bound(lb_ref):
        a = lb_ref[...].astype(F32)
        a0, a1 = a[0:1], a[1:2]
        m = jnp.maximum(a0, a1)
        e0, e1 = jnp.exp(a0 - m), jnp.exp(a1 - m)
        return e0 / (e0 + e1)

    def chunk(ci, state_t, f_ref, lb, t_ref, m_ref, reverse, first_pass):
        sl = pl.ds(pl.multiple_of(ci * c, c), c)
        q = q_ref[sl, :].astype(F32)
        v = v_ref[sl, :]
        hf = f_ref[sl, :].astype(F32)
        f = lb + (1.0 - lb) * (1.0 / (1.0 + jnp.exp(-hf)))
        k = 1.0 - f
        hi, lo = _split_bf16(jnp.log(f))
        tm = t_ref[...]
        e = jnp.exp(_dot(tm, hi) + _dot(tm, lo))
        e_b, e_rest = e[0:c], e[c:2 * c]
        o = _nt_dot((q * e_b).astype(BF16), state_t.astype(BF16))
        a = m_ref[HG_LEVELS] * _nt_dot(q.astype(BF16), k.astype(BF16))
        for lvl in range(HG_LEVELS):
            el = e[(2 + lvl) * c:(3 + lvl) * c]
            a = a + m_ref[lvl] * _nt_dot((q * el).astype(BF16), (k * el).astype(BF16))
        o = o + _dot(a.astype(BF16), v)
        if first_pass:
            acc_s[sl, :] = o
        else:
            acc_s[sl, :] = acc_s[sl, :] + o
        e_tot = e_b[0:1] if reverse else e_b[c - 1:c]
        kd = (k * e_rest).astype(BF16)
        return state_t * e_tot + _dot(v.T, kd)

    lbf = lower_bound(lbf_ref)
    lbb = lower_bound(lbb_ref)
    zero = jnp.zeros((HG_DK, HG_DK), F32)
    lax.fori_loop(
        0, n_chunks,
        lambda i, s: chunk(i, s, ff_ref, lbf, tf_ref, mf_ref, False, True), zero)
    lax.fori_loop(
        0, n_chunks,
        lambda i, s: chunk(n_chunks - 1 - i, s, fb_ref, lbb, tb_ref, mb_ref, True, False), zero)

    def out_body(i, carry):
        sl = pl.ds(pl.multiple_of(i * NORM_ROWS, NORM_ROWS), NORM_ROWS)
        o = acc_s[sl, :]
        ms = jnp.mean(o * o, axis=-1, keepdims=True)
        y = o * lax.rsqrt(ms + RMS_EPS) * gw_ref[...]
        g = g_ref[sl, :].astype(F32)
        o_ref[sl, :] = (y * (g * (1.0 / (1.0 + jnp.exp(-g))))).astype(o_ref.dtype)
        return carry

    lax.fori_loop(0, seq // NORM_ROWS, out_body, 0)


def _hgrn2(proj, lb_fwd, lb_bwd, hg_norm_w, batch, seq):
    assert HG_DK == V7X_LANES and lb_fwd.shape[0] == 2
    tf, mf = _hg_tables(False)
    tb, mb = _hg_tables(True)
    first = 3 * NA_WIDTH // V7X_LANES
    col = lambda k: pl.BlockSpec((seq, V7X_LANES),
                                 lambda b, h, k=k: (b, first + k * HG_HEADS + h))
    lbs = pl.BlockSpec((2, V7X_LANES), lambda b, h: (0, h))
    const = lambda a: pl.BlockSpec(a.shape, lambda b, h, nd=a.ndim: (0,) * nd)
    gw = hg_norm_w.astype(F32).reshape(1, HG_DK)
    return pl.pallas_call(
        functools.partial(_hgrn_kernel, seq=seq),
        grid=(batch, HG_HEADS),
        in_specs=[col(0), col(1), col(2), col(3), col(4), lbs, lbs, const(gw),
                  const(tf), const(tb), const(mf), const(mb)],
        out_specs=pl.BlockSpec((seq, V7X_LANES), lambda b, h: (b, h)),
        out_shape=jax.ShapeDtypeStruct((batch * seq, HG_WIDTH), BF16),
        scratch_shapes=[pltpu.VMEM((seq, HG_DK), F32)],
        compiler_params=pltpu.CompilerParams(
            dimension_semantics=("arbitrary", "arbitrary"),
            vmem_limit_bytes=V7X_VMEM_LIMIT_BYTES),
        name="hgrn2",
    )(proj, proj, proj, proj, proj, lb_fwd.astype(F32), lb_bwd.astype(F32), gw,
      tf, tb, mf, mb)


def _out_mlp_kernel(x_ref, na_ref, hg_ref, wo_na_ref, wo_hg_ref, nw_ref, wu_ref, wd_ref, o_ref):
    h = x_ref[...] + _dot(na_ref[...], wo_na_ref[...]) + _dot(hg_ref[...], wo_hg_ref[...])
    ms = jnp.mean(h * h, axis=-1, keepdims=True)
    u = (h * lax.rsqrt(ms + RMS_EPS) * nw_ref[...]).astype(BF16)
    mlp = None
    for j in range(D_FF // FF_CHUNK):
        cols = slice(j * FF_CHUNK, (j + 1) * FF_CHUNK)
        a = jnp.maximum(_dot(u, wu_ref[:, cols]), 0.0)
        part = _dot((a * a).astype(BF16), wd_ref[cols, :])
        mlp = part if mlp is None else mlp + part
    o_ref[...] = h + mlp


def _out_mlp(x2, y_na, y_hg, w_out, norm_w, w_up, w_down):
    m = x2.shape[0]
    tile = lambda width: pl.BlockSpec((TOKEN_TILE, width), lambda i: (i, 0))
    resident = lambda a: pl.BlockSpec(a.shape, lambda i: (0, 0), pipeline_mode=pl.Buffered(1))
    wo_na, wo_hg = w_out[:NA_WIDTH], w_out[NA_WIDTH:]
    return pl.pallas_call(
        _out_mlp_kernel,
        grid=(m // TOKEN_TILE,),
        in_specs=[tile(D_MODEL), tile(NA_WIDTH), tile(HG_WIDTH), resident(wo_na),
                  resident(wo_hg), pl.BlockSpec((1, D_MODEL), lambda i: (0, 0)),
                  resident(w_up), resident(w_down)],
        out_specs=tile(D_MODEL),
        out_shape=jax.ShapeDtypeStruct((m, D_MODEL), F32),
        compiler_params=pltpu.CompilerParams(
            dimension_semantics=("arbitrary",), vmem_limit_bytes=V7X_VMEM_LIMIT_BYTES),
        name="out_mlp",
    )(x2, y_na, y_hg, wo_na, wo_hg, norm_w, w_up, w_down)


def kernel(x, w_in, w_out, attn_norm_w, mlp_norm_w, q_norm_w, k_norm_w, rpb, hg_norm_w,
           lb_fwd, lb_bwd, w_up, w_down):
    batch, seq, d = x.shape
    assert d == D_MODEL and w_in.shape[0] == 1 and seq % GRID_W == 0 and seq % HG_CHUNK == 0
    x2 = x.reshape(batch * seq, d)
    proj = _in_proj(x2, attn_norm_w[0].astype(F32).reshape(1, d), w_in[0].astype(BF16))
    y_na = _natten(proj, q_norm_w[0], k_norm_w[0], rpb[0], batch, seq)
    y_hg = _hgrn2(proj, lb_fwd, lb_bwd, hg_norm_w[0], batch, seq)
    out = _out_mlp(x2, y_na, y_hg, w_out[0].astype(BF16),
                   mlp_norm_w[0].astype(F32).reshape(1, d),
                   w_up[0].astype(BF16), w_down[0].astype(BF16))
    return out.reshape(batch, seq, d)
```

```python
import functools

import numpy as np
import jax
import jax.numpy as jnp
from jax import lax
from jax.experimental import pallas as pl
from jax.experimental.pallas import tpu as pltpu

F32 = jnp.float32
BF16 = jnp.bfloat16

D_MODEL = 1024
GRID_W = 64
NA_HEADS = 8
NA_HEAD_DIM = 64
NA_WIDTH = NA_HEADS * NA_HEAD_DIM
NA_WIN_R = 8
NA_WIN_C = 16
HG_HEADS = 4
HG_DK = 128
HG_WIDTH = HG_HEADS * HG_DK
D_FF = 4 * D_MODEL
D_IN_PROJ = 3 * NA_WIDTH + 5 * HG_WIDTH
RMS_EPS = 1e-6

V7X_LANES = 128
V7X_VMEM_LIMIT_BYTES = 56 * 1024 * 1024

TOKEN_TILE = 512
FF_CHUNK = 1024
NORM_ROWS = 256
HG_CHUNK = 128
HG_LEVELS = 7
NA_QT = 16
NA_KT = 32
NA_TILES = GRID_W // NA_QT
NA_ROWS_PER_STEP = 4
MASK_NEG = -1e30

_NA_QCOLS = [list(range(0, 8)) + list(range(56, 64)),
             list(range(8, 24)), list(range(24, 40)), list(range(40, 56))]
_NA_KSEGS = [[(0, 16), (48, 16)], [(0, 32)], [(16, 32)], [(32, 32)]]


def _nt_dot(a, b):
    return lax.dot_general(a, b, (((1,), (1,)), ((), ())), preferred_element_type=F32)


def _dot(a, b):
    return jnp.dot(a, b, preferred_element_type=F32)


def _split_bf16(x):
    hi = x.astype(BF16)
    lo = (x - hi.astype(F32)).astype(BF16)
    return hi, lo


def _in_proj_kernel(x_ref, nw_ref, w_ref, o_ref):
    x = x_ref[...]
    ms = jnp.mean(x * x, axis=-1, keepdims=True)
    u = (x * lax.rsqrt(ms + RMS_EPS) * nw_ref[...]).astype(BF16)
    o_ref[...] = _dot(u, w_ref[...]).astype(o_ref.dtype)


def _in_proj(x2, norm_w, w_in):
    m = x2.shape[0]
    return pl.pallas_call(
        _in_proj_kernel,
        grid=(m // TOKEN_TILE,),
        in_specs=[
            pl.BlockSpec((TOKEN_TILE, D_MODEL), lambda i: (i, 0)),
            pl.BlockSpec((1, D_MODEL), lambda i: (0, 0)),
            pl.BlockSpec((D_MODEL, D_IN_PROJ), lambda i: (0, 0), pipeline_mode=pl.Buffered(1)),
        ],
        out_specs=pl.BlockSpec((TOKEN_TILE, D_IN_PROJ), lambda i: (i, 0)),
        out_shape=jax.ShapeDtypeStruct((m, D_IN_PROJ), BF16),
        compiler_params=pltpu.CompilerParams(
            dimension_semantics=("arbitrary",), vmem_limit_bytes=V7X_VMEM_LIMIT_BYTES),
        name="in_proj",
    )(x2, norm_w, w_in)


def _na_bias_table(rpb):
    tiles = []
    for t in range(NA_TILES):
        qc = np.array(_NA_QCOLS[t])
        kc = np.concatenate([np.arange(c0, c0 + n) for c0, n in _NA_KSEGS[t]])
        cs = np.clip(qc - NA_WIN_C // 2, 0, GRID_W - NA_WIN_C)
        valid = (kc[None, :] >= cs[:, None]) & (kc[None, :] < cs[:, None] + NA_WIN_C)
        cidx = np.clip(kc[None, :] - qc[:, None] + NA_WIN_C - 1, 0, 2 * NA_WIN_C - 2)
        tiles.append((valid, cidx))
    valid = np.stack([v for v, _ in tiles])
    cidx = np.stack([c for _, c in tiles])
    rs = np.arange(NA_WIN_R)[:, None]
    kr = np.arange(NA_WIN_R)[None, :]
    ridx = kr - rs + NA_WIN_R - 1
    b = rpb.astype(F32)[:, ridx]
    b = b[:, :, :, cidx]
    b = jnp.where(jnp.asarray(valid)[None, None, None], b, MASK_NEG)
    b = b.reshape(NA_HEADS // 2, 2, *b.shape[1:])
    b = jnp.transpose(b, (2, 0, 4, 1, 5, 3, 6))
    return b.reshape(NA_WIN_R, NA_HEADS // 2, NA_TILES, 2 * NA_QT, NA_WIN_R * NA_KT)


def _natten_kernel(q_ref, k_ref, v_ref, qw_ref, kw_ref, bias_ref, hm_ref, o_ref, qn_s, kn_s,
                   *, rows):
    seq = rows * GRID_W
    hm = hm_ref[...]

    def head_rmsnorm(x, w):
        x2 = x * x
        hi, lo = _split_bf16(x2)
        ms = _dot(hi, hm) + _dot(lo, hm)
        return x * lax.rsqrt(ms + RMS_EPS) * w

    def norm_body(i, carry):
        sl = pl.ds(pl.multiple_of(i * NORM_ROWS, NORM_ROWS), NORM_ROWS)
        qn = head_rmsnorm(q_ref[sl, :].astype(F32), qw_ref[...])
        qn_s[sl, :] = qn * (NA_HEAD_DIM ** -0.5)
        kn_s[sl, :] = head_rmsnorm(k_ref[sl, :].astype(F32), kw_ref[...]).astype(BF16)
        return carry

    lax.fori_loop(0, seq // NORM_ROWS, norm_body, 0)

    lane = lax.broadcasted_iota(jnp.int32, (NA_QT, V7X_LANES), 1)
    first_head = lane < NA_HEAD_DIM
    wr = min(NA_WIN_R, rows)

    def row_block(ref, grid_row):
        if isinstance(grid_row, int):
            return ref[grid_row * GRID_W:(grid_row + 1) * GRID_W, :]
        return ref[pl.ds(pl.multiple_of(grid_row * GRID_W, GRID_W), GRID_W), :]

    def band(blocks, t):
        return jnp.concatenate(
            [blk[c0:c0 + n] for blk in blocks for (c0, n) in _NA_KSEGS[t]], axis=0)

    def step(row0, key_row0, row_off, rel_starts):
        n_key_rows = max(rel_starts) + wr
        kblocks = [row_block(kn_s, key_row0 + u) for u in range(n_key_rows)]
        scores = []
        for j, rel in enumerate(rel_starts):
            qrow = row_block(qn_s, row0 + j)
            rs = row_off + j - rel
            for t in range(NA_TILES):
                qc = _NA_QCOLS[t]
                if t == 0:
                    qt = jnp.concatenate([qrow[0:8], qrow[56:64]], axis=0)
                else:
                    qt = qrow[qc[0]:qc[0] + NA_QT]
                q2 = jnp.concatenate([jnp.where(first_head, qt, 0.0),
                                      jnp.where(first_head, 0.0, qt)], axis=0).astype(BF16)
                kb = band(kblocks[rel:rel + wr], t)
                scores.append(_nt_dot(q2, kb) + bias_ref[rs, t])
        probs, sums = [], []
        for s in scores:
            p = jnp.exp(s - jnp.max(s, axis=-1, keepdims=True))
            sums.append(jnp.sum(p, axis=-1, keepdims=True))
            probs.append(p.astype(BF16))
        vblocks = [row_block(v_ref, key_row0 + u) for u in range(n_key_rows)]
        for j, rel in enumerate(rel_starts):
            o_tiles = []
            for t in range(NA_TILES):
                vb = band(vblocks[rel:rel + wr], t)
                o2 = _dot(probs[j * NA_TILES + t], vb) / sums[j * NA_TILES + t]
                o_tiles.append(jnp.where(first_head, o2[0:NA_QT], o2[NA_QT:2 * NA_QT]))
            orow = jnp.concatenate(
                [o_tiles[0][0:8], o_tiles[1], o_tiles[2], o_tiles[3], o_tiles[0][8:16]], axis=0)
            if isinstance(row0, int):
                osl = slice((row0 + j) * GRID_W, (row0 + j + 1) * GRID_W)
            else:
                osl = pl.ds(pl.multiple_of((row0 + j) * GRID_W, GRID_W), GRID_W)
            o_ref[osl, :] = orow.astype(o_ref.dtype)

    u = NA_ROWS_PER_STEP
    assert rows % u == 0
    starts = [[min(max(i * u + j - wr // 2, 0), rows - wr) for j in range(u)]
              for i in range(rows // u)]
    regular = [st == [i * u + j - wr // 2 for j in range(u)] for i, st in enumerate(starts)]
    lo = regular.index(True)
    hi = len(regular) - regular[::-1].index(True)
    assert all(regular[lo:hi])

    def peeled(i):
        step(i * u, starts[i][0], i * u - starts[i][0], [s - starts[i][0] for s in starts[i]])

    for i in range(lo):
        peeled(i)

    def regular_body(i, carry):
        step(i * u, i * u - wr // 2, wr // 2, list(range(u)))
        return carry

    lax.fori_loop(lo, hi, regular_body, 0)
    for i in range(hi, rows // u):
        peeled(i)


def _natten(proj, q_norm_w, k_norm_w, rpb, batch, seq):
    rows = seq // GRID_W
    assert rows >= NA_WIN_R and GRID_W == 64 and NA_HEAD_DIM * 2 == V7X_LANES
    pairs = NA_HEADS // 2
    bias = _na_bias_table(rpb)
    qw = jnp.tile(q_norm_w.astype(F32).reshape(1, NA_HEAD_DIM), (1, 2))
    kw = jnp.tile(k_norm_w.astype(F32).reshape(1, NA_HEAD_DIM), (1, 2))
    head_id = np.arange(V7X_LANES) // NA_HEAD_DIM
    hm = jnp.asarray((head_id[:, None] == head_id[None, :]) / NA_HEAD_DIM, BF16)
    col = lambda off: pl.BlockSpec((seq, V7X_LANES), lambda b, p, off=off: (b, off + p))
    const2 = lambda shape: pl.BlockSpec(shape, lambda b, p: (0, 0))
    return pl.pallas_call(
        functools.partial(_natten_kernel, rows=rows),
        grid=(batch, pairs),
        in_specs=[
            col(0), col(pairs), col(2 * pairs),
            const2((1, V7X_LANES)), const2((1, V7X_LANES)),
            pl.BlockSpec((NA_WIN_R, None, NA_TILES, 2 * NA_QT, NA_WIN_R * NA_KT),
                         lambda b, p: (0, p, 0, 0, 0)),
            const2((V7X_LANES, V7X_LANES)),
        ],
        out_specs=pl.BlockSpec((seq, V7X_LANES), lambda b, p: (b, p)),
        out_shape=jax.ShapeDtypeStruct((batch * seq, NA_WIDTH), BF16),
        scratch_shapes=[pltpu.VMEM((seq, V7X_LANES), F32), pltpu.VMEM((seq, V7X_LANES), BF16)],
        compiler_params=pltpu.CompilerParams(
            dimension_semantics=("arbitrary", "arbitrary"),
            vmem_limit_bytes=V7X_VMEM_LIMIT_BYTES),
        name="natten",
    )(proj, proj, proj, qw, kw, bias, hm)


def _hg_tables(reverse):
    c = HG_CHUNK
    idx = np.arange(c)
    blocks = [(idx[None, :] <= idx[:, None]), (idx[None, :] > idx[:, None])]
    masks = []
    for lvl in range(HG_LEVELS):
        s = 1 << lvl
        mid = (idx // (2 * s)) * (2 * s) + s
        later = idx >= mid
        t = idx[None, :]
        blk = np.where(later[:, None], (t >= mid[:, None]) & (t <= idx[:, None]),
                       (t > idx[:, None]) & (t < mid[:, None]))
        blocks.append(blk)
        same_pair = (idx[:, None] // (2 * s)) == (idx[None, :] // (2 * s))
        masks.append(same_pair & later[:, None] & (~later)[None, :])
    masks.append(np.eye(c, dtype=bool))
    tmat = np.concatenate(blocks, axis=0).astype(np.float32)
    mask = np.stack(masks).astype(np.float32)
    if reverse:
        tmat = tmat.reshape(-1, c, c)[:, ::-1, ::-1].reshape(-1, c)
        mask = mask[:, ::-1, ::-1]
    return jnp.asarray(tmat, BF16), jnp.asarray(mask, F32)


def _hgrn_kernel(q_ref, ff_ref, fb_ref, v_ref, g_ref, lbf_ref, lbb_ref, gw_ref,
                 tf_ref, tb_ref, mf_ref, mb_ref, o_ref, acc_s, *, seq):
    c = HG_CHUNK
    n_chunks = seq // c

    def lower_bound(lb_ref):
        a = lb_ref[...].astype(F32)
        a0, a1 = a[0:1], a[1:2]
        m = jnp.maximum(a0, a1)
        e0, e1 = jnp.exp(a0 - m), jnp.exp(a1 - m)
        return e0 / (e0 + e1)

    def chunk(ci, state_t, f_ref, lb, t_ref, m_ref, reverse, first_pass):
        sl = pl.ds(pl.multiple_of(ci * c, c), c)
        q = q_ref[sl, :].astype(F32)
        v = v_ref[sl, :]
        hf = f_ref[sl, :].astype(F32)
        f = lb + (1.0 - lb) * (1.0 / (1.0 + jnp.exp(-hf)))
        k = 1.0 - f
        hi, lo = _split_bf16(jnp.log(f))
        tm = t_ref[...]
        e = jnp.exp(_dot(tm, hi) + _dot(tm, lo))
        e_b, e_rest = e[0:c], e[c:2 * c]
        o = _nt_dot((q * e_b).astype(BF16), state_t.astype(BF16))
        a = m_ref[HG_LEVELS] * _nt_dot(q.astype(BF16), k.astype(BF16))
        for lvl in range(HG_LEVELS):
            el = e[(2 + lvl) * c:(3 + lvl) * c]
            a = a + m_ref[lvl] * _nt_dot((q * el).astype(BF16), (k * el).astype(BF16))
        o = o + _dot(a.astype(BF16), v)
        if first_pass:
            acc_s[sl, :] = o
        else:
            acc_s[sl, :] = acc_s[sl, :] + o
        e_tot = e_b[0:1] if reverse else e_b[c - 1:c]
        kd = (k * e_rest).astype(BF16)
        return state_t * e_tot + _dot(v.T, kd)

    lbf = lower_bound(lbf_ref)
    lbb = lower_bound(lbb_ref)
    zero = jnp.zeros((HG_DK, HG_DK), F32)
    lax.fori_loop(
        0, n_chunks,
        lambda i, s: chunk(i, s, ff_ref, lbf, tf_ref, mf_ref, False, True), zero)
    lax.fori_loop(
        0, n_chunks,
        lambda i, s: chunk(n_chunks - 1 - i, s, fb_ref, lbb, tb_ref, mb_ref, True, False), zero)

    def out_body(i, carry):
        sl = pl.ds(pl.multiple_of(i * NORM_ROWS, NORM_ROWS), NORM_ROWS)
        o = acc_s[sl, :]
        ms = jnp.mean(o * o, axis=-1, keepdims=True)
        y = o * lax.rsqrt(ms + RMS_EPS) * gw_ref[...]
        g = g_ref[sl, :].astype(F32)
        o_ref[sl, :] = (y * (g * (1.0 / (1.0 + jnp.exp(-g))))).astype(o_ref.dtype)
        return carry

    lax.fori_loop(0, seq // NORM_ROWS, out_body, 0)


def _hgrn2(proj, lb_fwd, lb_bwd, hg_norm_w, batch, seq):
    assert HG_DK == V7X_LANES and lb_fwd.shape[0] == 2
    tf, mf = _hg_tables(False)
    tb, mb = _hg_tables(True)
    first = 3 * NA_WIDTH // V7X_LANES
    col = lambda k: pl.BlockSpec((seq, V7X_LANES),
                                 lambda b, h, k=k: (b, first + k * HG_HEADS + h))
    lbs = pl.BlockSpec((2, V7X_LANES), lambda b, h: (0, h))
    const = lambda a: pl.BlockSpec(a.shape, lambda b, h, nd=a.ndim: (0,) * nd)
    gw = hg_norm_w.astype(F32).reshape(1, HG_DK)
    return pl.pallas_call(
        functools.partial(_hgrn_kernel, seq=seq),
        grid=(batch, HG_HEADS),
        in_specs=[col(0), col(1), col(2), col(3), col(4), lbs, lbs, const(gw),
                  const(tf), const(tb), const(mf), const(mb)],
        out_specs=pl.BlockSpec((seq, V7X_LANES), lambda b, h: (b, h)),
        out_shape=jax.ShapeDtypeStruct((batch * seq, HG_WIDTH), BF16),
        scratch_shapes=[pltpu.VMEM((seq, HG_DK), F32)],
        compiler_params=pltpu.CompilerParams(
            dimension_semantics=("arbitrary", "arbitrary"),
            vmem_limit_bytes=V7X_VMEM_LIMIT_BYTES),
        name="hgrn2",
    )(proj, proj, proj, proj, proj, lb_fwd.astype(F32), lb_bwd.astype(F32), gw,
      tf, tb, mf, mb)


def _out_mlp_kernel(x_ref, na_ref, hg_ref, wo_na_ref, wo_hg_ref, nw_ref, wu_ref, wd_ref, o_ref):
    h = x_ref[...] + _dot(na_ref[...], wo_na_ref[...]) + _dot(hg_ref[...], wo_hg_ref[...])
    ms = jnp.mean(h * h, axis=-1, keepdims=True)
    u = (h * lax.rsqrt(ms + RMS_EPS) * nw_ref[...]).astype(BF16)
    mlp = None
    for j in range(D_FF // FF_CHUNK):
        cols = slice(j * FF_CHUNK, (j + 1) * FF_CHUNK)
        a = jnp.maximum(_dot(u, wu_ref[:, cols]), 0.0)
        part = _dot((a * a).astype(BF16), wd_ref[cols, :])
        mlp = part if mlp is None else mlp + part
    o_ref[...] = h + mlp


def _out_mlp(x2, y_na, y_hg, w_out, norm_w, w_up, w_down):
    m = x2.shape[0]
    tile = lambda width: pl.BlockSpec((TOKEN_TILE, width), lambda i: (i, 0))
    resident = lambda a: pl.BlockSpec(a.shape, lambda i: (0, 0), pipeline_mode=pl.Buffered(1))
    wo_na, wo_hg = w_out[:NA_WIDTH], w_out[NA_WIDTH:]
    return pl.pallas_call(
        _out_mlp_kernel,
        grid=(m // TOKEN_TILE,),
        in_specs=[tile(D_MODEL), tile(NA_WIDTH), tile(HG_WIDTH), resident(wo_na),
                  resident(wo_hg), pl.BlockSpec((1, D_MODEL), lambda i: (0, 0)),
                  resident(w_up), resident(w_down)],
        out_specs=tile(D_MODEL),
        out_shape=jax.ShapeDtypeStruct((m, D_MODEL), F32),
        compiler_params=pltpu.CompilerParams(
            dimension_semantics=("arbitrary",), vmem_limit_bytes=V7X_VMEM_LIMIT_BYTES),
        name="out_mlp",
    )(x2, y_na, y_hg, wo_na, wo_hg, norm_w, w_up, w_down)


def kernel(x, w_in, w_out, attn_norm_w, mlp_norm_w, q_norm_w, k_norm_w, rpb, hg_norm_w,
           lb_fwd, lb_bwd, w_up, w_down):
    batch, seq, d = x.shape
    assert d == D_MODEL and w_in.shape[0] == 1 and seq % GRID_W == 0 and seq % HG_CHUNK == 0
    x2 = x.reshape(batch * seq, d)
    proj = _in_proj(x2, attn_norm_w[0].astype(F32).reshape(1, d), w_in[0].astype(BF16))
    y_na = _natten(proj, q_norm_w[0], k_norm_w[0], rpb[0], batch, seq)
    y_hg = _hgrn2(proj, lb_fwd, lb_bwd, hg_norm_w[0], batch, seq)
    out = _out_mlp(x2, y_na, y_hg, w_out[0].astype(BF16),
                   mlp_norm_w[0].astype(F32).reshape(1, d),
                   w_up[0].astype(BF16), w_down[0].astype(BF16))
    return out.reshape(batch, seq, d)
```

```python
import functools

import numpy as np
import jax
import jax.numpy as jnp
from jax import lax
from jax.experimental import pallas as pl
from jax.experimental.pallas import tpu as pltpu

F32 = jnp.float32
BF16 = jnp.bfloat16

D_MODEL = 1024
GRID_W = 64
NA_HEADS = 8
NA_HEAD_DIM = 64
NA_WIDTH = NA_HEADS * NA_HEAD_DIM
NA_WIN_R = 8
NA_WIN_C = 16
HG_HEADS = 4
HG_DK = 128
HG_WIDTH = HG_HEADS * HG_DK
D_FF = 4 * D_MODEL
D_IN_PROJ = 3 * NA_WIDTH + 5 * HG_WIDTH
RMS_EPS = 1e-6

V7X_LANES = 128
V7X_VMEM_LIMIT_BYTES = 56 * 1024 * 1024

TOKEN_TILE = 512
FF_CHUNK = 1024
NORM_ROWS = 256
HG_CHUNK = 128
HG_LEVELS = 7
HG_CHUNKS_PER_STEP = 2
HG_OUT_CHUNKS_PER_STEP = 4
NA_QT = 16
NA_KT = 32
NA_TILES = GRID_W // NA_QT
NA_ROWS_PER_STEP = 4
MASK_NEG = -1e30

_NA_QCOLS = [list(range(0, 8)) + list(range(56, 64)),
             list(range(8, 24)), list(range(24, 40)), list(range(40, 56))]
_NA_KSEGS = [[(0, 16), (48, 16)], [(0, 32)], [(16, 32)], [(32, 32)]]


def _nt_dot(a, b):
    return lax.dot_general(a, b, (((1,), (1,)), ((), ())), preferred_element_type=F32)


def _dot(a, b):
    return jnp.dot(a, b, preferred_element_type=F32)


def _split_bf16(x):
    hi = x.astype(BF16)
    lo = (x - hi.astype(F32)).astype(BF16)
    return hi, lo


def _in_proj_kernel(x_ref, nw_ref, w_ref, o_ref):
    x = x_ref[...]
    ms = jnp.mean(x * x, axis=-1, keepdims=True)
    u = (x * lax.rsqrt(ms + RMS_EPS) * nw_ref[...]).astype(BF16)
    o_ref[...] = _dot(u, w_ref[...]).astype(o_ref.dtype)


def _in_proj(x2, norm_w, w_in):
    m = x2.shape[0]
    return pl.pallas_call(
        _in_proj_kernel,
        grid=(m // TOKEN_TILE,),
        in_specs=[
            pl.BlockSpec((TOKEN_TILE, D_MODEL), lambda i: (i, 0)),
            pl.BlockSpec((1, D_MODEL), lambda i: (0, 0)),
            pl.BlockSpec((D_MODEL, D_IN_PROJ), lambda i: (0, 0), pipeline_mode=pl.Buffered(1)),
        ],
        out_specs=pl.BlockSpec((TOKEN_TILE, D_IN_PROJ), lambda i: (i, 0)),
        out_shape=jax.ShapeDtypeStruct((m, D_IN_PROJ), BF16),
        compiler_params=pltpu.CompilerParams(
            dimension_semantics=("arbitrary",), vmem_limit_bytes=V7X_VMEM_LIMIT_BYTES),
        name="in_proj",
    )(x2, norm_w, w_in)


def _na_bias_table(rpb):
    tiles = []
    for t in range(NA_TILES):
        qc = np.array(_NA_QCOLS[t])
        kc = np.concatenate([np.arange(c0, c0 + n) for c0, n in _NA_KSEGS[t]])
        cs = np.clip(qc - NA_WIN_C // 2, 0, GRID_W - NA_WIN_C)
        valid = (kc[None, :] >= cs[:, None]) & (kc[None, :] < cs[:, None] + NA_WIN_C)
        cidx = np.clip(kc[None, :] - qc[:, None] + NA_WIN_C - 1, 0, 2 * NA_WIN_C - 2)
        tiles.append((valid, cidx))
    valid = np.stack([v for v, _ in tiles])
    cidx = np.stack([c for _, c in tiles])
    rs = np.arange(NA_WIN_R)[:, None]
    kr = np.arange(NA_WIN_R)[None, :]
    ridx = kr - rs + NA_WIN_R - 1
    b = rpb.astype(F32)[:, ridx]
    b = b[:, :, :, cidx]
    b = jnp.where(jnp.asarray(valid)[None, None, None], b, MASK_NEG)
    b = b.reshape(NA_HEADS // 2, 2, *b.shape[1:])
    b = jnp.transpose(b, (2, 0, 4, 1, 5, 3, 6))
    return b.reshape(NA_WIN_R, NA_HEADS // 2, NA_TILES, 2 * NA_QT, NA_WIN_R * NA_KT)


def _natten_kernel(q_ref, k_ref, v_ref, qw_ref, kw_ref, bias_ref, hm_ref, o_ref, qn_s, kn_s,
                   *, rows):
    seq = rows * GRID_W
    hm = hm_ref[...]

    def head_rmsnorm(x, w):
        x2 = x * x
        hi, lo = _split_bf16(x2)
        ms = _dot(hi, hm) + _dot(lo, hm)
        return x * lax.rsqrt(ms + RMS_EPS) * w

    def norm_body(i, carry):
        sl = pl.ds(pl.multiple_of(i * NORM_ROWS, NORM_ROWS), NORM_ROWS)
        qn = head_rmsnorm(q_ref[sl, :].astype(F32), qw_ref[...])
        qn_s[sl, :] = qn * (NA_HEAD_DIM ** -0.5)
        kn_s[sl, :] = head_rmsnorm(k_ref[sl, :].astype(F32), kw_ref[...]).astype(BF16)
        return carry

    lax.fori_loop(0, seq // NORM_ROWS, norm_body, 0)

    lane = lax.broadcasted_iota(jnp.int32, (NA_QT, V7X_LANES), 1)
    first_head = lane < NA_HEAD_DIM
    wr = min(NA_WIN_R, rows)

    def row_block(ref, grid_row):
        if isinstance(grid_row, int):
            return ref[grid_row * GRID_W:(grid_row + 1) * GRID_W, :]
        return ref[pl.ds(pl.multiple_of(grid_row * GRID_W, GRID_W), GRID_W), :]

    def band(blocks, t):
        return jnp.concatenate(
            [blk[c0:c0 + n] for blk in blocks for (c0, n) in _NA_KSEGS[t]], axis=0)

    def step(row0, key_row0, row_off, rel_starts):
        n_key_rows = max(rel_starts) + wr
        kblocks = [row_block(kn_s, key_row0 + u) for u in range(n_key_rows)]
        scores = []
        for j, rel in enumerate(rel_starts):
            qrow = row_block(qn_s, row0 + j)
            rs = row_off + j - rel
            for t in range(NA_TILES):
                qc = _NA_QCOLS[t]
                if t == 0:
                    qt = jnp.concatenate([qrow[0:8], qrow[56:64]], axis=0)
                else:
                    qt = qrow[qc[0]:qc[0] + NA_QT]
                q2 = jnp.concatenate([jnp.where(first_head, qt, 0.0),
                                      jnp.where(first_head, 0.0, qt)], axis=0).astype(BF16)
                kb = band(kblocks[rel:rel + wr], t)
                scores.append(_nt_dot(q2, kb) + bias_ref[rs, t])
        probs, sums = [], []
        for s in scores:
            p = jnp.exp(s - jnp.max(s, axis=-1, keepdims=True))
            sums.append(jnp.sum(p, axis=-1, keepdims=True))
            probs.append(p.astype(BF16))
        vblocks = [row_block(v_ref, key_row0 + u) for u in range(n_key_rows)]
        for j, rel in enumerate(rel_starts):
            o_tiles = []
            for t in range(NA_TILES):
                vb = band(vblocks[rel:rel + wr], t)
                o2 = _dot(probs[j * NA_TILES + t], vb) / sums[j * NA_TILES + t]
                o_tiles.append(jnp.where(first_head, o2[0:NA_QT], o2[NA_QT:2 * NA_QT]))
            orow = jnp.concatenate(
                [o_tiles[0][0:8], o_tiles[1], o_tiles[2], o_tiles[3], o_tiles[0][8:16]], axis=0)
            if isinstance(row0, int):
                osl = slice((row0 + j) * GRID_W, (row0 + j + 1) * GRID_W)
            else:
                osl = pl.ds(pl.multiple_of((row0 + j) * GRID_W, GRID_W), GRID_W)
            o_ref[osl, :] = orow.astype(o_ref.dtype)

    u = NA_ROWS_PER_STEP
    assert rows % u == 0
    starts = [[min(max(i * u + j - wr // 2, 0), rows - wr) for j in range(u)]
              for i in range(rows // u)]
    regular = [st == [i * u + j - wr // 2 for j in range(u)] for i, st in enumerate(starts)]
    lo = regular.index(True)
    hi = len(regular) - regular[::-1].index(True)
    assert all(regular[lo:hi])

    def peeled(i):
        step(i * u, starts[i][0], i * u - starts[i][0], [s - starts[i][0] for s in starts[i]])

    for i in range(lo):
        peeled(i)

    def regular_body(i, carry):
        step(i * u, i * u - wr // 2, wr // 2, list(range(u)))
        return carry

    lax.fori_loop(lo, hi, regular_body, 0)
    for i in range(hi, rows // u):
        peeled(i)


def _natten(proj, q_norm_w, k_norm_w, rpb, batch, seq):
    rows = seq // GRID_W
    assert rows >= NA_WIN_R and GRID_W == 64 and NA_HEAD_DIM * 2 == V7X_LANES
    pairs = NA_HEADS // 2
    bias = _na_bias_table(rpb)
    qw = jnp.tile(q_norm_w.astype(F32).reshape(1, NA_HEAD_DIM), (1, 2))
    kw = jnp.tile(k_norm_w.astype(F32).reshape(1, NA_HEAD_DIM), (1, 2))
    head_id = np.arange(V7X_LANES) // NA_HEAD_DIM
    hm = jnp.asarray((head_id[:, None] == head_id[None, :]) / NA_HEAD_DIM, BF16)
    col = lambda off: pl.BlockSpec((seq, V7X_LANES), lambda b, p, off=off: (b, off + p))
    const2 = lambda shape: pl.BlockSpec(shape, lambda b, p: (0, 0))
    return pl.pallas_call(
        functools.partial(_natten_kernel, rows=rows),
        grid=(batch, pairs),
        in_specs=[
            col(0), col(pairs), col(2 * pairs),
            const2((1, V7X_LANES)), const2((1, V7X_LANES)),
            pl.BlockSpec((NA_WIN_R, None, NA_TILES, 2 * NA_QT, NA_WIN_R * NA_KT),
                         lambda b, p: (0, p, 0, 0, 0)),
            const2((V7X_LANES, V7X_LANES)),
        ],
        out_specs=pl.BlockSpec((seq, V7X_LANES), lambda b, p: (b, p)),
        out_shape=jax.ShapeDtypeStruct((batch * seq, NA_WIDTH), BF16),
        scratch_shapes=[pltpu.VMEM((seq, V7X_LANES), F32), pltpu.VMEM((seq, V7X_LANES), BF16)],
        compiler_params=pltpu.CompilerParams(
            dimension_semantics=("arbitrary", "arbitrary"),
            vmem_limit_bytes=V7X_VMEM_LIMIT_BYTES),
        name="natten",
    )(proj, proj, proj, qw, kw, bias, hm)


def _hg_tables():
    c = HG_CHUNK
    idx = np.arange(c)
    lower = (idx[None, :] <= idx[:, None]).astype(np.float32)
    cums = np.stack([np.concatenate([lower, lower], axis=1),
                     np.concatenate([lower.T, lower.T], axis=1)])
    masks = []
    for lvl in range(HG_LEVELS):
        s = 1 << lvl
        same_block = (idx[:, None] // (2 * s)) == (idx[None, :] // (2 * s))
        half = (idx // s) % 2
        masks.append(same_block & (half[:, None] != half[None, :]))
    masks.append(np.eye(c, dtype=bool))
    return jnp.asarray(cums, BF16), jnp.asarray(np.stack(masks), F32)


def _hgrn_kernel(q_ref, ff_ref, fb_ref, v_ref, g_ref, lbf_ref, lbb_ref, gw_ref,
                 cum_ref, mask_ref, o_ref, acc_s, qb_s, inc_s, etb_s, *, seq):
    c = HG_CHUNK
    n_chunks = seq // c
    n_step = HG_CHUNKS_PER_STEP
    assert n_chunks % n_step == 0
    sub = lax.broadcasted_iota(jnp.int32, (c, V7X_LANES), 0) & 7

    def lower_bound(lb_ref):
        a = lb_ref[...].astype(F32)
        a0, a1 = a[0:1], a[1:2]
        m = jnp.maximum(a0, a1)
        e0, e1 = jnp.exp(a0 - m), jnp.exp(a1 - m)
        return e0 / (e0 + e1)

    lbf = lower_bound(lbf_ref)
    lbb = lower_bound(lbb_ref)

    def rows(ci):
        return pl.ds(pl.multiple_of(ci * c, c), c)

    def group_rows(x, r):
        return jnp.concatenate(
            [jnp.broadcast_to(x[8 * j + r:8 * j + r + 1], (8, V7X_LANES)) for j in range(c // 8)],
            axis=0)

    def repeat8(x8, n_rows):
        return x8 if n_rows == 8 else jnp.concatenate([x8] * (n_rows // 8), axis=0)

    def gates(f_ref, sl, lb):
        hf = f_ref[sl, :].astype(F32)
        f = lb + (1.0 - lb) * (1.0 / (1.0 + jnp.exp(-hf)))
        hi, lo = _split_bf16(jnp.log2(f))
        return f, 1.0 - f, jnp.concatenate([hi, lo], axis=0)

    def stage_gates(ci):
        sl = rows(ci)
        st = dict(sl=sl, q=q_ref[sl, :].astype(F32), v=v_ref[sl, :])
        st["f_f"], st["k_f"], lf_f = gates(ff_ref, sl, lbf)
        st["f_b"], st["k_b"], lf_b = gates(fb_ref, sl, lbb)
        st["b_f"] = _dot(cum_ref[0], lf_f)
        st["b_b"] = _dot(cum_ref[1], lf_b)
        return st

    def stage_operands(ci, st):
        q, k_f, k_b, b_f, b_b = st["q"], st["k_f"], st["k_b"], st["b_f"], st["b_b"]
        last_f = group_rows(b_f, 7)
        first_b = group_rows(b_b, 0)
        odd = (sub & 1) == 1
        xs = [q * jnp.where(odd, st["f_f"], st["f_b"])]
        ys = [jnp.where(odd, k_b, k_f)]
        for lvl in (1, 2):
            later = (sub & (1 << lvl)) != 0
            if lvl == 1:
                ref_f = jnp.where(sub < 4, group_rows(b_f, 1), group_rows(b_f, 5))
                ref_b = jnp.where(sub < 4, group_rows(b_b, 2), group_rows(b_b, 6))
            else:
                ref_f, ref_b = group_rows(b_f, 3), group_rows(b_b, 4)
            d_f, d_b = b_f - ref_f, b_b - ref_b
            xs.append(q * jnp.exp2(jnp.where(later, d_f, d_b)))
            ys.append(jnp.where(later, k_b, k_f) * jnp.exp2(-jnp.where(later, d_b, d_f)))
        for lvl in range(3, HG_LEVELS):
            s = 1 << lvl
            xp, yp = [], []
            for r0 in range(0, c, 2 * s):
                m = r0 + s
                rf = repeat8(last_f[m - 8:m], s)
                rb = repeat8(first_b[m:m + 8], s)
                early, late = slice(r0, m), slice(m, m + s)
                xp += [q[early] * jnp.exp2(b_b[early] - rb), q[late] * jnp.exp2(b_f[late] - rf)]
                yp += [k_f[early] * jnp.exp2(rf - b_f[early]), k_b[late] * jnp.exp2(rb - b_b[late])]
            xs.append(jnp.concatenate(xp, axis=0))
            ys.append(jnp.concatenate(yp, axis=0))
        xs.append(q)
        ys.append(k_f + k_b)
        st["xs"] = [x.astype(BF16) for x in xs]
        st["ys"] = [y.astype(BF16) for y in ys]
        sl = st["sl"]
        tot_f = last_f[c - 8:c]
        tot_b = first_b[0:8]
        st["qe_f"] = (q * jnp.exp2(b_f)).astype(BF16)
        st["kd_f"] = (k_f * jnp.exp2(repeat8(tot_f, c) - b_f)).astype(BF16)
        st["tot_f"] = jnp.exp2(tot_f[0:1])
        qb_s[sl, :] = (q * jnp.exp2(b_b)).astype(BF16)
        st["kd_b"] = (k_b * jnp.exp2(repeat8(tot_b, c) - b_b)).astype(BF16)
        etb_s[pl.ds(pl.multiple_of(ci * 8, 8), 8), :] = jnp.exp2(tot_b)
        return st

    def stage_scores(st):
        a = None
        for lvl in range(HG_LEVELS + 1):
            term = mask_ref[lvl] * _nt_dot(st["xs"][lvl], st["ys"][lvl])
            a = term if a is None else a + term
        st["a"] = a.astype(BF16)
        return st

    def fwd_body(i, state_t):
        cis = [i * n_step + j for j in range(n_step)]
        sts = [stage_gates(ci) for ci in cis]
        sts = [stage_operands(ci, st) for ci, st in zip(cis, sts)]
        sts = [stage_scores(st) for st in sts]
        intra = [_dot(st["a"], st["v"]) for st in sts]
        incs = [_dot(st["v"].T, jnp.concatenate([st["kd_f"], st["kd_b"]], axis=1)) for st in sts]
        for st, o, inc in zip(sts, intra, incs):
            acc_s[st["sl"], :] = o + _nt_dot(st["qe_f"], state_t.astype(BF16))
            state_t = state_t * st["tot_f"] + inc[:, 0:HG_DK]
            inc_s[st["sl"], :] = inc[:, HG_DK:2 * HG_DK]
        return state_t

    zero = jnp.zeros((HG_DK, HG_DK), F32)
    lax.fori_loop(0, n_chunks // n_step, fwd_body, zero)

    n_out = HG_OUT_CHUNKS_PER_STEP
    assert n_chunks % n_out == 0

    def bwd_body(i, state_t):
        cis = [n_chunks - 1 - (i * n_out + j) for j in range(n_out)]
        inter = []
        for ci in cis:
            inter.append(_nt_dot(qb_s[rows(ci), :], state_t.astype(BF16)))
            tot_b = etb_s[pl.ds(pl.multiple_of(ci * 8, 8), 1), :]
            state_t = state_t * tot_b + inc_s[rows(ci), :]
        for ci, o_inter in zip(cis, inter):
            sl = rows(ci)
            o = acc_s[sl, :] + o_inter
            ms = jnp.mean(o * o, axis=-1, keepdims=True)
            y = o * lax.rsqrt(ms + RMS_EPS) * gw_ref[...]
            g = g_ref[sl, :].astype(F32)
            o_ref[sl, :] = (y * (g * (1.0 / (1.0 + jnp.exp(-g))))).astype(o_ref.dtype)
        return state_t

    lax.fori_loop(0, n_chunks // n_out, bwd_body, zero)


def _hgrn2(proj, lb_fwd, lb_bwd, hg_norm_w, batch, seq):
    assert HG_DK == V7X_LANES and lb_fwd.shape[0] == 2
    cums, masks = _hg_tables()
    first = 3 * NA_WIDTH // V7X_LANES
    col = lambda k: pl.BlockSpec((seq, V7X_LANES),
                                 lambda b, h, k=k: (b, first + k * HG_HEADS + h))
    lbs = pl.BlockSpec((2, V7X_LANES), lambda b, h: (0, h))
    const = lambda a: pl.BlockSpec(a.shape, lambda b, h, nd=a.ndim: (0,) * nd)
    gw = hg_norm_w.astype(F32).reshape(1, HG_DK)
    return pl.pallas_call(
        functools.partial(_hgrn_kernel, seq=seq),
        grid=(batch, HG_HEADS),
        in_specs=[col(0), col(1), col(2), col(3), col(4), lbs, lbs, const(gw),
                  const(cums), const(masks)],
        out_specs=pl.BlockSpec((seq, V7X_LANES), lambda b, h: (b, h)),
        out_shape=jax.ShapeDtypeStruct((batch * seq, HG_WIDTH), BF16),
        scratch_shapes=[pltpu.VMEM((seq, HG_DK), F32), pltpu.VMEM((seq, HG_DK), BF16),
                        pltpu.VMEM((seq, HG_DK), F32),
                        pltpu.VMEM((8 * seq // HG_CHUNK, HG_DK), F32)],
        compiler_params=pltpu.CompilerParams(
            dimension_semantics=("arbitrary", "arbitrary"),
            vmem_limit_bytes=V7X_VMEM_LIMIT_BYTES),
        name="hgrn2",
    )(proj, proj, proj, proj, proj, lb_fwd.astype(F32), lb_bwd.astype(F32), gw, cums, masks)


def _out_mlp_kernel(x_ref, na_ref, hg_ref, wo_na_ref, wo_hg_ref, nw_ref, wu_ref, wd_ref, o_ref):
    h = x_ref[...] + _dot(na_ref[...], wo_na_ref[...]) + _dot(hg_ref[...], wo_hg_ref[...])
    ms = jnp.mean(h * h, axis=-1, keepdims=True)
    u = (h * lax.rsqrt(ms + RMS_EPS) * nw_ref[...]).astype(BF16)
    mlp = None
    for j in range(D_FF // FF_CHUNK):
        cols = slice(j * FF_CHUNK, (j + 1) * FF_CHUNK)
        a = jnp.maximum(_dot(u, wu_ref[:, cols]), 0.0)
        part = _dot((a * a).astype(BF16), wd_ref[cols, :])
        mlp = part if mlp is None else mlp + part
    o_ref[...] = h + mlp


def _out_mlp(x2, y_na, y_hg, w_out, norm_w, w_up, w_down):
    m = x2.shape[0]
    tile = lambda width: pl.BlockSpec((TOKEN_TILE, width), lambda i: (i, 0))
    resident = lambda a: pl.BlockSpec(a.shape, lambda i: (0, 0), pipeline_mode=pl.Buffered(1))
    wo_na, wo_hg = w_out[:NA_WIDTH], w_out[NA_WIDTH:]
    return pl.pallas_call(
        _out_mlp_kernel,
        grid=(m // TOKEN_TILE,),
        in_specs=[tile(D_MODEL), tile(NA_WIDTH), tile(HG_WIDTH), resident(wo_na),
                  resident(wo_hg), pl.BlockSpec((1, D_MODEL), lambda i: (0, 0)),
                  resident(w_up), resident(w_down)],
        out_specs=tile(D_MODEL),
        out_shape=jax.ShapeDtypeStruct((m, D_MODEL), F32),
        compiler_params=pltpu.CompilerParams(
            dimension_semantics=("arbitrary",), vmem_limit_bytes=V7X_VMEM_LIMIT_BYTES),
        name="out_mlp",
    )(x2, y_na, y_hg, wo_na, wo_hg, norm_w, w_up, w_down)


def kernel(x, w_in, w_out, attn_norm_w, mlp_norm_w, q_norm_w, k_norm_w, rpb, hg_norm_w,
           lb_fwd, lb_bwd, w_up, w_down):
    batch, seq, d = x.shape
    assert d == D_MODEL and w_in.shape[0] == 1 and seq % GRID_W == 0 and seq % HG_CHUNK == 0
    x2 = x.reshape(batch * seq, d)
    proj = _in_proj(x2, attn_norm_w[0].astype(F32).reshape(1, d), w_in[0].astype(BF16))
    y_na = _natten(proj, q_norm_w[0], k_norm_w[0], rpb[0], batch, seq)
    y_hg = _hgrn2(proj, lb_fwd, lb_bwd, hg_norm_w[0], batch, seq)
    out = _out_mlp(x2, y_na, y_hg, w_out[0].astype(BF16),
                   mlp_norm_w[0].astype(F32).reshape(1, d),
                   w_up[0].astype(BF16), w_down[0].astype(BF16))
    return out.reshape(batch, seq, d)
```

```python
import functools

import numpy as np
import jax
import jax.numpy as jnp
from jax import lax
from jax.experimental import pallas as pl
from jax.experimental.pallas import tpu as pltpu

F32 = jnp.float32
BF16 = jnp.bfloat16

D_MODEL = 1024
GRID_W = 64
NA_HEADS = 8
NA_HEAD_DIM = 64
NA_WIDTH = NA_HEADS * NA_HEAD_DIM
NA_WIN_R = 8
NA_WIN_C = 16
HG_HEADS = 4
HG_DK = 128
HG_WIDTH = HG_HEADS * HG_DK
D_FF = 4 * D_MODEL
D_IN_PROJ = 3 * NA_WIDTH + 5 * HG_WIDTH
RMS_EPS = 1e-6

V7X_LANES = 128
V7X_VMEM_LIMIT_BYTES = 56 * 1024 * 1024

TOKEN_TILE = 512
FF_CHUNK = 1024
NORM_ROWS = 512
HG_CHUNK = 128
HG_LEVELS = 7
HG_CHUNKS_PER_STEP = 8
HG_OUT_CHUNKS_PER_STEP = 4
NA_QT = 16
NA_KT = 32
NA_TILES = GRID_W // NA_QT
NA_ROWS_PER_STEP = 8
MASK_NEG = -1e30

_NA_QCOLS = [list(range(0, 8)) + list(range(56, 64)),
             list(range(8, 24)), list(range(24, 40)), list(range(40, 56))]
_NA_KSEGS = [[(0, 16), (48, 16)], [(0, 32)], [(16, 32)], [(32, 32)]]


def _nt_dot(a, b):
    return lax.dot_general(a, b, (((1,), (1,)), ((), ())), preferred_element_type=F32)


def _dot(a, b):
    return jnp.dot(a, b, preferred_element_type=F32)


def _split_bf16(x):
    hi = x.astype(BF16)
    lo = (x - hi.astype(F32)).astype(BF16)
    return hi, lo


def _in_proj_kernel(x_ref, nw_ref, w_ref, o_ref):
    x = x_ref[...]
    ms = jnp.mean(x * x, axis=-1, keepdims=True)
    u = (x * lax.rsqrt(ms + RMS_EPS) * nw_ref[...]).astype(BF16)
    o_ref[...] = _dot(u, w_ref[...]).astype(o_ref.dtype)


def _in_proj(x2, norm_w, w_in):
    m = x2.shape[0]
    return pl.pallas_call(
        _in_proj_kernel,
        grid=(m // TOKEN_TILE,),
        in_specs=[
            pl.BlockSpec((TOKEN_TILE, D_MODEL), lambda i: (i, 0)),
            pl.BlockSpec((1, D_MODEL), lambda i: (0, 0)),
            pl.BlockSpec((D_MODEL, D_IN_PROJ), lambda i: (0, 0), pipeline_mode=pl.Buffered(1)),
        ],
        out_specs=pl.BlockSpec((TOKEN_TILE, D_IN_PROJ), lambda i: (i, 0)),
        out_shape=jax.ShapeDtypeStruct((m, D_IN_PROJ), BF16),
        compiler_params=pltpu.CompilerParams(
            dimension_semantics=("arbitrary",), vmem_limit_bytes=V7X_VMEM_LIMIT_BYTES),
        name="in_proj",
    )(x2, norm_w, w_in)


def _na_bias_table(rpb):
    n_ri, n_ci = 2 * NA_WIN_R - 1, 2 * NA_WIN_C - 1
    masked = NA_HEADS * n_ri * n_ci
    idx = np.empty((NA_HEADS // 2, NA_TILES, 2, NA_QT, n_ri, NA_KT), np.int32)
    for t in range(NA_TILES):
        qc = np.array(_NA_QCOLS[t])
        kc = np.concatenate([np.arange(c0, c0 + n) for c0, n in _NA_KSEGS[t]])
        cs = np.clip(qc - NA_WIN_C // 2, 0, GRID_W - NA_WIN_C)
        valid = (kc[None, :] >= cs[:, None]) & (kc[None, :] < cs[:, None] + NA_WIN_C)
        cidx = np.clip(kc[None, :] - qc[:, None] + NA_WIN_C - 1, 0, n_ci - 1)
        for h in range(NA_HEADS):
            flat = (h * n_ri + np.arange(n_ri)[None, :, None]) * n_ci + cidx[:, None, :]
            idx[h // 2, t, h % 2] = np.where(valid[:, None, :], flat, masked)
    src = jnp.concatenate([rpb.astype(F32).reshape(-1), jnp.full((1,), MASK_NEG, F32)])
    cols = jnp.take(src, jnp.asarray(idx), axis=0)
    tabs = [cols[..., NA_WIN_R - 1 - rs:2 * NA_WIN_R - 1 - rs, :].reshape(
        NA_HEADS // 2, NA_TILES, 2 * NA_QT, NA_WIN_R * NA_KT) for rs in range(NA_WIN_R)]
    return jnp.stack(tabs)


def _natten_kernel(q_ref, k_ref, v_ref, qw_ref, kw_ref, bias_ref, hm_ref, o_ref, qn_s, kn_s,
                   *, rows):
    seq = rows * GRID_W
    hm = hm_ref[...]

    def head_rmsnorm(x, w):
        hi, lo = _split_bf16(x * x)
        ms = _dot(jnp.concatenate([hi, lo], axis=1), hm)
        return x * lax.rsqrt(ms + RMS_EPS) * w

    def norm_body(i, carry):
        sl = pl.ds(pl.multiple_of(i * NORM_ROWS, NORM_ROWS), NORM_ROWS)
        qn = head_rmsnorm(q_ref[sl, :].astype(F32), qw_ref[...])
        qn_s[sl, :] = qn * (NA_HEAD_DIM ** -0.5)
        kn_s[sl, :] = head_rmsnorm(k_ref[sl, :].astype(F32), kw_ref[...]).astype(BF16)
        return carry

    lax.fori_loop(0, seq // NORM_ROWS, norm_body, 0)

    lane = lax.broadcasted_iota(jnp.int32, (NA_QT, V7X_LANES), 1)
    first_head = lane < NA_HEAD_DIM
    wr = min(NA_WIN_R, rows)

    def row_block(ref, grid_row):
        if isinstance(grid_row, int):
            return ref[grid_row * GRID_W:(grid_row + 1) * GRID_W, :]
        return ref[pl.ds(pl.multiple_of(grid_row * GRID_W, GRID_W), GRID_W), :]

    def band(blocks, t):
        return jnp.concatenate(
            [blk[c0:c0 + n] for blk in blocks for (c0, n) in _NA_KSEGS[t]], axis=0)

    def step(row0, key_row0, row_off, rel_starts):
        n_key_rows = max(rel_starts) + wr
        groups = [(rel, [j for j, r in enumerate(rel_starts) if r == rel])
                  for rel in sorted(set(rel_starts))]
        kblocks = [row_block(kn_s, key_row0 + u) for u in range(n_key_rows)]
        qrows = [row_block(qn_s, row0 + j) for j in range(len(rel_starts))]
        scores = []
        for rel, js in groups:
            for t in range(NA_TILES):
                qc = _NA_QCOLS[t]
                q2, bias = [], []
                for j in js:
                    if t == 0:
                        qt = jnp.concatenate([qrows[j][0:8], qrows[j][56:64]], axis=0)
                    else:
                        qt = qrows[j][qc[0]:qc[0] + NA_QT]
                    q2 += [jnp.where(first_head, qt, 0.0), jnp.where(first_head, 0.0, qt)]
                    bias.append(bias_ref[row_off + j - rel, t])
                kb = band(kblocks[rel:rel + wr], t)
                s = _nt_dot(jnp.concatenate(q2, axis=0).astype(BF16), kb)
                scores.append(s + (bias[0] if len(js) == 1 else jnp.concatenate(bias, axis=0)))
        probs, sums = [], []
        for s in scores:
            p = jnp.exp(s - jnp.max(s, axis=-1, keepdims=True))
            sums.append(jnp.sum(p, axis=-1, keepdims=True))
            probs.append(p.astype(BF16))
        vblocks = [row_block(v_ref, key_row0 + u) for u in range(n_key_rows)]
        o_tiles = {}
        for g, (rel, js) in enumerate(groups):
            for t in range(NA_TILES):
                vb = band(vblocks[rel:rel + wr], t)
                o2 = _dot(probs[g * NA_TILES + t], vb) / sums[g * NA_TILES + t]
                for n, j in enumerate(js):
                    o_j = o2[2 * NA_QT * n:2 * NA_QT * (n + 1)]
                    o_tiles[j, t] = jnp.where(first_head, o_j[0:NA_QT], o_j[NA_QT:2 * NA_QT])
        for j in range(len(rel_starts)):
            orow = jnp.concatenate([o_tiles[j, 0][0:8], o_tiles[j, 1], o_tiles[j, 2],
                                    o_tiles[j, 3], o_tiles[j, 0][8:16]], axis=0)
            if isinstance(row0, int):
                osl = slice((row0 + j) * GRID_W, (row0 + j + 1) * GRID_W)
            else:
                osl = pl.ds(pl.multiple_of((row0 + j) * GRID_W, GRID_W), GRID_W)
            o_ref[osl, :] = orow.astype(o_ref.dtype)

    u = NA_ROWS_PER_STEP
    assert rows % u == 0
    starts = [[min(max(i * u + j - wr // 2, 0), rows - wr) for j in range(u)]
              for i in range(rows // u)]
    regular = [st == [i * u + j - wr // 2 for j in range(u)] for i, st in enumerate(starts)]
    lo = regular.index(True)
    hi = len(regular) - regular[::-1].index(True)
    assert all(regular[lo:hi])

    def peeled(i):
        step(i * u, starts[i][0], i * u - starts[i][0], [s - starts[i][0] for s in starts[i]])

    for i in range(lo):
        peeled(i)

    def regular_body(i, carry):
        step(i * u, i * u - wr // 2, wr // 2, list(range(u)))
        return carry

    lax.fori_loop(lo, hi, regular_body, 0)
    for i in range(hi, rows // u):
        peeled(i)


def _natten(proj, q_norm_w, k_norm_w, rpb, batch, seq):
    rows = seq // GRID_W
    assert rows >= NA_WIN_R and GRID_W == 64 and NA_HEAD_DIM * 2 == V7X_LANES
    pairs = NA_HEADS // 2
    bias = _na_bias_table(rpb)
    qw = jnp.tile(q_norm_w.astype(F32).reshape(1, NA_HEAD_DIM), (1, 2))
    kw = jnp.tile(k_norm_w.astype(F32).reshape(1, NA_HEAD_DIM), (1, 2))
    head_id = np.arange(V7X_LANES) // NA_HEAD_DIM
    hm = np.tile((head_id[:, None] == head_id[None, :]) / NA_HEAD_DIM, (2, 1))
    hm = jnp.asarray(hm, BF16)
    col = lambda off: pl.BlockSpec((seq, V7X_LANES), lambda b, p, off=off: (b, off + p))
    const2 = lambda shape: pl.BlockSpec(shape, lambda b, p: (0, 0))
    return pl.pallas_call(
        functools.partial(_natten_kernel, rows=rows),
        grid=(batch, pairs),
        in_specs=[
            col(0), col(pairs), col(2 * pairs),
            const2((1, V7X_LANES)), const2((1, V7X_LANES)),
            pl.BlockSpec((NA_WIN_R, None, NA_TILES, 2 * NA_QT, NA_WIN_R * NA_KT),
                         lambda b, p: (0, p, 0, 0, 0)),
            const2((2 * V7X_LANES, V7X_LANES)),
        ],
        out_specs=pl.BlockSpec((seq, V7X_LANES), lambda b, p: (b, p)),
        out_shape=jax.ShapeDtypeStruct((batch * seq, NA_WIDTH), BF16),
        scratch_shapes=[pltpu.VMEM((seq, V7X_LANES), F32), pltpu.VMEM((seq, V7X_LANES), BF16)],
        compiler_params=pltpu.CompilerParams(
            dimension_semantics=("arbitrary", "arbitrary"),
            vmem_limit_bytes=V7X_VMEM_LIMIT_BYTES),
        name="natten",
    )(proj, proj, proj, qw, kw, bias, hm)


def _hg_tables():
    c = HG_CHUNK
    idx = np.arange(c)
    lower = (idx[None, :] <= idx[:, None]).astype(np.float32)
    cums = np.stack([np.concatenate([lower, lower], axis=1),
                     np.concatenate([lower.T, lower.T], axis=1)])
    masks = []
    for lvl in range(HG_LEVELS):
        s = 1 << lvl
        same_block = (idx[:, None] // (2 * s)) == (idx[None, :] // (2 * s))
        half = (idx // s) % 2
        masks.append(same_block & (half[:, None] != half[None, :]))
    masks.append(np.eye(c, dtype=bool))
    return jnp.asarray(cums, BF16), jnp.asarray(np.stack(masks), F32)


def _hgrn_kernel(q_ref, ff_ref, fb_ref, v_ref, g_ref, lbf_ref, lbb_ref, gw_ref,
                 cum_ref, mask_ref, o_ref, acc_s, qb_s, inc_s, etb_s, *, seq):
    c = HG_CHUNK
    n_chunks = seq // c
    n_step = HG_CHUNKS_PER_STEP
    assert n_chunks % n_step == 0
    sub = lax.broadcasted_iota(jnp.int32, (c, V7X_LANES), 0) & 7

    def lower_bound(lb_ref):
        a = lb_ref[...].astype(F32)
        a0, a1 = a[0:1], a[1:2]
        m = jnp.maximum(a0, a1)
        e0, e1 = jnp.exp(a0 - m), jnp.exp(a1 - m)
        return e0 / (e0 + e1)

    lbf = lower_bound(lbf_ref)
    lbb = lower_bound(lbb_ref)

    def rows(ci):
        return pl.ds(pl.multiple_of(ci * c, c), c)

    def group_rows(x, r):
        return jnp.concatenate(
            [jnp.broadcast_to(x[8 * j + r:8 * j + r + 1], (8, V7X_LANES)) for j in range(c // 8)],
            axis=0)

    def repeat8(x8, n_rows):
        return x8 if n_rows == 8 else jnp.concatenate([x8] * (n_rows // 8), axis=0)

    def gates(f_ref, sl, lb):
        hf = f_ref[sl, :].astype(F32)
        f = lb + (1.0 - lb) * (1.0 / (1.0 + jnp.exp(-hf)))
        hi, lo = _split_bf16(jnp.log2(f))
        return f, 1.0 - f, jnp.concatenate([hi, lo], axis=0)

    def stage_gates(ci):
        sl = rows(ci)
        st = dict(sl=sl, q=q_ref[sl, :].astype(F32), v=v_ref[sl, :])
        st["f_f"], st["k_f"], lf_f = gates(ff_ref, sl, lbf)
        st["f_b"], st["k_b"], lf_b = gates(fb_ref, sl, lbb)
        st["b_f"] = _dot(cum_ref[0], lf_f)
        st["b_b"] = _dot(cum_ref[1], lf_b)
        return st

    def stage_operands(ci, st):
        q, k_f, k_b, b_f, b_b = st["q"], st["k_f"], st["k_b"], st["b_f"], st["b_b"]
        last_f = group_rows(b_f, 7)
        first_b = group_rows(b_b, 0)
        odd = (sub & 1) == 1
        xs = [q * jnp.where(odd, st["f_f"], st["f_b"])]
        ys = [jnp.where(odd, k_b, k_f)]
        for lvl in (1, 2):
            later = (sub & (1 << lvl)) != 0
            if lvl == 1:
                ref_f = jnp.where(sub < 4, group_rows(b_f, 1), group_rows(b_f, 5))
                ref_b = jnp.where(sub < 4, group_rows(b_b, 2), group_rows(b_b, 6))
            else:
                ref_f, ref_b = group_rows(b_f, 3), group_rows(b_b, 4)
            d_f, d_b = b_f - ref_f, b_b - ref_b
            xs.append(q * jnp.exp2(jnp.where(later, d_f, d_b)))
            ys.append(jnp.where(later, k_b, k_f) * jnp.exp2(-jnp.where(later, d_b, d_f)))
        for lvl in range(3, HG_LEVELS):
            s = 1 << lvl
            xp, yp = [], []
            for r0 in range(0, c, 2 * s):
                m = r0 + s
                rf = repeat8(last_f[m - 8:m], s)
                rb = repeat8(first_b[m:m + 8], s)
                early, late = slice(r0, m), slice(m, m + s)
                xp += [q[early] * jnp.exp2(b_b[early] - rb), q[late] * jnp.exp2(b_f[late] - rf)]
                yp += [k_f[early] * jnp.exp2(rf - b_f[early]), k_b[late] * jnp.exp2(rb - b_b[late])]
            xs.append(jnp.concatenate(xp, axis=0))
            ys.append(jnp.concatenate(yp, axis=0))
        xs.append(q)
        ys.append(k_f + k_b)
        st["xs"] = [x.astype(BF16) for x in xs]
        st["ys"] = [y.astype(BF16) for y in ys]
        sl = st["sl"]
        tot_f = last_f[c - 8:c]
        tot_b = first_b[0:8]
        st["qe_f"] = (q * jnp.exp2(b_f)).astype(BF16)
        st["kd_f"] = (k_f * jnp.exp2(repeat8(tot_f, c) - b_f)).astype(BF16)
        st["tot_f"] = jnp.exp2(tot_f[0:1])
        qb_s[sl, :] = (q * jnp.exp2(b_b)).astype(BF16)
        st["kd_b"] = (k_b * jnp.exp2(repeat8(tot_b, c) - b_b)).astype(BF16)
        etb_s[pl.ds(pl.multiple_of(ci * 8, 8), 8), :] = jnp.exp2(tot_b)
        return st

    def stage_scores(st):
        a = None
        for lvl in range(HG_LEVELS + 1):
            term = mask_ref[lvl] * _nt_dot(st["xs"][lvl], st["ys"][lvl])
            a = term if a is None else a + term
        st["a"] = a.astype(BF16)
        return st

    def fwd_body(i, state_t):
        cis = [i * n_step + j for j in range(n_step)]
        sts = [stage_gates(ci) for ci in cis]
        sts = [stage_operands(ci, st) for ci, st in zip(cis, sts)]
        sts = [stage_scores(st) for st in sts]
        intra = [_dot(st["a"], st["v"]) for st in sts]
        incs = [_dot(st["v"].T, jnp.concatenate([st["kd_f"], st["kd_b"]], axis=1)) for st in sts]
        for st, o, inc in zip(sts, intra, incs):
            acc_s[st["sl"], :] = o + _nt_dot(st["qe_f"], state_t.astype(BF16))
            state_t = state_t * st["tot_f"] + inc[:, 0:HG_DK]
            inc_s[st["sl"], :] = inc[:, HG_DK:2 * HG_DK]
        return state_t

    zero = jnp.zeros((HG_DK, HG_DK), F32)
    lax.fori_loop(0, n_chunks // n_step, fwd_body, zero)

    n_out = HG_OUT_CHUNKS_PER_STEP
    assert n_chunks % n_out == 0

    def bwd_body(i, state_t):
        cis = [n_chunks - 1 - (i * n_out + j) for j in range(n_out)]
        inter = []
        for ci in cis:
            inter.append(_nt_dot(qb_s[rows(ci), :], state_t.astype(BF16)))
            tot_b = etb_s[pl.ds(pl.multiple_of(ci * 8, 8), 1), :]
            state_t = state_t * tot_b + inc_s[rows(ci), :]
        for ci, o_inter in zip(cis, inter):
            sl = rows(ci)
            o = acc_s[sl, :] + o_inter
            ms = jnp.mean(o * o, axis=-1, keepdims=True)
            y = o * lax.rsqrt(ms + RMS_EPS) * gw_ref[...]
            g = g_ref[sl, :].astype(F32)
            o_ref[sl, :] = (y * (g * (1.0 / (1.0 + jnp.exp(-g))))).astype(o_ref.dtype)
        return state_t

    lax.fori_loop(0, n_chunks // n_out, bwd_body, zero)


def _hgrn2(proj, lb_fwd, lb_bwd, hg_norm_w, batch, seq):
    assert HG_DK == V7X_LANES and lb_fwd.shape[0] == 2
    cums, masks = _hg_tables()
    first = 3 * NA_WIDTH // V7X_LANES
    col = lambda k: pl.BlockSpec((seq, V7X_LANES),
                                 lambda b, h, k=k: (b, first + k * HG_HEADS + h))
    lbs = pl.BlockSpec((2, V7X_LANES), lambda b, h: (0, h))
    const = lambda a: pl.BlockSpec(a.shape, lambda b, h, nd=a.ndim: (0,) * nd)
    gw = hg_norm_w.astype(F32).reshape(1, HG_DK)
    return pl.pallas_call(
        functools.partial(_hgrn_kernel, seq=seq),
        grid=(batch, HG_HEADS),
        in_specs=[col(0), col(1), col(2), col(3), col(4), lbs, lbs, const(gw),
                  const(cums), const(masks)],
        out_specs=pl.BlockSpec((seq, V7X_LANES), lambda b, h: (b, h)),
        out_shape=jax.ShapeDtypeStruct((batch * seq, HG_WIDTH), BF16),
        scratch_shapes=[pltpu.VMEM((seq, HG_DK), F32), pltpu.VMEM((seq, HG_DK), BF16),
                        pltpu.VMEM((seq, HG_DK), F32),
                        pltpu.VMEM((8 * seq // HG_CHUNK, HG_DK), F32)],
        compiler_params=pltpu.CompilerParams(
            dimension_semantics=("arbitrary", "arbitrary"),
            vmem_limit_bytes=V7X_VMEM_LIMIT_BYTES),
        name="hgrn2",
    )(proj, proj, proj, proj, proj, lb_fwd.astype(F32), lb_bwd.astype(F32), gw, cums, masks)


def _out_mlp_kernel(x_ref, na_ref, hg_ref, wo_na_ref, wo_hg_ref, nw_ref, wu_ref, wd_ref, o_ref):
    h = x_ref[...] + _dot(na_ref[...], wo_na_ref[...]) + _dot(hg_ref[...], wo_hg_ref[...])
    ms = jnp.mean(h * h, axis=-1, keepdims=True)
    u = (h * lax.rsqrt(ms + RMS_EPS) * nw_ref[...]).astype(BF16)
    mlp = None
    for j in range(D_FF // FF_CHUNK):
        cols = slice(j * FF_CHUNK, (j + 1) * FF_CHUNK)
        a = jnp.maximum(_dot(u, wu_ref[:, cols]), 0.0)
        part = _dot((a * a).astype(BF16), wd_ref[cols, :])
        mlp = part if mlp is None else mlp + part
    o_ref[...] = h + mlp


def _out_mlp(x2, y_na, y_hg, w_out, norm_w, w_up, w_down):
    m = x2.shape[0]
    tile = lambda width: pl.BlockSpec((TOKEN_TILE, width), lambda i: (i, 0))
    resident = lambda a: pl.BlockSpec(a.shape, lambda i: (0, 0), pipeline_mode=pl.Buffered(1))
    wo_na, wo_hg = w_out[:NA_WIDTH], w_out[NA_WIDTH:]
    return pl.pallas_call(
        _out_mlp_kernel,
        grid=(m // TOKEN_TILE,),
        in_specs=[tile(D_MODEL), tile(NA_WIDTH), tile(HG_WIDTH), resident(wo_na),
                  resident(wo_hg), pl.BlockSpec((1, D_MODEL), lambda i: (0, 0)),
                  resident(w_up), resident(w_down)],
        out_specs=tile(D_MODEL),
        out_shape=jax.ShapeDtypeStruct((m, D_MODEL), F32),
        compiler_params=pltpu.CompilerParams(
            dimension_semantics=("arbitrary",), vmem_limit_bytes=V7X_VMEM_LIMIT_BYTES),
        name="out_mlp",
    )(x2, y_na, y_hg, wo_na, wo_hg, norm_w, w_up, w_down)


def kernel(x, w_in, w_out, attn_norm_w, mlp_norm_w, q_norm_w, k_norm_w, rpb, hg_norm_w,
           lb_fwd, lb_bwd, w_up, w_down):
    batch, seq, d = x.shape
    assert d == D_MODEL and w_in.shape[0] == 1 and seq % GRID_W == 0 and seq % HG_CHUNK == 0
    x2 = x.reshape(batch * seq, d)
    proj = _in_proj(x2, attn_norm_w[0].astype(F32).reshape(1, d), w_in[0].astype(BF16))
    y_na = _natten(proj, q_norm_w[0], k_norm_w[0], rpb[0], batch, seq)
    y_hg = _hgrn2(proj, lb_fwd, lb_bwd, hg_norm_w[0], batch, seq)
    out = _out_mlp(x2, y_na, y_hg, w_out[0].astype(BF16),
                   mlp_norm_w[0].astype(F32).reshape(1, d),
                   w_up[0].astype(BF16), w_down[0].astype(BF16))
    return out.reshape(batch, seq, d)
```

```python
import functools

import numpy as np
import jax
import jax.numpy as jnp
from jax import lax
from jax.experimental import pallas as pl
from jax.experimental.pallas import tpu as pltpu

F32 = jnp.float32
BF16 = jnp.bfloat16

D_MODEL = 1024
GRID_W = 64
NA_HEADS = 8
NA_HEAD_DIM = 64
NA_WIDTH = NA_HEADS * NA_HEAD_DIM
NA_WIN_R = 8
NA_WIN_C = 16
HG_HEADS = 4
HG_DK = 128
HG_WIDTH = HG_HEADS * HG_DK
D_FF = 4 * D_MODEL
D_IN_PROJ = 3 * NA_WIDTH + 5 * HG_WIDTH
RMS_EPS = 1e-6

V7X_LANES = 128
V7X_VMEM_LIMIT_BYTES = 56 * 1024 * 1024

TOKEN_TILE = 512
FF_CHUNK = 1024
NORM_ROWS = 512
HG_CHUNK = 128
HG_LEVELS = 7
HG_CHUNKS_PER_STEP = 8
HG_OUT_CHUNKS_PER_STEP = 4
NA_QT = 16
NA_KT = 32
NA_TILES = GRID_W // NA_QT
NA_ROWS_PER_STEP = 8
MASK_NEG = -1e30

_NA_QCOLS = [list(range(0, 8)) + list(range(56, 64)),
             list(range(8, 24)), list(range(24, 40)), list(range(40, 56))]
_NA_KSEGS = [[(0, 16), (48, 16)], [(0, 32)], [(16, 32)], [(32, 32)]]


def _nt_dot(a, b):
    return lax.dot_general(a, b, (((1,), (1,)), ((), ())), preferred_element_type=F32)


def _dot(a, b):
    return jnp.dot(a, b, preferred_element_type=F32)


def _split_bf16(x):
    hi = x.astype(BF16)
    lo = (x - hi.astype(F32)).astype(BF16)
    return hi, lo


def _in_proj_kernel(x_ref, nw_ref, w_ref, o_ref):
    x = x_ref[...]
    ms = jnp.mean(x * x, axis=-1, keepdims=True)
    u = (x * lax.rsqrt(ms + RMS_EPS) * nw_ref[...]).astype(BF16)
    o_ref[...] = _dot(u, w_ref[...]).astype(o_ref.dtype)


def _in_proj(x2, norm_w, w_in):
    m = x2.shape[0]
    return pl.pallas_call(
        _in_proj_kernel,
        grid=(m // TOKEN_TILE,),
        in_specs=[
            pl.BlockSpec((TOKEN_TILE, D_MODEL), lambda i: (i, 0)),
            pl.BlockSpec((1, D_MODEL), lambda i: (0, 0)),
            pl.BlockSpec((D_MODEL, D_IN_PROJ), lambda i: (0, 0), pipeline_mode=pl.Buffered(1)),
        ],
        out_specs=pl.BlockSpec((TOKEN_TILE, D_IN_PROJ), lambda i: (i, 0)),
        out_shape=jax.ShapeDtypeStruct((m, D_IN_PROJ), BF16),
        compiler_params=pltpu.CompilerParams(
            dimension_semantics=("arbitrary",), vmem_limit_bytes=V7X_VMEM_LIMIT_BYTES),
        name="in_proj",
    )(x2, norm_w, w_in)


def _na_bias_table(rpb):
    n_ri, n_ci = 2 * NA_WIN_R - 1, 2 * NA_WIN_C - 1
    onehot = np.zeros((n_ci, NA_TILES, NA_QT, NA_KT), np.float32)
    for t in range(NA_TILES):
        qc = np.array(_NA_QCOLS[t])
        kc = np.concatenate([np.arange(c0, c0 + n) for c0, n in _NA_KSEGS[t]])
        cs = np.clip(qc - NA_WIN_C // 2, 0, GRID_W - NA_WIN_C)
        valid = (kc[None, :] >= cs[:, None]) & (kc[None, :] < cs[:, None] + NA_WIN_C)
        cidx = kc[None, :] - qc[:, None] + NA_WIN_C - 1
        qi, ki = np.nonzero(valid)
        onehot[cidx[qi, ki], t, qi, ki] = 1.0
    cols = jnp.einsum("hrc,ctqj->htqrj", rpb.astype(F32), jnp.asarray(onehot),
                      precision=lax.Precision.HIGHEST)
    cols = cols + jnp.asarray(np.where(onehot.sum(0) > 0, 0.0, MASK_NEG)[None, :, :, None, :], F32)
    cols = cols.reshape(NA_HEADS // 2, 2, NA_TILES, NA_QT, n_ri, NA_KT)
    cols = jnp.transpose(cols, (0, 2, 1, 3, 4, 5))
    tabs = [cols[..., NA_WIN_R - 1 - rs:2 * NA_WIN_R - 1 - rs, :].reshape(
        NA_HEADS // 2, NA_TILES, 2 * NA_QT, NA_WIN_R * NA_KT) for rs in range(NA_WIN_R)]
    return jnp.stack(tabs)


def _natten_kernel(q_ref, k_ref, v_ref, qw_ref, kw_ref, bias_ref, hm_ref, o_ref, qn_s, kn_s,
                   *, rows):
    seq = rows * GRID_W
    hm = hm_ref[...]

    def head_rmsnorm(x, w):
        hi, lo = _split_bf16(x * x)
        ms = _dot(jnp.concatenate([hi, lo], axis=1), hm)
        return x * lax.rsqrt(ms + RMS_EPS) * w

    def norm_body(i, carry):
        sl = pl.ds(pl.multiple_of(i * NORM_ROWS, NORM_ROWS), NORM_ROWS)
        qn = head_rmsnorm(q_ref[sl, :].astype(F32), qw_ref[...])
        qn_s[sl, :] = qn * (NA_HEAD_DIM ** -0.5)
        kn_s[sl, :] = head_rmsnorm(k_ref[sl, :].astype(F32), kw_ref[...]).astype(BF16)
        return carry

    lax.fori_loop(0, seq // NORM_ROWS, norm_body, 0)

    lane = lax.broadcasted_iota(jnp.int32, (NA_QT, V7X_LANES), 1)
    first_head = lane < NA_HEAD_DIM
    wr = min(NA_WIN_R, rows)

    def row_block(ref, grid_row):
        if isinstance(grid_row, int):
            return ref[grid_row * GRID_W:(grid_row + 1) * GRID_W, :]
        return ref[pl.ds(pl.multiple_of(grid_row * GRID_W, GRID_W), GRID_W), :]

    def band(blocks, t):
        return jnp.concatenate(
            [blk[c0:c0 + n] for blk in blocks for (c0, n) in _NA_KSEGS[t]], axis=0)

    def step(row0, key_row0, row_off, rel_starts):
        n_key_rows = max(rel_starts) + wr
        groups = [(rel, [j for j, r in enumerate(rel_starts) if r == rel])
                  for rel in sorted(set(rel_starts))]
        kblocks = [row_block(kn_s, key_row0 + u) for u in range(n_key_rows)]
        qrows = [row_block(qn_s, row0 + j) for j in range(len(rel_starts))]
        scores = []
        for rel, js in groups:
            for t in range(NA_TILES):
                qc = _NA_QCOLS[t]
                q2, bias = [], []
                for j in js:
                    if t == 0:
                        qt = jnp.concatenate([qrows[j][0:8], qrows[j][56:64]], axis=0)
                    else:
                        qt = qrows[j][qc[0]:qc[0] + NA_QT]
                    q2 += [jnp.where(first_head, qt, 0.0), jnp.where(first_head, 0.0, qt)]
                    bias.append(bias_ref[row_off + j - rel, t])
                kb = band(kblocks[rel:rel + wr], t)
                s = _nt_dot(jnp.concatenate(q2, axis=0).astype(BF16), kb)
                scores.append(s + (bias[0] if len(js) == 1 else jnp.concatenate(bias, axis=0)))
        probs, sums = [], []
        for s in scores:
            p = jnp.exp(s - jnp.max(s, axis=-1, keepdims=True))
            sums.append(jnp.sum(p, axis=-1, keepdims=True))
            probs.append(p.astype(BF16))
        vblocks = [row_block(v_ref, key_row0 + u) for u in range(n_key_rows)]
        o_tiles = {}
        for g, (rel, js) in enumerate(groups):
            for t in range(NA_TILES):
                vb = band(vblocks[rel:rel + wr], t)
                o2 = _dot(probs[g * NA_TILES + t], vb) / sums[g * NA_TILES + t]
                for n, j in enumerate(js):
                    o_j = o2[2 * NA_QT * n:2 * NA_QT * (n + 1)]
                    o_tiles[j, t] = jnp.where(first_head, o_j[0:NA_QT], o_j[NA_QT:2 * NA_QT])
        for j in range(len(rel_starts)):
            orow = jnp.concatenate([o_tiles[j, 0][0:8], o_tiles[j, 1], o_tiles[j, 2],
                                    o_tiles[j, 3], o_tiles[j, 0][8:16]], axis=0)
            if isinstance(row0, int):
                osl = slice((row0 + j) * GRID_W, (row0 + j + 1) * GRID_W)
            else:
                osl = pl.ds(pl.multiple_of((row0 + j) * GRID_W, GRID_W), GRID_W)
            o_ref[osl, :] = orow.astype(o_ref.dtype)

    u = NA_ROWS_PER_STEP
    assert rows % u == 0
    starts = [[min(max(i * u + j - wr // 2, 0), rows - wr) for j in range(u)]
              for i in range(rows // u)]
    regular = [st == [i * u + j - wr // 2 for j in range(u)] for i, st in enumerate(starts)]
    lo = regular.index(True)
    hi = len(regular) - regular[::-1].index(True)
    assert all(regular[lo:hi])

    def peeled(i):
        step(i * u, starts[i][0], i * u - starts[i][0], [s - starts[i][0] for s in starts[i]])

    for i in range(lo):
        peeled(i)

    def regular_body(i, carry):
        step(i * u, i * u - wr // 2, wr // 2, list(range(u)))
        return carry

    lax.fori_loop(lo, hi, regular_body, 0)
    for i in range(hi, rows // u):
        peeled(i)


def _natten(proj, q_norm_w, k_norm_w, rpb, batch, seq):
    rows = seq // GRID_W
    assert rows >= NA_WIN_R and GRID_W == 64 and NA_HEAD_DIM * 2 == V7X_LANES
    pairs = NA_HEADS // 2
    bias = _na_bias_table(rpb)
    qw = jnp.tile(q_norm_w.astype(F32).reshape(1, NA_HEAD_DIM), (1, 2))
    kw = jnp.tile(k_norm_w.astype(F32).reshape(1, NA_HEAD_DIM), (1, 2))
    head_id = np.arange(V7X_LANES) // NA_HEAD_DIM
    hm = np.tile((head_id[:, None] == head_id[None, :]) / NA_HEAD_DIM, (2, 1))
    hm = jnp.asarray(hm, BF16)
    col = lambda off: pl.BlockSpec((seq, V7X_LANES), lambda b, p, off=off: (b, off + p))
    const2 = lambda shape: pl.BlockSpec(shape, lambda b, p: (0, 0))
    return pl.pallas_call(
        functools.partial(_natten_kernel, rows=rows),
        grid=(batch, pairs),
        in_specs=[
            col(0), col(pairs), col(2 * pairs),
            const2((1, V7X_LANES)), const2((1, V7X_LANES)),
            pl.BlockSpec((NA_WIN_R, None, NA_TILES, 2 * NA_QT, NA_WIN_R * NA_KT),
                         lambda b, p: (0, p, 0, 0, 0)),
            const2((2 * V7X_LANES, V7X_LANES)),
        ],
        out_specs=pl.BlockSpec((seq, V7X_LANES), lambda b, p: (b, p)),
        out_shape=jax.ShapeDtypeStruct((batch * seq, NA_WIDTH), BF16),
        scratch_shapes=[pltpu.VMEM((seq, V7X_LANES), F32), pltpu.VMEM((seq, V7X_LANES), BF16)],
        compiler_params=pltpu.CompilerParams(
            dimension_semantics=("arbitrary", "arbitrary"),
            vmem_limit_bytes=V7X_VMEM_LIMIT_BYTES),
        name="natten",
    )(proj, proj, proj, qw, kw, bias, hm)


def _hg_tables():
    c = HG_CHUNK
    idx = np.arange(c)
    lower = (idx[None, :] <= idx[:, None]).astype(np.float32)
    cums = np.stack([np.concatenate([lower, lower], axis=1),
                     np.concatenate([lower.T, lower.T], axis=1)])
    masks = []
    for lvl in range(HG_LEVELS):
        s = 1 << lvl
        same_block = (idx[:, None] // (2 * s)) == (idx[None, :] // (2 * s))
        half = (idx // s) % 2
        masks.append(same_block & (half[:, None] != half[None, :]))
    masks.append(np.eye(c, dtype=bool))
    return jnp.asarray(cums, BF16), jnp.asarray(np.stack(masks), F32)


def _hgrn_kernel(q_ref, ff_ref, fb_ref, v_ref, g_ref, lbf_ref, lbb_ref, gw_ref,
                 cum_ref, mask_ref, o_ref, acc_s, qb_s, inc_s, etb_s, *, seq):
    c = HG_CHUNK
    n_chunks = seq // c
    n_step = HG_CHUNKS_PER_STEP
    assert n_chunks % n_step == 0
    sub = lax.broadcasted_iota(jnp.int32, (c, V7X_LANES), 0) & 7

    def lower_bound(lb_ref):
        a = lb_ref[...].astype(F32)
        a0, a1 = a[0:1], a[1:2]
        m = jnp.maximum(a0, a1)
        e0, e1 = jnp.exp(a0 - m), jnp.exp(a1 - m)
        return e0 / (e0 + e1)

    lbf = lower_bound(lbf_ref)
    lbb = lower_bound(lbb_ref)

    def rows(ci):
        return pl.ds(pl.multiple_of(ci * c, c), c)

    def group_rows(x, r):
        return jnp.concatenate(
            [jnp.broadcast_to(x[8 * j + r:8 * j + r + 1], (8, V7X_LANES)) for j in range(c // 8)],
            axis=0)

    def repeat8(x8, n_rows):
        return x8 if n_rows == 8 else jnp.concatenate([x8] * (n_rows // 8), axis=0)

    def gates(f_ref, sl, lb):
        hf = f_ref[sl, :].astype(F32)
        f = lb + (1.0 - lb) * (1.0 / (1.0 + jnp.exp(-hf)))
        hi, lo = _split_bf16(jnp.log2(f))
        return f, 1.0 - f, jnp.concatenate([hi, lo], axis=0)

    def stage_gates(ci):
        sl = rows(ci)
        st = dict(sl=sl, q=q_ref[sl, :].astype(F32), v=v_ref[sl, :])
        st["f_f"], st["k_f"], lf_f = gates(ff_ref, sl, lbf)
        st["f_b"], st["k_b"], lf_b = gates(fb_ref, sl, lbb)
        st["b_f"] = _dot(cum_ref[0], lf_f)
        st["b_b"] = _dot(cum_ref[1], lf_b)
        return st

    def stage_operands(ci, st):
        q, k_f, k_b, b_f, b_b = st["q"], st["k_f"], st["k_b"], st["b_f"], st["b_b"]
        last_f = group_rows(b_f, 7)
        first_b = group_rows(b_b, 0)
        odd = (sub & 1) == 1
        xs = [q * jnp.where(odd, st["f_f"], st["f_b"])]
        ys = [jnp.where(odd, k_b, k_f)]
        for lvl in (1, 2):
            later = (sub & (1 << lvl)) != 0
            if lvl == 1:
                ref_f = jnp.where(sub < 4, group_rows(b_f, 1), group_rows(b_f, 5))
                ref_b = jnp.where(sub < 4, group_rows(b_b, 2), group_rows(b_b, 6))
            else:
                ref_f, ref_b = group_rows(b_f, 3), group_rows(b_b, 4)
            d_f, d_b = b_f - ref_f, b_b - ref_b
            xs.append(q * jnp.exp2(jnp.where(later, d_f, d_b)))
            ys.append(jnp.where(later, k_b, k_f) * jnp.exp2(-jnp.where(later, d_b, d_f)))
        for lvl in range(3, HG_LEVELS):
            s = 1 << lvl
            xp, yp = [], []
            for r0 in range(0, c, 2 * s):
                m = r0 + s
                rf = repeat8(last_f[m - 8:m], s)
                rb = repeat8(first_b[m:m + 8], s)
                early, late = slice(r0, m), slice(m, m + s)
                xp += [q[early] * jnp.exp2(b_b[early] - rb), q[late] * jnp.exp2(b_f[late] - rf)]
                yp += [k_f[early] * jnp.exp2(rf - b_f[early]), k_b[late] * jnp.exp2(rb - b_b[late])]
            xs.append(jnp.concatenate(xp, axis=0))
            ys.append(jnp.concatenate(yp, axis=0))
        xs.append(q)
        ys.append(k_f + k_b)
        st["xs"] = [x.astype(BF16) for x in xs]
        st["ys"] = [y.astype(BF16) for y in ys]
        sl = st["sl"]
        tot_f = last_f[c - 8:c]
        tot_b = first_b[0:8]
        st["qe_f"] = (q * jnp.exp2(b_f)).astype(BF16)
        st["kd_f"] = (k_f * jnp.exp2(repeat8(tot_f, c) - b_f)).astype(BF16)
        st["tot_f"] = jnp.exp2(tot_f[0:1])
        qb_s[sl, :] = (q * jnp.exp2(b_b)).astype(BF16)
        st["kd_b"] = (k_b * jnp.exp2(repeat8(tot_b, c) - b_b)).astype(BF16)
        etb_s[pl.ds(pl.multiple_of(ci * 8, 8), 8), :] = jnp.exp2(tot_b)
        return st

    def stage_scores(st):
        a = None
        for lvl in range(HG_LEVELS + 1):
            term = mask_ref[lvl] * _nt_dot(st["xs"][lvl], st["ys"][lvl])
            a = term if a is None else a + term
        st["a"] = a.astype(BF16)
        return st

    def fwd_body(i, state_t):
        cis = [i * n_step + j for j in range(n_step)]
        sts = [stage_gates(ci) for ci in cis]
        sts = [stage_operands(ci, st) for ci, st in zip(cis, sts)]
        sts = [stage_scores(st) for st in sts]
        intra = [_dot(st["a"], st["v"]) for st in sts]
        incs = [_dot(st["v"].T, jnp.concatenate([st["kd_f"], st["kd_b"]], axis=1)) for st in sts]
        for st, o, inc in zip(sts, intra, incs):
            acc_s[st["sl"], :] = o + _nt_dot(st["qe_f"], state_t.astype(BF16))
            state_t = state_t * st["tot_f"] + inc[:, 0:HG_DK]
            inc_s[st["sl"], :] = inc[:, HG_DK:2 * HG_DK]
        return state_t

    zero = jnp.zeros((HG_DK, HG_DK), F32)
    lax.fori_loop(0, n_chunks // n_step, fwd_body, zero)

    n_out = HG_OUT_CHUNKS_PER_STEP
    assert n_chunks % n_out == 0

    def bwd_body(i, state_t):
        cis = [n_chunks - 1 - (i * n_out + j) for j in range(n_out)]
        inter = []
        for ci in cis:
            inter.append(_nt_dot(qb_s[rows(ci), :], state_t.astype(BF16)))
            tot_b = etb_s[pl.ds(pl.multiple_of(ci * 8, 8), 1), :]
            state_t = state_t * tot_b + inc_s[rows(ci), :]
        for ci, o_inter in zip(cis, inter):
            sl = rows(ci)
            o = acc_s[sl, :] + o_inter
            ms = jnp.mean(o * o, axis=-1, keepdims=True)
            y = o * lax.rsqrt(ms + RMS_EPS) * gw_ref[...]
            g = g_ref[sl, :].astype(F32)
            o_ref[sl, :] = (y * (g * (1.0 / (1.0 + jnp.exp(-g))))).astype(o_ref.dtype)
        return state_t

    lax.fori_loop(0, n_chunks // n_out, bwd_body, zero)


def _hgrn2(proj, lb_fwd, lb_bwd, hg_norm_w, batch, seq):
    assert HG_DK == V7X_LANES and lb_fwd.shape[0] == 2
    cums, masks = _hg_tables()
    first = 3 * NA_WIDTH // V7X_LANES
    col = lambda k: pl.BlockSpec((seq, V7X_LANES),
                                 lambda b, h, k=k: (b, first + k * HG_HEADS + h))
    lbs = pl.BlockSpec((2, V7X_LANES), lambda b, h: (0, h))
    const = lambda a: pl.BlockSpec(a.shape, lambda b, h, nd=a.ndim: (0,) * nd)
    gw = hg_norm_w.astype(F32).reshape(1, HG_DK)
    return pl.pallas_call(
        functools.partial(_hgrn_kernel, seq=seq),
        grid=(batch, HG_HEADS),
        in_specs=[col(0), col(1), col(2), col(3), col(4), lbs, lbs, const(gw),
                  const(cums), const(masks)],
        out_specs=pl.BlockSpec((seq, V7X_LANES), lambda b, h: (b, h)),
        out_shape=jax.ShapeDtypeStruct((batch * seq, HG_WIDTH), BF16),
        scratch_shapes=[pltpu.VMEM((seq, HG_DK), F32), pltpu.VMEM((seq, HG_DK), BF16),
                        pltpu.VMEM((seq, HG_DK), F32),
                        pltpu.VMEM((8 * seq // HG_CHUNK, HG_DK), F32)],
        compiler_params=pltpu.CompilerParams(
            dimension_semantics=("arbitrary", "arbitrary"),
            vmem_limit_bytes=V7X_VMEM_LIMIT_BYTES),
        name="hgrn2",
    )(proj, proj, proj, proj, proj, lb_fwd.astype(F32), lb_bwd.astype(F32), gw, cums, masks)


def _out_mlp_kernel(x_ref, na_ref, hg_ref, wo_na_ref, wo_hg_ref, nw_ref, wu_ref, wd_ref, o_ref):
    h = x_ref[...] + _dot(na_ref[...], wo_na_ref[...]) + _dot(hg_ref[...], wo_hg_ref[...])
    ms = jnp.mean(h * h, axis=-1, keepdims=True)
    u = (h * lax.rsqrt(ms + RMS_EPS) * nw_ref[...]).astype(BF16)
    mlp = None
    for j in range(D_FF // FF_CHUNK):
        cols = slice(j * FF_CHUNK, (j + 1) * FF_CHUNK)
        a = jnp.maximum(_dot(u, wu_ref[:, cols]), 0.0)
        part = _dot((a * a).astype(BF16), wd_ref[cols, :])
        mlp = part if mlp is None else mlp + part
    o_ref[...] = h + mlp


def _out_mlp(x2, y_na, y_hg, w_out, norm_w, w_up, w_down):
    m = x2.shape[0]
    tile = lambda width: pl.BlockSpec((TOKEN_TILE, width), lambda i: (i, 0))
    resident = lambda a: pl.BlockSpec(a.shape, lambda i: (0, 0), pipeline_mode=pl.Buffered(1))
    wo_na, wo_hg = w_out[:NA_WIDTH], w_out[NA_WIDTH:]
    return pl.pallas_call(
        _out_mlp_kernel,
        grid=(m // TOKEN_TILE,),
        in_specs=[tile(D_MODEL), tile(NA_WIDTH), tile(HG_WIDTH), resident(wo_na),
                  resident(wo_hg), pl.BlockSpec((1, D_MODEL), lambda i: (0, 0)),
                  resident(w_up), resident(w_down)],
        out_specs=tile(D_MODEL),
        out_shape=jax.ShapeDtypeStruct((m, D_MODEL), F32),
        compiler_params=pltpu.CompilerParams(
            dimension_semantics=("arbitrary",), vmem_limit_bytes=V7X_VMEM_LIMIT_BYTES),
        name="out_mlp",
    )(x2, y_na, y_hg, wo_na, wo_hg, norm_w, w_up, w_down)


def kernel(x, w_in, w_out, attn_norm_w, mlp_norm_w, q_norm_w, k_norm_w, rpb, hg_norm_w,
           lb_fwd, lb_bwd, w_up, w_down):
    batch, seq, d = x.shape
    assert d == D_MODEL and w_in.shape[0] == 1 and seq % GRID_W == 0 and seq % HG_CHUNK == 0
    x2 = x.reshape(batch * seq, d)
    proj = _in_proj(x2, attn_norm_w[0].astype(F32).reshape(1, d), w_in[0].astype(BF16))
    y_na = _natten(proj, q_norm_w[0], k_norm_w[0], rpb[0], batch, seq)
    y_hg = _hgrn2(proj, lb_fwd, lb_bwd, hg_norm_w[0], batch, seq)
    out = _out_mlp(x2, y_na, y_hg, w_out[0].astype(BF16),
                   mlp_norm_w[0].astype(F32).reshape(1, d),
                   w_up[0].astype(BF16), w_down[0].astype(BF16))
    return out.reshape(batch, seq, d)
```

```python
import functools

import numpy as np
import jax
import jax.numpy as jnp
from jax import lax
from jax.experimental import pallas as pl
from jax.experimental.pallas import tpu as pltpu

F32 = jnp.float32
BF16 = jnp.bfloat16

D_MODEL = 1024
GRID_W = 64
NA_HEADS = 8
NA_HEAD_DIM = 64
NA_WIDTH = NA_HEADS * NA_HEAD_DIM
NA_WIN_R = 8
NA_WIN_C = 16
HG_HEADS = 4
HG_DK = 128
HG_WIDTH = HG_HEADS * HG_DK
D_FF = 4 * D_MODEL
D_IN_PROJ = 3 * NA_WIDTH + 5 * HG_WIDTH
RMS_EPS = 1e-6

V7X_LANES = 128
V7X_VMEM_LIMIT_BYTES = 56 * 1024 * 1024

TOKEN_TILE = 512
FF_CHUNK = 1024
NORM_ROWS = 512
HG_CHUNK = 128
HG_LEVELS = 7
HG_CHUNKS_PER_STEP = 8
HG_OUT_CHUNKS_PER_STEP = 8
NA_QT = 16
NA_KT = 32
NA_TILES = GRID_W // NA_QT
NA_GROUP_ROWS = 4
NA_BAND_ROWS = 12
NA_GROUPS_PER_STEP = 2
MASK_NEG = -1e30

_NA_QCOLS = [list(range(0, 8)) + list(range(56, 64)),
             list(range(8, 24)), list(range(24, 40)), list(range(40, 56))]
_NA_KSEGS = [[(0, 16), (48, 16)], [(0, 32)], [(16, 32)], [(32, 32)]]


def _nt_dot(a, b):
    return lax.dot_general(a, b, (((1,), (1,)), ((), ())), preferred_element_type=F32)


def _dot(a, b):
    return jnp.dot(a, b, preferred_element_type=F32)


def _split_bf16(x):
    hi = x.astype(BF16)
    lo = (x - hi.astype(F32)).astype(BF16)
    return hi, lo


def _in_proj_kernel(x_ref, nw_ref, w_ref, o_ref):
    x = x_ref[...]
    ms = jnp.mean(x * x, axis=-1, keepdims=True)
    u = (x * lax.rsqrt(ms + RMS_EPS) * nw_ref[...]).astype(BF16)
    o_ref[...] = _dot(u, w_ref[...]).astype(o_ref.dtype)


def _in_proj(x2, norm_w, w_in):
    m = x2.shape[0]
    return pl.pallas_call(
        _in_proj_kernel,
        grid=(m // TOKEN_TILE,),
        in_specs=[
            pl.BlockSpec((TOKEN_TILE, D_MODEL), lambda i: (i, 0)),
            pl.BlockSpec((1, D_MODEL), lambda i: (0, 0)),
            pl.BlockSpec((D_MODEL, D_IN_PROJ), lambda i: (0, 0), pipeline_mode=pl.Buffered(1)),
        ],
        out_specs=pl.BlockSpec((TOKEN_TILE, D_IN_PROJ), lambda i: (i, 0)),
        out_shape=jax.ShapeDtypeStruct((m, D_IN_PROJ), BF16),
        compiler_params=pltpu.CompilerParams(
            dimension_semantics=("arbitrary",), vmem_limit_bytes=V7X_VMEM_LIMIT_BYTES),
        name="in_proj",
    )(x2, norm_w, w_in)


def _na_group_plan(rows):
    wr = min(NA_WIN_R, rows)
    bases, types, sigs = [], [], []
    for g in range(rows // NA_GROUP_ROWS):
        r0 = g * NA_GROUP_ROWS
        starts = [min(max(r0 + j - wr // 2, 0), rows - wr) for j in range(NA_GROUP_ROWS)]
        base = min(max(starts[0], 0), rows - NA_BAND_ROWS)
        assert base <= starts[0] and starts[-1] + wr <= base + NA_BAND_ROWS
        sig = tuple((starts[j] - base, r0 + j - base) for j in range(NA_GROUP_ROWS))
        if sig not in sigs:
            sigs.append(sig)
        bases.append(base)
        types.append(sigs.index(sig))
    return bases, types, sigs


def _na_bias_table(rpb, sigs):
    n_ri, n_ci = 2 * NA_WIN_R - 1, 2 * NA_WIN_C - 1
    csel = np.zeros((n_ci, NA_TILES, NA_QT, NA_KT), np.float32)
    for t in range(NA_TILES):
        qc = np.array(_NA_QCOLS[t])
        kc = np.concatenate([np.arange(c0, c0 + n) for c0, n in _NA_KSEGS[t]])
        cs = np.clip(qc - NA_WIN_C // 2, 0, GRID_W - NA_WIN_C)
        valid = (kc[None, :] >= cs[:, None]) & (kc[None, :] < cs[:, None] + NA_WIN_C)
        cidx = kc[None, :] - qc[:, None] + NA_WIN_C - 1
        qi, ki = np.nonzero(valid)
        csel[cidx[qi, ki], t, qi, ki] = 1.0
    rsel = np.zeros((len(sigs), NA_GROUP_ROWS, NA_BAND_ROWS, n_ri), np.float32)
    for y, sig in enumerate(sigs):
        for j, (first, qrow) in enumerate(sig):
            for u in range(first, first + NA_WIN_R):
                rsel[y, j, u, u - qrow + NA_WIN_R - 1] = 1.0
    hi = lax.Precision.HIGHEST
    cols = jnp.einsum("hrc,ctqk->htqrk", rpb.astype(F32), jnp.asarray(csel), precision=hi)
    cols = cols.reshape(NA_HEADS // 2, 2, NA_TILES, NA_QT, n_ri, NA_KT)
    tab = jnp.einsum("phtqrk,yjur->yptjhquk", cols, jnp.asarray(rsel), precision=hi)
    allowed = (rsel.sum(-1)[:, None, :, None, None, :, None]
               * csel.sum(0)[None, :, None, None, :, None, :])
    tab = tab + jnp.asarray(np.where(allowed > 0, 0.0, MASK_NEG)[:, None], F32)
    return tab.reshape(len(sigs), NA_HEADS // 2, NA_TILES, NA_GROUP_ROWS * 2 * NA_QT,
                       NA_BAND_ROWS * NA_KT)


def _natten_kernel(q_ref, k_ref, v_ref, qw_ref, kw_ref, bias_ref, hm_ref, o_ref, qn_s, kn_s,
                   *, rows, bases, types):
    seq = rows * GRID_W
    hm = hm_ref[...]

    def head_rmsnorm(x, w):
        hi, lo = _split_bf16(x * x)
        ms = _dot(jnp.concatenate([hi, lo], axis=1), hm)
        return x * lax.rsqrt(ms + RMS_EPS) * w

    def norm_body(i, carry):
        sl = pl.ds(pl.multiple_of(i * NORM_ROWS, NORM_ROWS), NORM_ROWS)
        qn = head_rmsnorm(q_ref[sl, :].astype(F32), qw_ref[...])
        qn_s[sl, :] = qn * (NA_HEAD_DIM ** -0.5)
        kn_s[sl, :] = head_rmsnorm(k_ref[sl, :].astype(F32), kw_ref[...]).astype(BF16)
        return carry

    lax.fori_loop(0, seq // NORM_ROWS, norm_body, 0)

    lane = lax.broadcasted_iota(jnp.int32, (NA_QT, V7X_LANES), 1)
    first_head = lane < NA_HEAD_DIM
    gr = NA_GROUP_ROWS

    def row_block(ref, grid_row):
        if isinstance(grid_row, int):
            return ref[grid_row * GRID_W:(grid_row + 1) * GRID_W, :]
        return ref[pl.ds(pl.multiple_of(grid_row * GRID_W, GRID_W), GRID_W), :]

    def band(blocks, t):
        return jnp.concatenate(
            [blk[c0:c0 + n] for blk in blocks for (c0, n) in _NA_KSEGS[t]], axis=0)

    def step(groups):
        scores = []
        for row0, base, y in groups:
            kblocks = [row_block(kn_s, base + u) for u in range(NA_BAND_ROWS)]
            qrows = [row_block(qn_s, row0 + j) for j in range(gr)]
            for t in range(NA_TILES):
                qc = _NA_QCOLS[t]
                q2 = []
                for j in range(gr):
                    if t == 0:
                        qt = jnp.concatenate([qrows[j][0:8], qrows[j][56:64]], axis=0)
                    else:
                        qt = qrows[j][qc[0]:qc[0] + NA_QT]
                    q2 += [jnp.where(first_head, qt, 0.0), jnp.where(first_head, 0.0, qt)]
                s = _nt_dot(jnp.concatenate(q2, axis=0).astype(BF16), band(kblocks, t))
                scores.append(s + bias_ref[y, t])
        probs, sums = [], []
        for s in scores:
            p = jnp.exp(s - jnp.max(s, axis=-1, keepdims=True))
            sums.append(jnp.sum(p, axis=-1, keepdims=True))
            probs.append(p.astype(BF16))
        for g, (row0, base, y) in enumerate(groups):
            vblocks = [row_block(v_ref, base + u) for u in range(NA_BAND_ROWS)]
            o_tiles = []
            for t in range(NA_TILES):
                o2 = _dot(probs[g * NA_TILES + t], band(vblocks, t)) / sums[g * NA_TILES + t]
                o_tiles.append(o2)
            for j in range(gr):
                sel = [jnp.where(first_head, o[2 * NA_QT * j:2 * NA_QT * j + NA_QT],
                                 o[2 * NA_QT * j + NA_QT:2 * NA_QT * (j + 1)]) for o in o_tiles]
                orow = jnp.concatenate([sel[0][0:8], sel[1], sel[2], sel[3], sel[0][8:16]], axis=0)
                if isinstance(row0, int):
                    osl = slice((row0 + j) * GRID_W, (row0 + j + 1) * GRID_W)
                else:
                    osl = pl.ds(pl.multiple_of((row0 + j) * GRID_W, GRID_W), GRID_W)
                o_ref[osl, :] = orow.astype(o_ref.dtype)

    n_groups = rows // gr
    shift = min(NA_WIN_R, rows) // 2
    regular = [bases[g] == g * gr - shift for g in range(n_groups)]
    lo = regular.index(True)
    hi = n_groups - regular[::-1].index(True)
    per = NA_GROUPS_PER_STEP
    assert all(regular[lo:hi]) and (hi - lo) % per == 0
    assert len({types[g] for g in range(lo, hi)}) == 1
    edge = [(g * gr, bases[g], types[g]) for g in list(range(lo)) + list(range(hi, n_groups))]
    for i in range(0, len(edge), per):
        step(edge[i:i + per])

    def regular_body(i, carry):
        g0 = lo + i * per
        step([((g0 + n) * gr, (g0 + n) * gr - shift, types[lo]) for n in range(per)])
        return carry

    lax.fori_loop(0, (hi - lo) // per, regular_body, 0)


def _natten(proj, q_norm_w, k_norm_w, rpb, batch, seq):
    rows = seq // GRID_W
    assert rows >= NA_BAND_ROWS and rows % NA_GROUP_ROWS == 0
    assert GRID_W == 64 and NA_HEAD_DIM * 2 == V7X_LANES
    pairs = NA_HEADS // 2
    bases, types, sigs = _na_group_plan(rows)
    bias = _na_bias_table(rpb, sigs)
    qw = jnp.tile(q_norm_w.astype(F32).reshape(1, NA_HEAD_DIM), (1, 2))
    kw = jnp.tile(k_norm_w.astype(F32).reshape(1, NA_HEAD_DIM), (1, 2))
    head_id = np.arange(V7X_LANES) // NA_HEAD_DIM
    hm = np.tile((head_id[:, None] == head_id[None, :]) / NA_HEAD_DIM, (2, 1))
    hm = jnp.asarray(hm, BF16)
    col = lambda off: pl.BlockSpec((seq, V7X_LANES), lambda b, p, off=off: (b, off + p))
    const2 = lambda shape: pl.BlockSpec(shape, lambda b, p: (0, 0))
    return pl.pallas_call(
        functools.partial(_natten_kernel, rows=rows, bases=tuple(bases), types=tuple(types)),
        grid=(batch, pairs),
        in_specs=[
            col(0), col(pairs), col(2 * pairs),
            const2((1, V7X_LANES)), const2((1, V7X_LANES)),
            pl.BlockSpec((len(sigs), None, NA_TILES, NA_GROUP_ROWS * 2 * NA_QT,
                          NA_BAND_ROWS * NA_KT), lambda b, p: (0, p, 0, 0, 0)),
            const2((2 * V7X_LANES, V7X_LANES)),
        ],
        out_specs=pl.BlockSpec((seq, V7X_LANES), lambda b, p: (b, p)),
        out_shape=jax.ShapeDtypeStruct((batch * seq, NA_WIDTH), BF16),
        scratch_shapes=[pltpu.VMEM((seq, V7X_LANES), F32), pltpu.VMEM((seq, V7X_LANES), BF16)],
        compiler_params=pltpu.CompilerParams(
            dimension_semantics=("arbitrary", "arbitrary"),
            vmem_limit_bytes=V7X_VMEM_LIMIT_BYTES),
        name="natten",
    )(proj, proj, proj, qw, kw, bias, hm)


def _hg_tables():
    c = HG_CHUNK
    idx = np.arange(c)
    lower = (idx[None, :] <= idx[:, None]).astype(np.float32)
    cums = np.stack([np.concatenate([lower, lower], axis=1),
                     np.concatenate([lower.T, lower.T], axis=1)])
    masks = []
    for lvl in range(HG_LEVELS):
        s = 1 << lvl
        same_block = (idx[:, None] // (2 * s)) == (idx[None, :] // (2 * s))
        half = (idx // s) % 2
        masks.append(same_block & (half[:, None] != half[None, :]))
    masks.append(np.eye(c, dtype=bool))
    return jnp.asarray(cums, BF16), jnp.asarray(np.stack(masks), F32)


def _hgrn_kernel(q_ref, ff_ref, fb_ref, v_ref, g_ref, lbf_ref, lbb_ref, gw_ref,
                 cum_ref, mask_ref, o_ref, acc_s, qb_s, inc_s, etb_s, *, seq):
    c = HG_CHUNK
    n_chunks = seq // c
    n_step = HG_CHUNKS_PER_STEP
    assert n_chunks % n_step == 0
    sub = lax.broadcasted_iota(jnp.int32, (c, V7X_LANES), 0) & 7

    def lower_bound(lb_ref):
        a = lb_ref[...].astype(F32)
        a0, a1 = a[0:1], a[1:2]
        m = jnp.maximum(a0, a1)
        e0, e1 = jnp.exp(a0 - m), jnp.exp(a1 - m)
        return e0 / (e0 + e1)

    lbf = lower_bound(lbf_ref)
    lbb = lower_bound(lbb_ref)

    def rows(ci):
        return pl.ds(pl.multiple_of(ci * c, c), c)

    def group_rows(x, r):
        return jnp.concatenate(
            [jnp.broadcast_to(x[8 * j + r:8 * j + r + 1], (8, V7X_LANES)) for j in range(c // 8)],
            axis=0)

    def repeat8(x8, n_rows):
        return x8 if n_rows == 8 else jnp.concatenate([x8] * (n_rows // 8), axis=0)

    def gates(f_ref, sl, lb):
        hf = f_ref[sl, :].astype(F32)
        f = lb + (1.0 - lb) * (1.0 / (1.0 + jnp.exp(-hf)))
        hi, lo = _split_bf16(jnp.log2(f))
        return f, 1.0 - f, jnp.concatenate([hi, lo], axis=0)

    def stage_gates(ci):
        sl = rows(ci)
        st = dict(sl=sl, q=q_ref[sl, :].astype(F32), v=v_ref[sl, :])
        st["f_f"], st["k_f"], lf_f = gates(ff_ref, sl, lbf)
        st["f_b"], st["k_b"], lf_b = gates(fb_ref, sl, lbb)
        st["b_f"] = _dot(cum_ref[0], lf_f)
        st["b_b"] = _dot(cum_ref[1], lf_b)
        return st

    def stage_operands(ci, st):
        q, k_f, k_b, b_f, b_b = st["q"], st["k_f"], st["k_b"], st["b_f"], st["b_b"]
        last_f = group_rows(b_f, 7)
        first_b = group_rows(b_b, 0)
        odd = (sub & 1) == 1
        xs = [q * jnp.where(odd, st["f_f"], st["f_b"])]
        ys = [jnp.where(odd, k_b, k_f)]
        for lvl in (1, 2):
            later = (sub & (1 << lvl)) != 0
            if lvl == 1:
                ref_f = jnp.where(sub < 4, group_rows(b_f, 1), group_rows(b_f, 5))
                ref_b = jnp.where(sub < 4, group_rows(b_b, 2), group_rows(b_b, 6))
            else:
                ref_f, ref_b = group_rows(b_f, 3), group_rows(b_b, 4)
            d_f, d_b = b_f - ref_f, b_b - ref_b
            xs.append(q * jnp.exp2(jnp.where(later, d_f, d_b)))
            ys.append(jnp.where(later, k_b, k_f) * jnp.exp2(-jnp.where(later, d_b, d_f)))
        for lvl in range(3, HG_LEVELS):
            s = 1 << lvl
            xp, yp = [], []
            for r0 in range(0, c, 2 * s):
                m = r0 + s
                rf = repeat8(last_f[m - 8:m], s)
                rb = repeat8(first_b[m:m + 8], s)
                early, late = slice(r0, m), slice(m, m + s)
                xp += [q[early] * jnp.exp2(b_b[early] - rb), q[late] * jnp.exp2(b_f[late] - rf)]
                yp += [k_f[early] * jnp.exp2(rf - b_f[early]), k_b[late] * jnp.exp2(rb - b_b[late])]
            xs.append(jnp.concatenate(xp, axis=0))
            ys.append(jnp.concatenate(yp, axis=0))
        xs.append(q)
        ys.append(k_f + k_b)
        st["xs"] = [x.astype(BF16) for x in xs]
        st["ys"] = [y.astype(BF16) for y in ys]
        sl = st["sl"]
        tot_f = last_f[c - 8:c]
        tot_b = first_b[0:8]
        st["qe_f"] = (q * jnp.exp2(b_f)).astype(BF16)
        st["kd_f"] = (k_f * jnp.exp2(repeat8(tot_f, c) - b_f)).astype(BF16)
        st["tot_f"] = jnp.exp2(tot_f[0:1])
        qb_s[sl, :] = (q * jnp.exp2(b_b)).astype(BF16)
        st["kd_b"] = (k_b * jnp.exp2(repeat8(tot_b, c) - b_b)).astype(BF16)
        etb_s[pl.ds(pl.multiple_of(ci * 8, 8), 8), :] = jnp.exp2(tot_b)
        return st

    def stage_scores(st):
        a = None
        for lvl in range(HG_LEVELS + 1):
            term = mask_ref[lvl] * _nt_dot(st["xs"][lvl], st["ys"][lvl])
            a = term if a is None else a + term
        st["a"] = a.astype(BF16)
        return st

    def fwd_body(i, state_t):
        cis = [i * n_step + j for j in range(n_step)]
        sts = [stage_gates(ci) for ci in cis]
        sts = [stage_operands(ci, st) for ci, st in zip(cis, sts)]
        sts = [stage_scores(st) for st in sts]
        intra = [_dot(st["a"], st["v"]) for st in sts]
        incs = [_dot(st["v"].T, jnp.concatenate([st["kd_f"], st["kd_b"]], axis=1)) for st in sts]
        for st, o, inc in zip(sts, intra, incs):
            acc_s[st["sl"], :] = o + _nt_dot(st["qe_f"], state_t.astype(BF16))
            state_t = state_t * st["tot_f"] + inc[:, 0:HG_DK]
            inc_s[st["sl"], :] = inc[:, HG_DK:2 * HG_DK]
        return state_t

    zero = jnp.zeros((HG_DK, HG_DK), F32)
    lax.fori_loop(0, n_chunks // n_step, fwd_body, zero)

    n_out = HG_OUT_CHUNKS_PER_STEP
    assert n_chunks % n_out == 0

    def bwd_body(i, state_t):
        cis = [n_chunks - 1 - (i * n_out + j) for j in range(n_out)]
        inter = []
        for ci in cis:
            inter.append(_nt_dot(qb_s[rows(ci), :], state_t.astype(BF16)))
            tot_b = etb_s[pl.ds(pl.multiple_of(ci * 8, 8), 1), :]
            state_t = state_t * tot_b + inc_s[rows(ci), :]
        for ci, o_inter in zip(cis, inter):
            sl = rows(ci)
            o = acc_s[sl, :] + o_inter
            ms = jnp.mean(o * o, axis=-1, keepdims=True)
            y = o * lax.rsqrt(ms + RMS_EPS) * gw_ref[...]
            g = g_ref[sl, :].astype(F32)
            o_ref[sl, :] = (y * (g * (1.0 / (1.0 + jnp.exp(-g))))).astype(o_ref.dtype)
        return state_t

    lax.fori_loop(0, n_chunks // n_out, bwd_body, zero)


def _hgrn2(proj, lb_fwd, lb_bwd, hg_norm_w, batch, seq):
    assert HG_DK == V7X_LANES and lb_fwd.shape[0] == 2
    cums, masks = _hg_tables()
    first = 3 * NA_WIDTH // V7X_LANES
    col = lambda k: pl.BlockSpec((seq, V7X_LANES),
                                 lambda b, h, k=k: (b, first + k * HG_HEADS + h))
    lbs = pl.BlockSpec((2, V7X_LANES), lambda b, h: (0, h))
    const = lambda a: pl.BlockSpec(a.shape, lambda b, h, nd=a.ndim: (0,) * nd)
    gw = hg_norm_w.astype(F32).reshape(1, HG_DK)
    return pl.pallas_call(
        functools.partial(_hgrn_kernel, seq=seq),
        grid=(batch, HG_HEADS),
        in_specs=[col(0), col(1), col(2), col(3), col(4), lbs, lbs, const(gw),
                  const(cums), const(masks)],
        out_specs=pl.BlockSpec((seq, V7X_LANES), lambda b, h: (b, h)),
        out_shape=jax.ShapeDtypeStruct((batch * seq, HG_WIDTH), BF16),
        scratch_shapes=[pltpu.VMEM((seq, HG_DK), F32), pltpu.VMEM((seq, HG_DK), BF16),
                        pltpu.VMEM((seq, HG_DK), F32),
                        pltpu.VMEM((8 * seq // HG_CHUNK, HG_DK), F32)],
        compiler_params=pltpu.CompilerParams(
            dimension_semantics=("arbitrary", "arbitrary"),
            vmem_limit_bytes=V7X_VMEM_LIMIT_BYTES),
        name="hgrn2",
    )(proj, proj, proj, proj, proj, lb_fwd.astype(F32), lb_bwd.astype(F32), gw, cums, masks)


def _out_mlp_kernel(x_ref, na_ref, hg_ref, wo_na_ref, wo_hg_ref, nw_ref, wu_ref, wd_ref, o_ref):
    h = x_ref[...] + _dot(na_ref[...], wo_na_ref[...]) + _dot(hg_ref[...], wo_hg_ref[...])
    ms = jnp.mean(h * h, axis=-1, keepdims=True)
    u = (h * lax.rsqrt(ms + RMS_EPS) * nw_ref[...]).astype(BF16)
    mlp = None
    for j in range(D_FF // FF_CHUNK):
        cols = slice(j * FF_CHUNK, (j + 1) * FF_CHUNK)
        a = jnp.maximum(_dot(u, wu_ref[:, cols]), 0.0)
        part = _dot((a * a).astype(BF16), wd_ref[cols, :])
        mlp = part if mlp is None else mlp + part
    o_ref[...] = h + mlp


def _out_mlp(x2, y_na, y_hg, w_out, norm_w, w_up, w_down):
    m = x2.shape[0]
    tile = lambda width: pl.BlockSpec((TOKEN_TILE, width), lambda i: (i, 0))
    resident = lambda a: pl.BlockSpec(a.shape, lambda i: (0, 0), pipeline_mode=pl.Buffered(1))
    wo_na, wo_hg = w_out[:NA_WIDTH], w_out[NA_WIDTH:]
    return pl.pallas_call(
        _out_mlp_kernel,
        grid=(m // TOKEN_TILE,),
        in_specs=[tile(D_MODEL), tile(NA_WIDTH), tile(HG_WIDTH), resident(wo_na),
                  resident(wo_hg), pl.BlockSpec((1, D_MODEL), lambda i: (0, 0)),
                  resident(w_up), resident(w_down)],
        out_specs=tile(D_MODEL),
        out_shape=jax.ShapeDtypeStruct((m, D_MODEL), F32),
        compiler_params=pltpu.CompilerParams(
            dimension_semantics=("arbitrary",), vmem_limit_bytes=V7X_VMEM_LIMIT_BYTES),
        name="out_mlp",
    )(x2, y_na, y_hg, wo_na, wo_hg, norm_w, w_up, w_down)


def kernel(x, w_in, w_out, attn_norm_w, mlp_norm_w, q_norm_w, k_norm_w, rpb, hg_norm_w,
           lb_fwd, lb_bwd, w_up, w_down):
    batch, seq, d = x.shape
    assert d == D_MODEL and w_in.shape[0] == 1 and seq % GRID_W == 0 and seq % HG_CHUNK == 0
    x2 = x.reshape(batch * seq, d)
    proj = _in_proj(x2, attn_norm_w[0].astype(F32).reshape(1, d), w_in[0].astype(BF16))
    y_na = _natten(proj, q_norm_w[0], k_norm_w[0], rpb[0], batch, seq)
    y_hg = _hgrn2(proj, lb_fwd, lb_bwd, hg_norm_w[0], batch, seq)
    out = _out_mlp(x2, y_na, y_hg, w_out[0].astype(BF16),
                   mlp_norm_w[0].astype(F32).reshape(1, d),
                   w_up[0].astype(BF16), w_down[0].astype(BF16))
    return out.reshape(batch, seq, d)
```

```python
import functools

import numpy as np
import jax
import jax.numpy as jnp
from jax import lax
from jax.experimental import pallas as pl
from jax.experimental.pallas import tpu as pltpu

F32 = jnp.float32
BF16 = jnp.bfloat16

D_MODEL = 1024
GRID_W = 64
NA_HEADS = 8
NA_HEAD_DIM = 64
NA_WIDTH = NA_HEADS * NA_HEAD_DIM
NA_WIN_R = 8
NA_WIN_C = 16
HG_HEADS = 4
HG_DK = 128
HG_WIDTH = HG_HEADS * HG_DK
D_FF = 4 * D_MODEL
D_IN_PROJ = 3 * NA_WIDTH + 5 * HG_WIDTH
RMS_EPS = 1e-6

V7X_LANES = 128
V7X_VMEM_LIMIT_BYTES = 56 * 1024 * 1024

TOKEN_TILE = 512
FF_CHUNK = 1024
NORM_ROWS = 512
HG_CHUNK = 128
HG_LEVELS = 7
HG_CHUNKS_PER_STEP = 8
HG_OUT_CHUNKS_PER_STEP = 8
NA_QT = 16
NA_KT = 32
NA_TILES = GRID_W // NA_QT
NA_GROUP_ROWS = 4
NA_BAND_ROWS = 12
NA_GROUPS_PER_STEP = 2
MASK_NEG = -1e30

_NA_QCOLS = [list(range(0, 8)) + list(range(56, 64)),
             list(range(8, 24)), list(range(24, 40)), list(range(40, 56))]
_NA_KSEGS = [[(0, 16), (48, 16)], [(0, 32)], [(16, 32)], [(32, 32)]]


def _nt_dot(a, b):
    return lax.dot_general(a, b, (((1,), (1,)), ((), ())), preferred_element_type=F32)


def _dot(a, b):
    return jnp.dot(a, b, preferred_element_type=F32)


def _split_bf16(x):
    hi = x.astype(BF16)
    lo = (x - hi.astype(F32)).astype(BF16)
    return hi, lo


def _in_proj_kernel(x_ref, nw_ref, w_ref, o_ref):
    x = x_ref[...]
    ms = jnp.mean(x * x, axis=-1, keepdims=True)
    u = (x * lax.rsqrt(ms + RMS_EPS) * nw_ref[...]).astype(BF16)
    res = _dot(u, w_ref[...])
    for j in range(D_IN_PROJ // V7X_LANES):
        o_ref[j] = res[:, j * V7X_LANES:(j + 1) * V7X_LANES].astype(o_ref.dtype)


def _in_proj(x2, norm_w, w_in):
    m = x2.shape[0]
    return pl.pallas_call(
        _in_proj_kernel,
        grid=(m // TOKEN_TILE,),
        in_specs=[
            pl.BlockSpec((TOKEN_TILE, D_MODEL), lambda i: (i, 0)),
            pl.BlockSpec((1, D_MODEL), lambda i: (0, 0)),
            pl.BlockSpec((D_MODEL, D_IN_PROJ), lambda i: (0, 0), pipeline_mode=pl.Buffered(1)),
        ],
        out_specs=pl.BlockSpec((D_IN_PROJ // V7X_LANES, TOKEN_TILE, V7X_LANES), lambda i: (0, i, 0)),
        out_shape=jax.ShapeDtypeStruct((D_IN_PROJ // V7X_LANES, m, V7X_LANES), BF16),
        compiler_params=pltpu.CompilerParams(
            dimension_semantics=("arbitrary",), vmem_limit_bytes=V7X_VMEM_LIMIT_BYTES),
        name="in_proj",
    )(x2, norm_w, w_in)


def _na_group_plan(rows):
    wr = min(NA_WIN_R, rows)
    bases, types, sigs = [], [], []
    for g in range(rows // NA_GROUP_ROWS):
        r0 = g * NA_GROUP_ROWS
        starts = [min(max(r0 + j - wr // 2, 0), rows - wr) for j in range(NA_GROUP_ROWS)]
        base = min(max(starts[0], 0), rows - NA_BAND_ROWS)
        assert base <= starts[0] and starts[-1] + wr <= base + NA_BAND_ROWS
        sig = tuple((starts[j] - base, r0 + j - base) for j in range(NA_GROUP_ROWS))
        if sig not in sigs:
            sigs.append(sig)
        bases.append(base)
        types.append(sigs.index(sig))
    return bases, types, sigs


def _na_bias_table(rpb, sigs):
    n_ri, n_ci = 2 * NA_WIN_R - 1, 2 * NA_WIN_C - 1
    csel = np.zeros((n_ci, NA_TILES, NA_QT, NA_KT), np.float32)
    for t in range(NA_TILES):
        qc = np.array(_NA_QCOLS[t])
        kc = np.concatenate([np.arange(c0, c0 + n) for c0, n in _NA_KSEGS[t]])
        cs = np.clip(qc - NA_WIN_C // 2, 0, GRID_W - NA_WIN_C)
        valid = (kc[None, :] >= cs[:, None]) & (kc[None, :] < cs[:, None] + NA_WIN_C)
        cidx = kc[None, :] - qc[:, None] + NA_WIN_C - 1
        qi, ki = np.nonzero(valid)
        csel[cidx[qi, ki], t, qi, ki] = 1.0
    cols = jnp.einsum("hrc,ctqk->htqrk", rpb.astype(F32), jnp.asarray(csel),
                      precision=lax.Precision.HIGHEST)
    cols = cols + jnp.asarray(np.where(csel.sum(0) > 0, 0.0, MASK_NEG)[None, :, :, None, :], F32)
    cols = cols.reshape(NA_HEADS // 2, 2, NA_TILES, NA_QT, n_ri, NA_KT)
    cols = jnp.transpose(cols, (0, 2, 1, 3, 4, 5))
    masked = lambda n: jnp.full(cols.shape[:4] + (n, NA_KT), MASK_NEG, F32)
    tabs = []
    for sig in sigs:
        per_row = []
        for first, qrow in sig:
            ri0 = first - qrow + NA_WIN_R - 1
            after = NA_BAND_ROWS - first - NA_WIN_R
            parts = ([masked(first)] if first else []) + [cols[..., ri0:ri0 + NA_WIN_R, :]]
            per_row.append(jnp.concatenate(parts + ([masked(after)] if after else []), axis=4))
        tabs.append(jnp.stack(per_row, axis=2))
    return jnp.stack(tabs).reshape(len(sigs), NA_HEADS // 2, NA_TILES,
                                   NA_GROUP_ROWS * 2 * NA_QT, NA_BAND_ROWS * NA_KT)


def _natten_kernel(q_ref, k_ref, v_ref, qw_ref, kw_ref, bias_ref, hm_ref, o_ref, qn_s, kn_s,
                   *, rows, bases, types):
    seq = rows * GRID_W
    hm = hm_ref[...]

    def head_rmsnorm(x, w):
        hi, lo = _split_bf16(x * x)
        ms = _dot(jnp.concatenate([hi, lo], axis=1), hm)
        return x * lax.rsqrt(ms + RMS_EPS) * w

    def norm_body(i, carry):
        sl = pl.ds(pl.multiple_of(i * NORM_ROWS, NORM_ROWS), NORM_ROWS)
        qn = head_rmsnorm(q_ref[sl, :].astype(F32), qw_ref[...])
        qn_s[sl, :] = qn * (NA_HEAD_DIM ** -0.5)
        kn_s[sl, :] = head_rmsnorm(k_ref[sl, :].astype(F32), kw_ref[...]).astype(BF16)
        return carry

    lax.fori_loop(0, seq // NORM_ROWS, norm_body, 0)

    lane = lax.broadcasted_iota(jnp.int32, (NA_QT, V7X_LANES), 1)
    first_head = lane < NA_HEAD_DIM
    gr = NA_GROUP_ROWS

    def row_block(ref, grid_row):
        if isinstance(grid_row, int):
            return ref[grid_row * GRID_W:(grid_row + 1) * GRID_W, :]
        return ref[pl.ds(pl.multiple_of(grid_row * GRID_W, GRID_W), GRID_W), :]

    def band(blocks, t):
        return jnp.concatenate(
            [blk[c0:c0 + n] for blk in blocks for (c0, n) in _NA_KSEGS[t]], axis=0)

    def step(groups):
        scores = []
        for row0, base, y in groups:
            kblocks = [row_block(kn_s, base + u) for u in range(NA_BAND_ROWS)]
            qrows = [row_block(qn_s, row0 + j) for j in range(gr)]
            for t in range(NA_TILES):
                qc = _NA_QCOLS[t]
                q2 = []
                for j in range(gr):
                    if t == 0:
                        qt = jnp.concatenate([qrows[j][0:8], qrows[j][56:64]], axis=0)
                    else:
                        qt = qrows[j][qc[0]:qc[0] + NA_QT]
                    q2 += [jnp.where(first_head, qt, 0.0), jnp.where(first_head, 0.0, qt)]
                s = _nt_dot(jnp.concatenate(q2, axis=0).astype(BF16), band(kblocks, t))
                scores.append(s + bias_ref[y, t])
        probs, sums = [], []
        for s in scores:
            p = jnp.exp(s - jnp.max(s, axis=-1, keepdims=True))
            sums.append(jnp.sum(p, axis=-1, keepdims=True))
            probs.append(p.astype(BF16))
        for g, (row0, base, y) in enumerate(groups):
            vblocks = [row_block(v_ref, base + u) for u in range(NA_BAND_ROWS)]
            o_tiles = []
            for t in range(NA_TILES):
                o2 = _dot(probs[g * NA_TILES + t], band(vblocks, t)) / sums[g * NA_TILES + t]
                o_tiles.append(o2)
            for j in range(gr):
                sel = [jnp.where(first_head, o[2 * NA_QT * j:2 * NA_QT * j + NA_QT],
                                 o[2 * NA_QT * j + NA_QT:2 * NA_QT * (j + 1)]) for o in o_tiles]
                orow = jnp.concatenate([sel[0][0:8], sel[1], sel[2], sel[3], sel[0][8:16]], axis=0)
                if isinstance(row0, int):
                    osl = slice((row0 + j) * GRID_W, (row0 + j + 1) * GRID_W)
                else:
                    osl = pl.ds(pl.multiple_of((row0 + j) * GRID_W, GRID_W), GRID_W)
                o_ref[osl, :] = orow.astype(o_ref.dtype)

    n_groups = rows // gr
    shift = min(NA_WIN_R, rows) // 2
    regular = [bases[g] == g * gr - shift for g in range(n_groups)]
    lo = regular.index(True)
    hi = n_groups - regular[::-1].index(True)
    per = NA_GROUPS_PER_STEP
    assert all(regular[lo:hi]) and (hi - lo) % per == 0
    assert len({types[g] for g in range(lo, hi)}) == 1
    edge = [(g * gr, bases[g], types[g]) for g in list(range(lo)) + list(range(hi, n_groups))]
    for i in range(0, len(edge), per):
        step(edge[i:i + per])

    def regular_body(i, carry):
        g0 = lo + i * per
        step([((g0 + n) * gr, (g0 + n) * gr - shift, types[lo]) for n in range(per)])
        return carry

    lax.fori_loop(0, (hi - lo) // per, regular_body, 0)


def _natten(proj, q_norm_w, k_norm_w, rpb, batch, seq):
    rows = seq // GRID_W
    assert rows >= NA_BAND_ROWS and rows % NA_GROUP_ROWS == 0
    assert GRID_W == 64 and NA_HEAD_DIM * 2 == V7X_LANES
    pairs = NA_HEADS // 2
    bases, types, sigs = _na_group_plan(rows)
    bias = _na_bias_table(rpb, sigs)
    qw = jnp.tile(q_norm_w.astype(F32).reshape(1, NA_HEAD_DIM), (1, 2))
    kw = jnp.tile(k_norm_w.astype(F32).reshape(1, NA_HEAD_DIM), (1, 2))
    head_id = np.arange(V7X_LANES) // NA_HEAD_DIM
    hm = np.tile((head_id[:, None] == head_id[None, :]) / NA_HEAD_DIM, (2, 1))
    hm = jnp.asarray(hm, BF16)
    col = lambda off: pl.BlockSpec((None, seq, V7X_LANES), lambda b, p, off=off: (off + p, b, 0))
    const2 = lambda shape: pl.BlockSpec(shape, lambda b, p: (0, 0))
    return pl.pallas_call(
        functools.partial(_natten_kernel, rows=rows, bases=tuple(bases), types=tuple(types)),
        grid=(batch, pairs),
        in_specs=[
            col(0), col(pairs), col(2 * pairs),
            const2((1, V7X_LANES)), const2((1, V7X_LANES)),
            pl.BlockSpec((len(sigs), None, NA_TILES, NA_GROUP_ROWS * 2 * NA_QT,
                          NA_BAND_ROWS * NA_KT), lambda b, p: (0, p, 0, 0, 0)),
            const2((2 * V7X_LANES, V7X_LANES)),
        ],
        out_specs=pl.BlockSpec((None, seq, V7X_LANES), lambda b, p: (p, b, 0)),
        out_shape=jax.ShapeDtypeStruct((pairs, batch * seq, V7X_LANES), BF16),
        scratch_shapes=[pltpu.VMEM((seq, V7X_LANES), F32), pltpu.VMEM((seq, V7X_LANES), BF16)],
        compiler_params=pltpu.CompilerParams(
            dimension_semantics=("arbitrary", "arbitrary"),
            vmem_limit_bytes=V7X_VMEM_LIMIT_BYTES),
        name="natten",
    )(proj, proj, proj, qw, kw, bias, hm)


def _hg_tables():
    c = HG_CHUNK
    idx = np.arange(c)
    lower = (idx[None, :] <= idx[:, None]).astype(np.float32)
    cums = np.stack([np.concatenate([lower, lower], axis=1),
                     np.concatenate([lower.T, lower.T], axis=1)])
    masks = []
    for lvl in range(HG_LEVELS):
        s = 1 << lvl
        same_block = (idx[:, None] // (2 * s)) == (idx[None, :] // (2 * s))
        half = (idx // s) % 2
        masks.append(same_block & (half[:, None] != half[None, :]))
    masks.append(np.eye(c, dtype=bool))
    return jnp.asarray(cums, BF16), jnp.asarray(np.stack(masks), F32)


def _hgrn_kernel(q_ref, ff_ref, fb_ref, v_ref, g_ref, lbf_ref, lbb_ref, gw_ref,
                 cum_ref, mask_ref, o_ref, acc_s, qb_s, inc_s, etb_s, *, seq):
    c = HG_CHUNK
    n_chunks = seq // c
    n_step = HG_CHUNKS_PER_STEP
    assert n_chunks % n_step == 0
    sub = lax.broadcasted_iota(jnp.int32, (c, V7X_LANES), 0) & 7

    def lower_bound(lb_ref):
        a = lb_ref[...].astype(F32)
        a0, a1 = a[0:1], a[1:2]
        m = jnp.maximum(a0, a1)
        e0, e1 = jnp.exp(a0 - m), jnp.exp(a1 - m)
        return e0 / (e0 + e1)

    lbf = lower_bound(lbf_ref)
    lbb = lower_bound(lbb_ref)

    def rows(ci):
        return pl.ds(pl.multiple_of(ci * c, c), c)

    def group_rows(x, r):
        return jnp.concatenate(
            [jnp.broadcast_to(x[8 * j + r:8 * j + r + 1], (8, V7X_LANES)) for j in range(c // 8)],
            axis=0)

    def repeat8(x8, n_rows):
        return x8 if n_rows == 8 else jnp.concatenate([x8] * (n_rows // 8), axis=0)

    def gates(f_ref, sl, lb):
        hf = f_ref[sl, :].astype(F32)
        f = lb + (1.0 - lb) * (1.0 / (1.0 + jnp.exp(-hf)))
        hi, lo = _split_bf16(jnp.log2(f))
        return f, 1.0 - f, jnp.concatenate([hi, lo], axis=0)

    def stage_gates(ci):
        sl = rows(ci)
        st = dict(sl=sl, q=q_ref[sl, :].astype(F32), v=v_ref[sl, :])
        st["f_f"], st["k_f"], lf_f = gates(ff_ref, sl, lbf)
        st["f_b"], st["k_b"], lf_b = gates(fb_ref, sl, lbb)
        st["b_f"] = _dot(cum_ref[0], lf_f)
        st["b_b"] = _dot(cum_ref[1], lf_b)
        return st

    def level_operands(st, lvl):
        q, k_f, k_b, b_f, b_b = st["q"], st["k_f"], st["k_b"], st["b_f"], st["b_b"]
        if lvl == HG_LEVELS:
            return q, k_f + k_b
        if lvl == 0:
            odd = (sub & 1) == 1
            return q * jnp.where(odd, st["f_f"], st["f_b"]), jnp.where(odd, k_b, k_f)
        if lvl < 3:
            later = (sub & (1 << lvl)) != 0
            if lvl == 1:
                ref_f = jnp.where(sub < 4, group_rows(b_f, 1), group_rows(b_f, 5))
                ref_b = jnp.where(sub < 4, group_rows(b_b, 2), group_rows(b_b, 6))
            else:
                ref_f, ref_b = group_rows(b_f, 3), group_rows(b_b, 4)
            d_f, d_b = b_f - ref_f, b_b - ref_b
            return (q * jnp.exp2(jnp.where(later, d_f, d_b)),
                    jnp.where(later, k_b, k_f) * jnp.exp2(-jnp.where(later, d_b, d_f)))
        s = 1 << lvl
        xp, yp = [], []
        for r0 in range(0, c, 2 * s):
            m = r0 + s
            rf = repeat8(st["last_f"][m - 8:m], s)
            rb = repeat8(st["first_b"][m:m + 8], s)
            early, late = slice(r0, m), slice(m, m + s)
            xp += [q[early] * jnp.exp2(b_b[early] - rb), q[late] * jnp.exp2(b_f[late] - rf)]
            yp += [k_f[early] * jnp.exp2(rf - b_f[early]), k_b[late] * jnp.exp2(rb - b_b[late])]
        return jnp.concatenate(xp, axis=0), jnp.concatenate(yp, axis=0)

    def stage_operands(ci, st):
        q, k_f, k_b, b_f, b_b = st["q"], st["k_f"], st["k_b"], st["b_f"], st["b_b"]
        last_f = st["last_f"] = group_rows(b_f, 7)
        first_b = st["first_b"] = group_rows(b_b, 0)
        sl = st["sl"]
        tot_f = last_f[c - 8:c]
        tot_b = first_b[0:8]
        st["qe_f"] = (q * jnp.exp2(b_f)).astype(BF16)
        st["kd_f"] = (k_f * jnp.exp2(repeat8(tot_f, c) - b_f)).astype(BF16)
        st["tot_f"] = jnp.exp2(tot_f[0:1])
        qb_s[sl, :] = (q * jnp.exp2(b_b)).astype(BF16)
        st["kd_b"] = (k_b * jnp.exp2(repeat8(tot_b, c) - b_b)).astype(BF16)
        etb_s[pl.ds(pl.multiple_of(ci * 8, 8), 8), :] = jnp.exp2(tot_b)
        return st

    def fwd_body(i, state_t):
        cis = [i * n_step + j for j in range(n_step)]
        sts = [stage_gates(ci) for ci in cis]
        sts = [stage_operands(ci, st) for ci, st in zip(cis, sts)]
        for st in sts:
            for lvl in range(HG_LEVELS + 1):
                x, y = level_operands(st, lvl)
                term = mask_ref[lvl] * _nt_dot(x.astype(BF16), y.astype(BF16))
                st["a"] = term if lvl == 0 else st["a"] + term
        intra = [_dot(st["a"].astype(BF16), st["v"]) for st in sts]
        incs = [_dot(st["v"].T, jnp.concatenate([st["kd_f"], st["kd_b"]], axis=1)) for st in sts]
        for st, o, inc in zip(sts, intra, incs):
            acc_s[st["sl"], :] = o + _nt_dot(st["qe_f"], state_t.astype(BF16))
            state_t = state_t * st["tot_f"] + inc[:, 0:HG_DK]
            inc_s[st["sl"], :] = inc[:, HG_DK:2 * HG_DK]
        return state_t

    zero = jnp.zeros((HG_DK, HG_DK), F32)
    lax.fori_loop(0, n_chunks // n_step, fwd_body, zero)

    n_out = HG_OUT_CHUNKS_PER_STEP
    assert n_chunks % n_out == 0

    def bwd_body(i, state_t):
        cis = [n_chunks - 1 - (i * n_out + j) for j in range(n_out)]
        inter = []
        for ci in cis:
            inter.append(_nt_dot(qb_s[rows(ci), :], state_t.astype(BF16)))
            tot_b = etb_s[pl.ds(pl.multiple_of(ci * 8, 8), 1), :]
            state_t = state_t * tot_b + inc_s[rows(ci), :]
        for ci, o_inter in zip(cis, inter):
            sl = rows(ci)
            o = acc_s[sl, :] + o_inter
            ms = jnp.mean(o * o, axis=-1, keepdims=True)
            y = o * lax.rsqrt(ms + RMS_EPS) * gw_ref[...]
            g = g_ref[sl, :].astype(F32)
            o_ref[sl, :] = (y * (g * (1.0 / (1.0 + jnp.exp(-g))))).astype(o_ref.dtype)
        return state_t

    lax.fori_loop(0, n_chunks // n_out, bwd_body, zero)


def _hgrn2(proj, lb_fwd, lb_bwd, hg_norm_w, batch, seq):
    assert HG_DK == V7X_LANES and lb_fwd.shape[0] == 2
    cums, masks = _hg_tables()
    first = 3 * NA_WIDTH // V7X_LANES
    col = lambda k: pl.BlockSpec((None, seq, V7X_LANES),
                                 lambda b, h, k=k: (first + k * HG_HEADS + h, b, 0))
    lbs = pl.BlockSpec((2, V7X_LANES), lambda b, h: (0, h))
    const = lambda a: pl.BlockSpec(a.shape, lambda b, h, nd=a.ndim: (0,) * nd)
    gw = hg_norm_w.astype(F32).reshape(1, HG_DK)
    return pl.pallas_call(
        functools.partial(_hgrn_kernel, seq=seq),
        grid=(batch, HG_HEADS),
        in_specs=[col(0), col(1), col(2), col(3), col(4), lbs, lbs, const(gw),
                  const(cums), const(masks)],
        out_specs=pl.BlockSpec((None, seq, V7X_LANES), lambda b, h: (h, b, 0)),
        out_shape=jax.ShapeDtypeStruct((HG_HEADS, batch * seq, V7X_LANES), BF16),
        scratch_shapes=[pltpu.VMEM((seq, HG_DK), F32), pltpu.VMEM((seq, HG_DK), BF16),
                        pltpu.VMEM((seq, HG_DK), F32),
                        pltpu.VMEM((8 * seq // HG_CHUNK, HG_DK), F32)],
        compiler_params=pltpu.CompilerParams(
            dimension_semantics=("arbitrary", "arbitrary"),
            vmem_limit_bytes=V7X_VMEM_LIMIT_BYTES),
        name="hgrn2",
    )(proj, proj, proj, proj, proj, lb_fwd.astype(F32), lb_bwd.astype(F32), gw, cums, masks)


def _out_mlp_kernel(x_ref, na_ref, hg_ref, wo_na_ref, wo_hg_ref, nw_ref, wu_ref, wd_ref, o_ref):
    wide = lambda ref: jnp.concatenate([ref[j] for j in range(ref.shape[0])], axis=1)
    h = x_ref[...] + _dot(wide(na_ref), wo_na_ref[...]) + _dot(wide(hg_ref), wo_hg_ref[...])
    ms = jnp.mean(h * h, axis=-1, keepdims=True)
    u = (h * lax.rsqrt(ms + RMS_EPS) * nw_ref[...]).astype(BF16)
    mlp = None
    for j in range(D_FF // FF_CHUNK):
        cols = slice(j * FF_CHUNK, (j + 1) * FF_CHUNK)
        a = jnp.maximum(_dot(u, wu_ref[:, cols]), 0.0)
        part = _dot((a * a).astype(BF16), wd_ref[cols, :])
        mlp = part if mlp is None else mlp + part
    o_ref[...] = h + mlp


def _out_mlp(x2, y_na, y_hg, w_out, norm_w, w_up, w_down):
    m = x2.shape[0]
    tile = lambda width: pl.BlockSpec((TOKEN_TILE, width), lambda i: (i, 0))
    blocks = lambda a: pl.BlockSpec((a.shape[0], TOKEN_TILE, V7X_LANES), lambda i: (0, i, 0))
    resident = lambda a: pl.BlockSpec(a.shape, lambda i: (0, 0), pipeline_mode=pl.Buffered(1))
    wo_na, wo_hg = w_out[:NA_WIDTH], w_out[NA_WIDTH:]
    return pl.pallas_call(
        _out_mlp_kernel,
        grid=(m // TOKEN_TILE,),
        in_specs=[tile(D_MODEL), blocks(y_na), blocks(y_hg), resident(wo_na),
                  resident(wo_hg), pl.BlockSpec((1, D_MODEL), lambda i: (0, 0)),
                  resident(w_up), resident(w_down)],
        out_specs=tile(D_MODEL),
        out_shape=jax.ShapeDtypeStruct((m, D_MODEL), F32),
        compiler_params=pltpu.CompilerParams(
            dimension_semantics=("arbitrary",), vmem_limit_bytes=V7X_VMEM_LIMIT_BYTES),
        name="out_mlp",
    )(x2, y_na, y_hg, wo_na, wo_hg, norm_w, w_up, w_down)


def kernel(x, w_in, w_out, attn_norm_w, mlp_norm_w, q_norm_w, k_norm_w, rpb, hg_norm_w,
           lb_fwd, lb_bwd, w_up, w_down):
    batch, seq, d = x.shape
    assert d == D_MODEL and w_in.shape[0] == 1 and seq % GRID_W == 0 and seq % HG_CHUNK == 0
    x2 = x.reshape(batch * seq, d)
    proj = _in_proj(x2, attn_norm_w[0].astype(F32).reshape(1, d), w_in[0].astype(BF16))
    y_na = _natten(proj, q_norm_w[0], k_norm_w[0], rpb[0], batch, seq)
    y_hg = _hgrn2(proj, lb_fwd, lb_bwd, hg_norm_w[0], batch, seq)
    out = _out_mlp(x2, y_na, y_hg, w_out[0].astype(BF16),
                   mlp_norm_w[0].astype(F32).reshape(1, d),
                   w_up[0].astype(BF16), w_down[0].astype(BF16))
    return out.reshape(batch, seq, d)
```

```python
import functools

import numpy as np
import jax
import jax.numpy as jnp
from jax import lax
from jax.experimental import pallas as pl
from jax.experimental.pallas import tpu as pltpu

F32 = jnp.float32
BF16 = jnp.bfloat16

D_MODEL = 1024
GRID_W = 64
NA_HEADS = 8
NA_HEAD_DIM = 64
NA_WIDTH = NA_HEADS * NA_HEAD_DIM
NA_WIN_R = 8
NA_WIN_C = 16
HG_HEADS = 4
HG_DK = 128
HG_WIDTH = HG_HEADS * HG_DK
D_FF = 4 * D_MODEL
D_IN_PROJ = 3 * NA_WIDTH + 5 * HG_WIDTH
RMS_EPS = 1e-6

V7X_LANES = 128
V7X_VMEM_LIMIT_BYTES = 56 * 1024 * 1024

TOKEN_TILE = 512
FF_CHUNK = 1024
NORM_ROWS = 512
HG_CHUNK = 128
HG_LEVELS = 7
HG_CHUNKS_PER_STEP = 8
HG_OUT_CHUNKS_PER_STEP = 8
NA_QT = 16
NA_KT = 32
NA_TILES = GRID_W // NA_QT
NA_GROUP_ROWS = 4
NA_BAND_ROWS = 12
NA_GROUPS_PER_STEP = 2
MASK_NEG = -1e30

_NA_QCOLS = [list(range(0, 8)) + list(range(56, 64)),
             list(range(8, 24)), list(range(24, 40)), list(range(40, 56))]
_NA_KSEGS = [[(0, 16), (48, 16)], [(0, 32)], [(16, 32)], [(32, 32)]]


def _nt_dot(a, b):
    return lax.dot_general(a, b, (((1,), (1,)), ((), ())), preferred_element_type=F32)


def _dot(a, b):
    return jnp.dot(a, b, preferred_element_type=F32)


def _split_bf16(x):
    hi = x.astype(BF16)
    lo = (x - hi.astype(F32)).astype(BF16)
    return hi, lo


def _in_proj_kernel(x_ref, nw_ref, w_ref, o_ref):
    x = x_ref[...]
    ms = jnp.mean(x * x, axis=-1, keepdims=True)
    u = (x * lax.rsqrt(ms + RMS_EPS) * nw_ref[...]).astype(BF16)
    res = _dot(u, w_ref[...])
    for j in range(D_IN_PROJ // V7X_LANES):
        o_ref[j] = res[:, j * V7X_LANES:(j + 1) * V7X_LANES].astype(o_ref.dtype)


def _in_proj(x2, norm_w, w_in):
    m = x2.shape[0]
    return pl.pallas_call(
        _in_proj_kernel,
        grid=(m // TOKEN_TILE,),
        in_specs=[
            pl.BlockSpec((TOKEN_TILE, D_MODEL), lambda i: (i, 0)),
            pl.BlockSpec((1, D_MODEL), lambda i: (0, 0)),
            pl.BlockSpec((D_MODEL, D_IN_PROJ), lambda i: (0, 0), pipeline_mode=pl.Buffered(1)),
        ],
        out_specs=pl.BlockSpec((D_IN_PROJ // V7X_LANES, TOKEN_TILE, V7X_LANES), lambda i: (0, i, 0)),
        out_shape=jax.ShapeDtypeStruct((D_IN_PROJ // V7X_LANES, m, V7X_LANES), BF16),
        compiler_params=pltpu.CompilerParams(
            dimension_semantics=("arbitrary",), vmem_limit_bytes=V7X_VMEM_LIMIT_BYTES),
        name="in_proj",
    )(x2, norm_w, w_in)


def _na_group_plan(rows):
    wr = min(NA_WIN_R, rows)
    bases, types, sigs = [], [], []
    for g in range(rows // NA_GROUP_ROWS):
        r0 = g * NA_GROUP_ROWS
        starts = [min(max(r0 + j - wr // 2, 0), rows - wr) for j in range(NA_GROUP_ROWS)]
        base = min(max(starts[0], 0), rows - NA_BAND_ROWS)
        assert base <= starts[0] and starts[-1] + wr <= base + NA_BAND_ROWS
        sig = tuple((starts[j] - base, r0 + j - base) for j in range(NA_GROUP_ROWS))
        if sig not in sigs:
            sigs.append(sig)
        bases.append(base)
        types.append(sigs.index(sig))
    return bases, types, sigs


def _na_bias_table(rpb, sigs):
    n_ri, n_ci = 2 * NA_WIN_R - 1, 2 * NA_WIN_C - 1
    csel = np.zeros((n_ci, NA_TILES, NA_QT, NA_KT), np.float32)
    for t in range(NA_TILES):
        qc = np.array(_NA_QCOLS[t])
        kc = np.concatenate([np.arange(c0, c0 + n) for c0, n in _NA_KSEGS[t]])
        cs = np.clip(qc - NA_WIN_C // 2, 0, GRID_W - NA_WIN_C)
        valid = (kc[None, :] >= cs[:, None]) & (kc[None, :] < cs[:, None] + NA_WIN_C)
        cidx = kc[None, :] - qc[:, None] + NA_WIN_C - 1
        qi, ki = np.nonzero(valid)
        csel[cidx[qi, ki], t, qi, ki] = 1.0
    cols = jnp.einsum("hrc,ctqk->htqrk", rpb.astype(F32), jnp.asarray(csel),
                      precision=lax.Precision.HIGHEST)
    cols = cols + jnp.asarray(np.where(csel.sum(0) > 0, 0.0, MASK_NEG)[None, :, :, None, :], F32)
    cols = cols.reshape(NA_HEADS // 2, 2, NA_TILES, NA_QT, n_ri * NA_KT)
    cols = jnp.transpose(cols, (0, 2, 1, 3, 4))
    tabs = []
    for sig in sigs:
        per_row = []
        for first, qrow in sig:
            ri0 = first - qrow + NA_WIN_R - 1
            after = NA_BAND_ROWS - first - NA_WIN_R
            window = cols[..., ri0 * NA_KT:(ri0 + NA_WIN_R) * NA_KT]
            per_row.append(jnp.pad(window, [(0, 0)] * 4 + [(first * NA_KT, after * NA_KT)],
                                   constant_values=MASK_NEG))
        tabs.append(jnp.stack(per_row, axis=2))
    return jnp.stack(tabs).reshape(len(sigs), NA_HEADS // 2, NA_TILES,
                                   NA_GROUP_ROWS * 2 * NA_QT, NA_BAND_ROWS * NA_KT)


def _natten_kernel(q_ref, k_ref, v_ref, qw_ref, kw_ref, bias_ref, hm_ref, o_ref, qn_s, kn_s,
                   *, rows, bases, types):
    seq = rows * GRID_W
    hm = hm_ref[...]

    def head_rmsnorm(x, w):
        hi, lo = _split_bf16(x * x)
        ms = _dot(jnp.concatenate([hi, lo], axis=1), hm)
        return x * lax.rsqrt(ms + RMS_EPS) * w

    def norm_body(i, carry):
        sl = pl.ds(pl.multiple_of(i * NORM_ROWS, NORM_ROWS), NORM_ROWS)
        qn = head_rmsnorm(q_ref[sl, :].astype(F32), qw_ref[...])
        qn_s[sl, :] = qn * (NA_HEAD_DIM ** -0.5)
        kn_s[sl, :] = head_rmsnorm(k_ref[sl, :].astype(F32), kw_ref[...]).astype(BF16)
        return carry

    lax.fori_loop(0, seq // NORM_ROWS, norm_body, 0)

    lane = lax.broadcasted_iota(jnp.int32, (NA_QT, V7X_LANES), 1)
    first_head = lane < NA_HEAD_DIM
    gr = NA_GROUP_ROWS

    def row_block(ref, grid_row):
        if isinstance(grid_row, int):
            return ref[grid_row * GRID_W:(grid_row + 1) * GRID_W, :]
        return ref[pl.ds(pl.multiple_of(grid_row * GRID_W, GRID_W), GRID_W), :]

    def band(blocks, t):
        return jnp.concatenate(
            [blk[c0:c0 + n] for blk in blocks for (c0, n) in _NA_KSEGS[t]], axis=0)

    def step(groups):
        scores = []
        for row0, base, y in groups:
            kblocks = [row_block(kn_s, base + u) for u in range(NA_BAND_ROWS)]
            qrows = [row_block(qn_s, row0 + j) for j in range(gr)]
            for t in range(NA_TILES):
                qc = _NA_QCOLS[t]
                q2 = []
                for j in range(gr):
                    if t == 0:
                        qt = jnp.concatenate([qrows[j][0:8], qrows[j][56:64]], axis=0)
                    else:
                        qt = qrows[j][qc[0]:qc[0] + NA_QT]
                    q2 += [jnp.where(first_head, qt, 0.0), jnp.where(first_head, 0.0, qt)]
                s = _nt_dot(jnp.concatenate(q2, axis=0).astype(BF16), band(kblocks, t))
                scores.append(s + bias_ref[y, t])
        probs, sums = [], []
        for s in scores:
            p = jnp.exp(s - jnp.max(s, axis=-1, keepdims=True))
            sums.append(jnp.sum(p, axis=-1, keepdims=True))
            probs.append(p.astype(BF16))
        for g, (row0, base, y) in enumerate(groups):
            vblocks = [row_block(v_ref, base + u) for u in range(NA_BAND_ROWS)]
            o_tiles = []
            for t in range(NA_TILES):
                o2 = _dot(probs[g * NA_TILES + t], band(vblocks, t)) / sums[g * NA_TILES + t]
                o_tiles.append(o2)
            for j in range(gr):
                sel = [jnp.where(first_head, o[2 * NA_QT * j:2 * NA_QT * j + NA_QT],
                                 o[2 * NA_QT * j + NA_QT:2 * NA_QT * (j + 1)]) for o in o_tiles]
                orow = jnp.concatenate([sel[0][0:8], sel[1], sel[2], sel[3], sel[0][8:16]], axis=0)
                if isinstance(row0, int):
                    osl = slice((row0 + j) * GRID_W, (row0 + j + 1) * GRID_W)
                else:
                    osl = pl.ds(pl.multiple_of((row0 + j) * GRID_W, GRID_W), GRID_W)
                o_ref[osl, :] = orow.astype(o_ref.dtype)

    n_groups = rows // gr
    shift = min(NA_WIN_R, rows) // 2
    regular = [bases[g] == g * gr - shift for g in range(n_groups)]
    lo = regular.index(True)
    hi = n_groups - regular[::-1].index(True)
    per = NA_GROUPS_PER_STEP
    assert all(regular[lo:hi]) and (hi - lo) % per == 0
    assert len({types[g] for g in range(lo, hi)}) == 1
    edge = [(g * gr, bases[g], types[g]) for g in list(range(lo)) + list(range(hi, n_groups))]
    for i in range(0, len(edge), per):
        step(edge[i:i + per])

    def regular_body(i, carry):
        g0 = lo + i * per
        step([((g0 + n) * gr, (g0 + n) * gr - shift, types[lo]) for n in range(per)])
        return carry

    lax.fori_loop(0, (hi - lo) // per, regular_body, 0)


def _natten(proj, q_norm_w, k_norm_w, rpb, batch, seq):
    rows = seq // GRID_W
    assert rows >= NA_BAND_ROWS and rows % NA_GROUP_ROWS == 0
    assert GRID_W == 64 and NA_HEAD_DIM * 2 == V7X_LANES
    pairs = NA_HEADS // 2
    bases, types, sigs = _na_group_plan(rows)
    bias = _na_bias_table(rpb, sigs)
    qw = jnp.tile(q_norm_w.astype(F32).reshape(1, NA_HEAD_DIM), (1, 2))
    kw = jnp.tile(k_norm_w.astype(F32).reshape(1, NA_HEAD_DIM), (1, 2))
    head_id = np.arange(V7X_LANES) // NA_HEAD_DIM
    hm = np.tile((head_id[:, None] == head_id[None, :]) / NA_HEAD_DIM, (2, 1))
    hm = jnp.asarray(hm, BF16)
    col = lambda off: pl.BlockSpec((None, seq, V7X_LANES), lambda p, b, off=off: (off + p, b, 0))
    const2 = lambda shape: pl.BlockSpec(shape, lambda p, b: (0, 0))
    return pl.pallas_call(
        functools.partial(_natten_kernel, rows=rows, bases=tuple(bases), types=tuple(types)),
        grid=(pairs, batch),
        in_specs=[
            col(0), col(pairs), col(2 * pairs),
            const2((1, V7X_LANES)), const2((1, V7X_LANES)),
            pl.BlockSpec((len(sigs), None, NA_TILES, NA_GROUP_ROWS * 2 * NA_QT,
                          NA_BAND_ROWS * NA_KT), lambda p, b: (0, p, 0, 0, 0)),
            const2((2 * V7X_LANES, V7X_LANES)),
        ],
        out_specs=pl.BlockSpec((None, seq, V7X_LANES), lambda p, b: (p, b, 0)),
        out_shape=jax.ShapeDtypeStruct((pairs, batch * seq, V7X_LANES), BF16),
        scratch_shapes=[pltpu.VMEM((seq, V7X_LANES), F32), pltpu.VMEM((seq, V7X_LANES), BF16)],
        compiler_params=pltpu.CompilerParams(
            dimension_semantics=("arbitrary", "arbitrary"),
            vmem_limit_bytes=V7X_VMEM_LIMIT_BYTES),
        name="natten",
    )(proj, proj, proj, qw, kw, bias, hm)


def _hg_tables():
    c = HG_CHUNK
    idx = np.arange(c)
    lower = (idx[None, :] <= idx[:, None]).astype(np.float32)
    cums = np.stack([np.concatenate([lower, lower], axis=1),
                     np.concatenate([lower.T, lower.T], axis=1)])
    masks = []
    for lvl in range(HG_LEVELS):
        s = 1 << lvl
        same_block = (idx[:, None] // (2 * s)) == (idx[None, :] // (2 * s))
        half = (idx // s) % 2
        masks.append(same_block & (half[:, None] != half[None, :]))
    masks.append(np.eye(c, dtype=bool))
    return jnp.asarray(cums, BF16), jnp.asarray(np.stack(masks), F32)


def _hgrn_kernel(q_ref, ff_ref, fb_ref, v_ref, g_ref, lbf_ref, lbb_ref, gw_ref,
                 cum_ref, mask_ref, o_ref, acc_s, qb_s, inc_s, etb_s, *, seq):
    c = HG_CHUNK
    n_chunks = seq // c
    n_step = HG_CHUNKS_PER_STEP
    assert n_chunks % n_step == 0
    sub = lax.broadcasted_iota(jnp.int32, (c, V7X_LANES), 0) & 7

    def lower_bound(lb_ref):
        a = lb_ref[...].astype(F32)
        a0, a1 = a[0:1], a[1:2]
        m = jnp.maximum(a0, a1)
        e0, e1 = jnp.exp(a0 - m), jnp.exp(a1 - m)
        return e0 / (e0 + e1)

    lbf = lower_bound(lbf_ref)
    lbb = lower_bound(lbb_ref)

    def rows(ci):
        return pl.ds(pl.multiple_of(ci * c, c), c)

    def group_rows(x, r):
        return jnp.concatenate(
            [jnp.broadcast_to(x[8 * j + r:8 * j + r + 1], (8, V7X_LANES)) for j in range(c // 8)],
            axis=0)

    def repeat8(x8, n_rows):
        return x8 if n_rows == 8 else jnp.concatenate([x8] * (n_rows // 8), axis=0)

    def gates(f_ref, sl, lb):
        hf = f_ref[sl, :].astype(F32)
        f = lb + (1.0 - lb) * (1.0 / (1.0 + jnp.exp(-hf)))
        hi, lo = _split_bf16(jnp.log2(f))
        return f, 1.0 - f, jnp.concatenate([hi, lo], axis=0)

    def stage_gates(ci):
        sl = rows(ci)
        st = dict(sl=sl, q=q_ref[sl, :].astype(F32), v=v_ref[sl, :])
        st["f_f"], st["k_f"], lf_f = gates(ff_ref, sl, lbf)
        st["f_b"], st["k_b"], lf_b = gates(fb_ref, sl, lbb)
        st["b_f"] = _dot(cum_ref[0], lf_f)
        st["b_b"] = _dot(cum_ref[1], lf_b)
        return st

    def level_operands(st, lvl):
        q, k_f, k_b, b_f, b_b = st["q"], st["k_f"], st["k_b"], st["b_f"], st["b_b"]
        if lvl == HG_LEVELS:
            return q, k_f + k_b
        if lvl == 0:
            odd = (sub & 1) == 1
            return q * jnp.where(odd, st["f_f"], st["f_b"]), jnp.where(odd, k_b, k_f)
        if lvl < 3:
            later = (sub & (1 << lvl)) != 0
            if lvl == 1:
                ref_f = jnp.where(sub < 4, group_rows(b_f, 1), group_rows(b_f, 5))
                ref_b = jnp.where(sub < 4, group_rows(b_b, 2), group_rows(b_b, 6))
            else:
                ref_f, ref_b = group_rows(b_f, 3), group_rows(b_b, 4)
            d_f, d_b = b_f - ref_f, b_b - ref_b
            return (q * jnp.exp2(jnp.where(later, d_f, d_b)),
                    jnp.where(later, k_b, k_f) * jnp.exp2(-jnp.where(later, d_b, d_f)))
        s = 1 << lvl
        xp, yp = [], []
        for r0 in range(0, c, 2 * s):
            m = r0 + s
            rf = repeat8(st["last_f"][m - 8:m], s)
            rb = repeat8(st["first_b"][m:m + 8], s)
            early, late = slice(r0, m), slice(m, m + s)
            xp += [q[early] * jnp.exp2(b_b[early] - rb), q[late] * jnp.exp2(b_f[late] - rf)]
            yp += [k_f[early] * jnp.exp2(rf - b_f[early]), k_b[late] * jnp.exp2(rb - b_b[late])]
        return jnp.concatenate(xp, axis=0), jnp.concatenate(yp, axis=0)

    def stage_operands(ci, st):
        q, k_f, k_b, b_f, b_b = st["q"], st["k_f"], st["k_b"], st["b_f"], st["b_b"]
        last_f = st["last_f"] = group_rows(b_f, 7)
        first_b = st["first_b"] = group_rows(b_b, 0)
        sl = st["sl"]
        tot_f = last_f[c - 8:c]
        tot_b = first_b[0:8]
        st["qe_f"] = (q * jnp.exp2(b_f)).astype(BF16)
        st["kd_f"] = (k_f * jnp.exp2(repeat8(tot_f, c) - b_f)).astype(BF16)
        st["tot_f"] = jnp.exp2(tot_f[0:1])
        qb_s[sl, :] = (q * jnp.exp2(b_b)).astype(BF16)
        st["kd_b"] = (k_b * jnp.exp2(repeat8(tot_b, c) - b_b)).astype(BF16)
        etb_s[pl.ds(pl.multiple_of(ci * 8, 8), 8), :] = jnp.exp2(tot_b)
        return st

    def fwd_body(i, state_t):
        cis = [i * n_step + j for j in range(n_step)]
        sts = [stage_gates(ci) for ci in cis]
        sts = [stage_operands(ci, st) for ci, st in zip(cis, sts)]
        for st in sts:
            for lvl in range(HG_LEVELS + 1):
                x, y = level_operands(st, lvl)
                term = mask_ref[lvl] * _nt_dot(x.astype(BF16), y.astype(BF16))
                st["a"] = term if lvl == 0 else st["a"] + term
        intra = [_dot(st["a"].astype(BF16), st["v"]) for st in sts]
        incs = [_dot(st["v"].T, jnp.concatenate([st["kd_f"], st["kd_b"]], axis=1)) for st in sts]
        for st, o, inc in zip(sts, intra, incs):
            acc_s[st["sl"], :] = o + _nt_dot(st["qe_f"], state_t.astype(BF16))
            state_t = state_t * st["tot_f"] + inc[:, 0:HG_DK]
            inc_s[st["sl"], :] = inc[:, HG_DK:2 * HG_DK]
        return state_t

    zero = jnp.zeros((HG_DK, HG_DK), F32)
    lax.fori_loop(0, n_chunks // n_step, fwd_body, zero)

    n_out = HG_OUT_CHUNKS_PER_STEP
    assert n_chunks % n_out == 0

    def bwd_body(i, state_t):
        cis = [n_chunks - 1 - (i * n_out + j) for j in range(n_out)]
        inter = []
        for ci in cis:
            inter.append(_nt_dot(qb_s[rows(ci), :], state_t.astype(BF16)))
            tot_b = etb_s[pl.ds(pl.multiple_of(ci * 8, 8), 1), :]
            state_t = state_t * tot_b + inc_s[rows(ci), :]
        for ci, o_inter in zip(cis, inter):
            sl = rows(ci)
            o = acc_s[sl, :] + o_inter
            ms = jnp.mean(o * o, axis=-1, keepdims=True)
            y = o * lax.rsqrt(ms + RMS_EPS) * gw_ref[...]
            g = g_ref[sl, :].astype(F32)
            o_ref[sl, :] = (y * (g * (1.0 / (1.0 + jnp.exp(-g))))).astype(o_ref.dtype)
        return state_t

    lax.fori_loop(0, n_chunks // n_out, bwd_body, zero)


def _hgrn2(proj, lb_fwd, lb_bwd, hg_norm_w, batch, seq):
    assert HG_DK == V7X_LANES and lb_fwd.shape[0] == 2
    cums, masks = _hg_tables()
    first = 3 * NA_WIDTH // V7X_LANES
    col = lambda k: pl.BlockSpec((None, seq, V7X_LANES),
                                 lambda b, h, k=k: (first + k * HG_HEADS + h, b, 0))
    lbs = pl.BlockSpec((2, V7X_LANES), lambda b, h: (0, h))
    const = lambda a: pl.BlockSpec(a.shape, lambda b, h, nd=a.ndim: (0,) * nd)
    gw = hg_norm_w.astype(F32).reshape(1, HG_DK)
    return pl.pallas_call(
        functools.partial(_hgrn_kernel, seq=seq),
        grid=(batch, HG_HEADS),
        in_specs=[col(0), col(1), col(2), col(3), col(4), lbs, lbs, const(gw),
                  const(cums), const(masks)],
        out_specs=pl.BlockSpec((None, seq, V7X_LANES), lambda b, h: (h, b, 0)),
        out_shape=jax.ShapeDtypeStruct((HG_HEADS, batch * seq, V7X_LANES), BF16),
        scratch_shapes=[pltpu.VMEM((seq, HG_DK), F32), pltpu.VMEM((seq, HG_DK), BF16),
                        pltpu.VMEM((seq, HG_DK), F32),
                        pltpu.VMEM((8 * seq // HG_CHUNK, HG_DK), F32)],
        compiler_params=pltpu.CompilerParams(
            dimension_semantics=("arbitrary", "arbitrary"),
            vmem_limit_bytes=V7X_VMEM_LIMIT_BYTES),
        name="hgrn2",
    )(proj, proj, proj, proj, proj, lb_fwd.astype(F32), lb_bwd.astype(F32), gw, cums, masks)


def _out_mlp_kernel(x_ref, na_ref, hg_ref, wo_na_ref, wo_hg_ref, nw_ref, wu_ref, wd_ref, o_ref):
    wide = lambda ref: jnp.concatenate([ref[j] for j in range(ref.shape[0])], axis=1)
    h = x_ref[...] + _dot(wide(na_ref), wo_na_ref[...]) + _dot(wide(hg_ref), wo_hg_ref[...])
    ms = jnp.mean(h * h, axis=-1, keepdims=True)
    u = (h * lax.rsqrt(ms + RMS_EPS) * nw_ref[...]).astype(BF16)
    mlp = None
    for j in range(D_FF // FF_CHUNK):
        cols = slice(j * FF_CHUNK, (j + 1) * FF_CHUNK)
        a = jnp.maximum(_dot(u, wu_ref[:, cols]), 0.0)
        part = _dot((a * a).astype(BF16), wd_ref[cols, :])
        mlp = part if mlp is None else mlp + part
    o_ref[...] = h + mlp


def _out_mlp(x2, y_na, y_hg, w_out, norm_w, w_up, w_down):
    m = x2.shape[0]
    tile = lambda width: pl.BlockSpec((TOKEN_TILE, width), lambda i: (i, 0))
    blocks = lambda a: pl.BlockSpec((a.shape[0], TOKEN_TILE, V7X_LANES), lambda i: (0, i, 0))
    resident = lambda a: pl.BlockSpec(a.shape, lambda i: (0, 0), pipeline_mode=pl.Buffered(1))
    wo_na, wo_hg = w_out[:NA_WIDTH], w_out[NA_WIDTH:]
    return pl.pallas_call(
        _out_mlp_kernel,
        grid=(m // TOKEN_TILE,),
        in_specs=[tile(D_MODEL), blocks(y_na), blocks(y_hg), resident(wo_na),
                  resident(wo_hg), pl.BlockSpec((1, D_MODEL), lambda i: (0, 0)),
                  resident(w_up), resident(w_down)],
        out_specs=tile(D_MODEL),
        out_shape=jax.ShapeDtypeStruct((m, D_MODEL), F32),
        compiler_params=pltpu.CompilerParams(
            dimension_semantics=("arbitrary",), vmem_limit_bytes=V7X_VMEM_LIMIT_BYTES),
        name="out_mlp",
    )(x2, y_na, y_hg, wo_na, wo_hg, norm_w, w_up, w_down)


def kernel(x, w_in, w_out, attn_norm_w, mlp_norm_w, q_norm_w, k_norm_w, rpb, hg_norm_w,
           lb_fwd, lb_bwd, w_up, w_down):
    batch, seq, d = x.shape
    assert d == D_MODEL and w_in.shape[0] == 1 and seq % GRID_W == 0 and seq % HG_CHUNK == 0
    x2 = x.reshape(batch * seq, d)
    proj = _in_proj(x2, attn_norm_w[0].astype(F32).reshape(1, d), w_in[0].astype(BF16))
    y_na = _natten(proj, q_norm_w[0], k_norm_w[0], rpb[0], batch, seq)
    y_hg = _hgrn2(proj, lb_fwd, lb_bwd, hg_norm_w[0], batch, seq)
    out = _out_mlp(x2, y_na, y_hg, w_out[0].astype(BF16),
                   mlp_norm_w[0].astype(F32).reshape(1, d),
                   w_up[0].astype(BF16), w_down[0].astype(BF16))
    return out.reshape(batch, seq, d)
```

```python
import functools

import numpy as np
import jax
import jax.numpy as jnp
from jax import lax
from jax.experimental import pallas as pl
from jax.experimental.pallas import tpu as pltpu

F32 = jnp.float32
BF16 = jnp.bfloat16

D_MODEL = 1024
GRID_W = 64
NA_HEADS = 8
NA_HEAD_DIM = 64
NA_WIDTH = NA_HEADS * NA_HEAD_DIM
NA_WIN_R = 8
NA_WIN_C = 16
HG_HEADS = 4
HG_DK = 128
HG_WIDTH = HG_HEADS * HG_DK
D_FF = 4 * D_MODEL
D_IN_PROJ = 3 * NA_WIDTH + 5 * HG_WIDTH
RMS_EPS = 1e-6

V7X_LANES = 128
V7X_VMEM_LIMIT_BYTES = 56 * 1024 * 1024

TOKEN_TILE = 512
FF_CHUNK = 1024
NORM_ROWS = 512
HG_CHUNK = 128
HG_LEVELS = 7
HG_CHUNKS_PER_STEP = 8
HG_OUT_CHUNKS_PER_STEP = 8
NA_QT = 16
NA_KT = 32
NA_TILES = GRID_W // NA_QT
NA_GROUP_ROWS = 4
NA_BAND_ROWS = 12
NA_GROUPS_PER_STEP = 2
NA_BATCH_PER_STEP = 2
HG_BATCH_PER_STEP = 2
MASK_NEG = -1e30

_NA_QCOLS = [list(range(0, 8)) + list(range(56, 64)),
             list(range(8, 24)), list(range(24, 40)), list(range(40, 56))]
_NA_KSEGS = [[(0, 16), (48, 16)], [(0, 32)], [(16, 32)], [(32, 32)]]


def _nt_dot(a, b):
    return lax.dot_general(a, b, (((1,), (1,)), ((), ())), preferred_element_type=F32)


def _dot(a, b):
    return jnp.dot(a, b, preferred_element_type=F32)


def _split_bf16(x):
    hi = x.astype(BF16)
    lo = (x - hi.astype(F32)).astype(BF16)
    return hi, lo


def _in_proj_kernel(x_ref, nw_ref, w_ref, o_ref):
    x = x_ref[...]
    ms = jnp.mean(x * x, axis=-1, keepdims=True)
    u = (x * lax.rsqrt(ms + RMS_EPS) * nw_ref[...]).astype(BF16)
    res = _dot(u, w_ref[...])
    for j in range(D_IN_PROJ // V7X_LANES):
        o_ref[j] = res[:, j * V7X_LANES:(j + 1) * V7X_LANES].astype(o_ref.dtype)


def _in_proj(x2, norm_w, w_in):
    m = x2.shape[0]
    return pl.pallas_call(
        _in_proj_kernel,
        grid=(m // TOKEN_TILE,),
        in_specs=[
            pl.BlockSpec((TOKEN_TILE, D_MODEL), lambda i: (i, 0)),
            pl.BlockSpec((1, D_MODEL), lambda i: (0, 0)),
            pl.BlockSpec((D_MODEL, D_IN_PROJ), lambda i: (0, 0), pipeline_mode=pl.Buffered(1)),
        ],
        out_specs=pl.BlockSpec((D_IN_PROJ // V7X_LANES, TOKEN_TILE, V7X_LANES), lambda i: (0, i, 0)),
        out_shape=jax.ShapeDtypeStruct((D_IN_PROJ // V7X_LANES, m, V7X_LANES), BF16),
        compiler_params=pltpu.CompilerParams(
            dimension_semantics=("arbitrary",), vmem_limit_bytes=V7X_VMEM_LIMIT_BYTES),
        name="in_proj",
    )(x2, norm_w, w_in)


def _na_group_plan(rows):
    wr = min(NA_WIN_R, rows)
    bases, types, sigs = [], [], []
    for g in range(rows // NA_GROUP_ROWS):
        r0 = g * NA_GROUP_ROWS
        starts = [min(max(r0 + j - wr // 2, 0), rows - wr) for j in range(NA_GROUP_ROWS)]
        base = min(max(starts[0], 0), rows - NA_BAND_ROWS)
        assert base <= starts[0] and starts[-1] + wr <= base + NA_BAND_ROWS
        sig = tuple((starts[j] - base, r0 + j - base) for j in range(NA_GROUP_ROWS))
        if sig not in sigs:
            sigs.append(sig)
        bases.append(base)
        types.append(sigs.index(sig))
    return bases, types, sigs


def _na_bias_table(rpb, sigs):
    n_ri, n_ci = 2 * NA_WIN_R - 1, 2 * NA_WIN_C - 1
    csel = np.zeros((n_ci, NA_TILES, NA_QT, NA_KT), np.float32)
    for t in range(NA_TILES):
        qc = np.array(_NA_QCOLS[t])
        kc = np.concatenate([np.arange(c0, c0 + n) for c0, n in _NA_KSEGS[t]])
        cs = np.clip(qc - NA_WIN_C // 2, 0, GRID_W - NA_WIN_C)
        valid = (kc[None, :] >= cs[:, None]) & (kc[None, :] < cs[:, None] + NA_WIN_C)
        cidx = kc[None, :] - qc[:, None] + NA_WIN_C - 1
        qi, ki = np.nonzero(valid)
        csel[cidx[qi, ki], t, qi, ki] = 1.0
    cols = jnp.einsum("hrc,ctqk->htqrk", rpb.astype(F32), jnp.asarray(csel),
                      precision=lax.Precision.HIGHEST)
    cols = cols + jnp.asarray(np.where(csel.sum(0) > 0, 0.0, MASK_NEG)[None, :, :, None, :], F32)
    cols = cols.reshape(NA_HEADS // 2, 2, NA_TILES, NA_QT, n_ri * NA_KT)
    cols = jnp.transpose(cols, (0, 2, 1, 3, 4))
    tabs = []
    for sig in sigs:
        per_row = []
        for first, qrow in sig:
            ri0 = first - qrow + NA_WIN_R - 1
            after = NA_BAND_ROWS - first - NA_WIN_R
            window = cols[..., ri0 * NA_KT:(ri0 + NA_WIN_R) * NA_KT]
            per_row.append(jnp.pad(window, [(0, 0)] * 4 + [(first * NA_KT, after * NA_KT)],
                                   constant_values=MASK_NEG))
        tabs.append(jnp.stack(per_row, axis=2))
    return jnp.stack(tabs).reshape(len(sigs), NA_HEADS // 2, NA_TILES,
                                   NA_GROUP_ROWS * 2 * NA_QT, NA_BAND_ROWS * NA_KT)


def _natten_kernel(*refs, n_batch, rows, bases, types):
    def one(bb, carry):
        _natten_one(*refs, bb * rows * GRID_W, rows=rows, bases=bases, types=types)
        return carry

    lax.fori_loop(0, n_batch, one, 0)


def _natten_one(q_ref, k_ref, v_ref, qw_ref, kw_ref, bias_ref, hm_ref, o_ref, qn_s, kn_s, tok0,
                *, rows, bases, types):
    seq = rows * GRID_W
    hm = hm_ref[...]

    def head_rmsnorm(x, w):
        hi, lo = _split_bf16(x * x)
        ms = _dot(jnp.concatenate([hi, lo], axis=1), hm)
        return x * lax.rsqrt(ms + RMS_EPS) * w

    def norm_body(i, carry):
        sl = pl.ds(pl.multiple_of(i * NORM_ROWS, NORM_ROWS), NORM_ROWS)
        src = pl.ds(pl.multiple_of(tok0 + i * NORM_ROWS, NORM_ROWS), NORM_ROWS)
        qn = head_rmsnorm(q_ref[src, :].astype(F32), qw_ref[...])
        qn_s[sl, :] = qn * (NA_HEAD_DIM ** -0.5)
        kn_s[sl, :] = head_rmsnorm(k_ref[src, :].astype(F32), kw_ref[...]).astype(BF16)
        return carry

    lax.fori_loop(0, seq // NORM_ROWS, norm_body, 0)

    lane = lax.broadcasted_iota(jnp.int32, (NA_QT, V7X_LANES), 1)
    first_head = lane < NA_HEAD_DIM
    gr = NA_GROUP_ROWS

    def row_slice(grid_row, tok=0):
        if isinstance(grid_row, int) and isinstance(tok, int):
            return slice(tok + grid_row * GRID_W, tok + (grid_row + 1) * GRID_W)
        return pl.ds(pl.multiple_of(tok + grid_row * GRID_W, GRID_W), GRID_W)

    def row_block(ref, grid_row, tok=0):
        return ref[row_slice(grid_row, tok), :]

    def band(blocks, t):
        return jnp.concatenate(
            [blk[c0:c0 + n] for blk in blocks for (c0, n) in _NA_KSEGS[t]], axis=0)

    def step(groups):
        scores = []
        for row0, base, y in groups:
            kblocks = [row_block(kn_s, base + u) for u in range(NA_BAND_ROWS)]
            qrows = [row_block(qn_s, row0 + j) for j in range(gr)]
            for t in range(NA_TILES):
                qc = _NA_QCOLS[t]
                q2 = []
                for j in range(gr):
                    if t == 0:
                        qt = jnp.concatenate([qrows[j][0:8], qrows[j][56:64]], axis=0)
                    else:
                        qt = qrows[j][qc[0]:qc[0] + NA_QT]
                    q2 += [jnp.where(first_head, qt, 0.0), jnp.where(first_head, 0.0, qt)]
                s = _nt_dot(jnp.concatenate(q2, axis=0).astype(BF16), band(kblocks, t))
                scores.append(s + bias_ref[y, t])
        probs, sums = [], []
        for s in scores:
            p = jnp.exp(s - jnp.max(s, axis=-1, keepdims=True))
            sums.append(jnp.sum(p, axis=-1, keepdims=True))
            probs.append(p.astype(BF16))
        for g, (row0, base, y) in enumerate(groups):
            vblocks = [row_block(v_ref, base + u, tok0) for u in range(NA_BAND_ROWS)]
            o_tiles = []
            for t in range(NA_TILES):
                o2 = _dot(probs[g * NA_TILES + t], band(vblocks, t)) / sums[g * NA_TILES + t]
                o_tiles.append(o2)
            for j in range(gr):
                sel = [jnp.where(first_head, o[2 * NA_QT * j:2 * NA_QT * j + NA_QT],
                                 o[2 * NA_QT * j + NA_QT:2 * NA_QT * (j + 1)]) for o in o_tiles]
                orow = jnp.concatenate([sel[0][0:8], sel[1], sel[2], sel[3], sel[0][8:16]], axis=0)
                o_ref[row_slice(row0 + j, tok0), :] = orow.astype(o_ref.dtype)

    n_groups = rows // gr
    shift = min(NA_WIN_R, rows) // 2
    regular = [bases[g] == g * gr - shift for g in range(n_groups)]
    lo = regular.index(True)
    hi = n_groups - regular[::-1].index(True)
    per = NA_GROUPS_PER_STEP
    assert all(regular[lo:hi]) and (hi - lo) % per == 0
    assert len({types[g] for g in range(lo, hi)}) == 1
    edge = [(g * gr, bases[g], types[g]) for g in list(range(lo)) + list(range(hi, n_groups))]
    for i in range(0, len(edge), per):
        step(edge[i:i + per])

    def regular_body(i, carry):
        g0 = lo + i * per
        step([((g0 + n) * gr, (g0 + n) * gr - shift, types[lo]) for n in range(per)])
        return carry

    lax.fori_loop(0, (hi - lo) // per, regular_body, 0)


def _natten(proj, q_norm_w, k_norm_w, rpb, batch, seq):
    rows = seq // GRID_W
    assert rows >= NA_BAND_ROWS and rows % NA_GROUP_ROWS == 0
    assert GRID_W == 64 and NA_HEAD_DIM * 2 == V7X_LANES
    pairs = NA_HEADS // 2
    bases, types, sigs = _na_group_plan(rows)
    bias = _na_bias_table(rpb, sigs)
    qw = jnp.tile(q_norm_w.astype(F32).reshape(1, NA_HEAD_DIM), (1, 2))
    kw = jnp.tile(k_norm_w.astype(F32).reshape(1, NA_HEAD_DIM), (1, 2))
    head_id = np.arange(V7X_LANES) // NA_HEAD_DIM
    hm = np.tile((head_id[:, None] == head_id[None, :]) / NA_HEAD_DIM, (2, 1))
    hm = jnp.asarray(hm, BF16)
    nb = NA_BATCH_PER_STEP
    assert batch % nb == 0
    col = lambda off: pl.BlockSpec((None, nb * seq, V7X_LANES),
                                   lambda p, b, off=off: (off + p, b, 0))
    const2 = lambda shape: pl.BlockSpec(shape, lambda p, b: (0, 0))
    return pl.pallas_call(
        functools.partial(_natten_kernel, n_batch=nb, rows=rows, bases=tuple(bases),
                          types=tuple(types)),
        grid=(pairs, batch // nb),
        in_specs=[
            col(0), col(pairs), col(2 * pairs),
            const2((1, V7X_LANES)), const2((1, V7X_LANES)),
            pl.BlockSpec((len(sigs), None, NA_TILES, NA_GROUP_ROWS * 2 * NA_QT,
                          NA_BAND_ROWS * NA_KT), lambda p, b: (0, p, 0, 0, 0)),
            const2((2 * V7X_LANES, V7X_LANES)),
        ],
        out_specs=pl.BlockSpec((None, nb * seq, V7X_LANES), lambda p, b: (p, b, 0)),
        out_shape=jax.ShapeDtypeStruct((pairs, batch * seq, V7X_LANES), BF16),
        scratch_shapes=[pltpu.VMEM((seq, V7X_LANES), F32), pltpu.VMEM((seq, V7X_LANES), BF16)],
        compiler_params=pltpu.CompilerParams(
            dimension_semantics=("arbitrary", "arbitrary"),
            vmem_limit_bytes=V7X_VMEM_LIMIT_BYTES),
        name="natten",
    )(proj, proj, proj, qw, kw, bias, hm)


def _hg_tables():
    c = HG_CHUNK
    idx = np.arange(c)
    lower = (idx[None, :] <= idx[:, None]).astype(np.float32)
    cums = np.stack([np.concatenate([lower, lower], axis=1),
                     np.concatenate([lower.T, lower.T], axis=1)])
    masks = []
    for lvl in range(HG_LEVELS):
        s = 1 << lvl
        same_block = (idx[:, None] // (2 * s)) == (idx[None, :] // (2 * s))
        half = (idx // s) % 2
        masks.append(same_block & (half[:, None] != half[None, :]))
    masks.append(np.eye(c, dtype=bool))
    return jnp.asarray(cums, BF16), jnp.asarray(np.stack(masks), F32)


def _hgrn_kernel(*refs, n_batch, seq):
    def one(bb, carry):
        _hgrn_one(*refs, bb * seq, seq=seq)
        return carry

    lax.fori_loop(0, n_batch, one, 0)


def _hgrn_one(q_ref, ff_ref, fb_ref, v_ref, g_ref, lbf_ref, lbb_ref, gw_ref,
              cum_ref, mask_ref, o_ref, acc_s, qb_s, inc_s, etb_s, tok0, *, seq):
    c = HG_CHUNK
    n_chunks = seq // c
    n_step = HG_CHUNKS_PER_STEP
    assert n_chunks % n_step == 0
    sub = lax.broadcasted_iota(jnp.int32, (c, V7X_LANES), 0) & 7

    def lower_bound(lb_ref):
        a = lb_ref[...].astype(F32)
        a0, a1 = a[0:1], a[1:2]
        m = jnp.maximum(a0, a1)
        e0, e1 = jnp.exp(a0 - m), jnp.exp(a1 - m)
        return e0 / (e0 + e1)

    lbf = lower_bound(lbf_ref)
    lbb = lower_bound(lbb_ref)

    def rows(ci, tok=0):
        return pl.ds(pl.multiple_of(tok + ci * c, c), c)

    def group_rows(x, r):
        return jnp.concatenate(
            [jnp.broadcast_to(x[8 * j + r:8 * j + r + 1], (8, V7X_LANES)) for j in range(c // 8)],
            axis=0)

    def repeat8(x8, n_rows):
        return x8 if n_rows == 8 else jnp.concatenate([x8] * (n_rows // 8), axis=0)

    def gates(f_ref, sl, lb):
        hf = f_ref[sl, :].astype(F32)
        f = lb + (1.0 - lb) * (1.0 / (1.0 + jnp.exp(-hf)))
        hi, lo = _split_bf16(jnp.log2(f))
        return f, 1.0 - f, jnp.concatenate([hi, lo], axis=0)

    def stage_gates(ci):
        src = rows(ci, tok0)
        st = dict(sl=rows(ci), q=q_ref[src, :].astype(F32), v=v_ref[src, :])
        st["f_f"], st["k_f"], lf_f = gates(ff_ref, src, lbf)
        st["f_b"], st["k_b"], lf_b = gates(fb_ref, src, lbb)
        st["b_f"] = _dot(cum_ref[0], lf_f)
        st["b_b"] = _dot(cum_ref[1], lf_b)
        return st

    def level_operands(st, lvl):
        q, k_f, k_b, b_f, b_b = st["q"], st["k_f"], st["k_b"], st["b_f"], st["b_b"]
        if lvl == HG_LEVELS:
            return q, k_f + k_b
        if lvl == 0:
            odd = (sub & 1) == 1
            return q * jnp.where(odd, st["f_f"], st["f_b"]), jnp.where(odd, k_b, k_f)
        if lvl < 3:
            later = (sub & (1 << lvl)) != 0
            if lvl == 1:
                ref_f = jnp.where(sub < 4, group_rows(b_f, 1), group_rows(b_f, 5))
                ref_b = jnp.where(sub < 4, group_rows(b_b, 2), group_rows(b_b, 6))
            else:
                ref_f, ref_b = group_rows(b_f, 3), group_rows(b_b, 4)
            d_f, d_b = b_f - ref_f, b_b - ref_b
            return (q * jnp.exp2(jnp.where(later, d_f, d_b)),
                    jnp.where(later, k_b, k_f) * jnp.exp2(-jnp.where(later, d_b, d_f)))
        s = 1 << lvl
        xp, yp = [], []
        for r0 in range(0, c, 2 * s):
            m = r0 + s
            rf = repeat8(st["last_f"][m - 8:m], s)
            rb = repeat8(st["first_b"][m:m + 8], s)
            early, late = slice(r0, m), slice(m, m + s)
            xp += [q[early] * jnp.exp2(b_b[early] - rb), q[late] * jnp.exp2(b_f[late] - rf)]
            yp += [k_f[early] * jnp.exp2(rf - b_f[early]), k_b[late] * jnp.exp2(rb - b_b[late])]
        return jnp.concatenate(xp, axis=0), jnp.concatenate(yp, axis=0)

    def stage_operands(ci, st):
        q, k_f, k_b, b_f, b_b = st["q"], st["k_f"], st["k_b"], st["b_f"], st["b_b"]
        last_f = st["last_f"] = group_rows(b_f, 7)
        first_b = st["first_b"] = group_rows(b_b, 0)
        sl = st["sl"]
        tot_f = last_f[c - 8:c]
        tot_b = first_b[0:8]
        st["qe_f"] = (q * jnp.exp2(b_f)).astype(BF16)
        st["kd_f"] = (k_f * jnp.exp2(repeat8(tot_f, c) - b_f)).astype(BF16)
        st["tot_f"] = jnp.exp2(tot_f[0:1])
        qb_s[sl, :] = (q * jnp.exp2(b_b)).astype(BF16)
        st["kd_b"] = (k_b * jnp.exp2(repeat8(tot_b, c) - b_b)).astype(BF16)
        etb_s[pl.ds(pl.multiple_of(ci * 8, 8), 8), :] = jnp.exp2(tot_b)
        return st

    def fwd_body(i, state_t):
        cis = [i * n_step + j for j in range(n_step)]
        sts = [stage_gates(ci) for ci in cis]
        sts = [stage_operands(ci, st) for ci, st in zip(cis, sts)]
        for st in sts:
            for lvl in range(HG_LEVELS + 1):
                x, y = level_operands(st, lvl)
                term = mask_ref[lvl] * _nt_dot(x.astype(BF16), y.astype(BF16))
                st["a"] = term if lvl == 0 else st["a"] + term
        intra = [_dot(st["a"].astype(BF16), st["v"]) for st in sts]
        incs = [_dot(st["v"].T, jnp.concatenate([st["kd_f"], st["kd_b"]], axis=1)) for st in sts]
        for st, o, inc in zip(sts, intra, incs):
            acc_s[st["sl"], :] = o + _nt_dot(st["qe_f"], state_t.astype(BF16))
            state_t = state_t * st["tot_f"] + inc[:, 0:HG_DK]
            inc_s[st["sl"], :] = inc[:, HG_DK:2 * HG_DK]
        return state_t

    zero = jnp.zeros((HG_DK, HG_DK), F32)
    lax.fori_loop(0, n_chunks // n_step, fwd_body, zero)

    n_out = HG_OUT_CHUNKS_PER_STEP
    assert n_chunks % n_out == 0

    def bwd_body(i, state_t):
        cis = [n_chunks - 1 - (i * n_out + j) for j in range(n_out)]
        inter = []
        for ci in cis:
            inter.append(_nt_dot(qb_s[rows(ci), :], state_t.astype(BF16)))
            tot_b = etb_s[pl.ds(pl.multiple_of(ci * 8, 8), 1), :]
            state_t = state_t * tot_b + inc_s[rows(ci), :]
        for ci, o_inter in zip(cis, inter):
            sl = rows(ci)
            o = acc_s[sl, :] + o_inter
            ms = jnp.mean(o * o, axis=-1, keepdims=True)
            y = o * lax.rsqrt(ms + RMS_EPS) * gw_ref[...]
            g = g_ref[rows(ci, tok0), :].astype(F32)
            o_ref[rows(ci, tok0), :] = (y * (g * (1.0 / (1.0 + jnp.exp(-g))))).astype(o_ref.dtype)
        return state_t

    lax.fori_loop(0, n_chunks // n_out, bwd_body, zero)


def _hgrn2(proj, lb_fwd, lb_bwd, hg_norm_w, batch, seq):
    assert HG_DK == V7X_LANES and lb_fwd.shape[0] == 2
    cums, masks = _hg_tables()
    first = 3 * NA_WIDTH // V7X_LANES
    nb = HG_BATCH_PER_STEP
    assert batch % nb == 0
    col = lambda k: pl.BlockSpec((None, nb * seq, V7X_LANES),
                                 lambda b, h, k=k: (first + k * HG_HEADS + h, b, 0))
    lbs = pl.BlockSpec((2, V7X_LANES), lambda b, h: (0, h))
    const = lambda a: pl.BlockSpec(a.shape, lambda b, h, nd=a.ndim: (0,) * nd)
    gw = hg_norm_w.astype(F32).reshape(1, HG_DK)
    return pl.pallas_call(
        functools.partial(_hgrn_kernel, n_batch=nb, seq=seq),
        grid=(batch // nb, HG_HEADS),
        in_specs=[col(0), col(1), col(2), col(3), col(4), lbs, lbs, const(gw),
                  const(cums), const(masks)],
        out_specs=pl.BlockSpec((None, nb * seq, V7X_LANES), lambda b, h: (h, b, 0)),
        out_shape=jax.ShapeDtypeStruct((HG_HEADS, batch * seq, V7X_LANES), BF16),
        scratch_shapes=[pltpu.VMEM((seq, HG_DK), F32), pltpu.VMEM((seq, HG_DK), BF16),
                        pltpu.VMEM((seq, HG_DK), F32),
                        pltpu.VMEM((8 * seq // HG_CHUNK, HG_DK), F32)],
        compiler_params=pltpu.CompilerParams(
            dimension_semantics=("arbitrary", "arbitrary"),
            vmem_limit_bytes=V7X_VMEM_LIMIT_BYTES),
        name="hgrn2",
    )(proj, proj, proj, proj, proj, lb_fwd.astype(F32), lb_bwd.astype(F32), gw, cums, masks)


def _out_mlp_kernel(x_ref, na_ref, hg_ref, wo_na_ref, wo_hg_ref, nw_ref, wu_ref, wd_ref, o_ref):
    wide = lambda ref: jnp.concatenate([ref[j] for j in range(ref.shape[0])], axis=1)
    h = x_ref[...] + _dot(wide(na_ref), wo_na_ref[...]) + _dot(wide(hg_ref), wo_hg_ref[...])
    ms = jnp.mean(h * h, axis=-1, keepdims=True)
    u = (h * lax.rsqrt(ms + RMS_EPS) * nw_ref[...]).astype(BF16)
    mlp = None
    for j in range(D_FF // FF_CHUNK):
        cols = slice(j * FF_CHUNK, (j + 1) * FF_CHUNK)
        a = jnp.maximum(_dot(u, wu_ref[:, cols]), 0.0)
        part = _dot((a * a).astype(BF16), wd_ref[cols, :])
        mlp = part if mlp is None else mlp + part
    o_ref[...] = h + mlp


def _out_mlp(x2, y_na, y_hg, w_out, norm_w, w_up, w_down):
    m = x2.shape[0]
    tile = lambda width: pl.BlockSpec((TOKEN_TILE, width), lambda i: (i, 0))
    blocks = lambda a: pl.BlockSpec((a.shape[0], TOKEN_TILE, V7X_LANES), lambda i: (0, i, 0))
    resident = lambda a: pl.BlockSpec(a.shape, lambda i: (0, 0), pipeline_mode=pl.Buffered(1))
    wo_na, wo_hg = w_out[:NA_WIDTH], w_out[NA_WIDTH:]
    return pl.pallas_call(
        _out_mlp_kernel,
        grid=(m // TOKEN_TILE,),
        in_specs=[tile(D_MODEL), blocks(y_na), blocks(y_hg), resident(wo_na),
                  resident(wo_hg), pl.BlockSpec((1, D_MODEL), lambda i: (0, 0)),
                  resident(w_up), resident(w_down)],
        out_specs=tile(D_MODEL),
        out_shape=jax.ShapeDtypeStruct((m, D_MODEL), F32),
        compiler_params=pltpu.CompilerParams(
            dimension_semantics=("arbitrary",), vmem_limit_bytes=V7X_VMEM_LIMIT_BYTES),
        name="out_mlp",
    )(x2, y_na, y_hg, wo_na, wo_hg, norm_w, w_up, w_down)


def kernel(x, w_in, w_out, attn_norm_w, mlp_norm_w, q_norm_w, k_norm_w, rpb, hg_norm_w,
           lb_fwd, lb_bwd, w_up, w_down):
    batch, seq, d = x.shape
    assert d == D_MODEL and w_in.shape[0] == 1 and seq % GRID_W == 0 and seq % HG_CHUNK == 0
    x2 = x.reshape(batch * seq, d)
    proj = _in_proj(x2, attn_norm_w[0].astype(F32).reshape(1, d), w_in[0].astype(BF16))
    y_na = _natten(proj, q_norm_w[0], k_norm_w[0], rpb[0], batch, seq)
    y_hg = _hgrn2(proj, lb_fwd, lb_bwd, hg_norm_w[0], batch, seq)
    out = _out_mlp(x2, y_na, y_hg, w_out[0].astype(BF16),
                   mlp_norm_w[0].astype(F32).reshape(1, d),
                   w_up[0].astype(BF16), w_down[0].astype(BF16))
    return out.reshape(batch, seq, d)
```

```python
import functools

import numpy as np
import jax
import jax.numpy as jnp
from jax import lax
from jax.experimental import pallas as pl
from jax.experimental.pallas import tpu as pltpu

F32 = jnp.float32
BF16 = jnp.bfloat16

D_MODEL = 1024
GRID_W = 64
NA_HEADS = 8
NA_HEAD_DIM = 64
NA_WIDTH = NA_HEADS * NA_HEAD_DIM
NA_WIN_R = 8
NA_WIN_C = 16
HG_HEADS = 4
HG_DK = 128
HG_WIDTH = HG_HEADS * HG_DK
D_FF = 4 * D_MODEL
D_IN_PROJ = 3 * NA_WIDTH + 5 * HG_WIDTH
RMS_EPS = 1e-6

V7X_LANES = 128
V7X_VMEM_LIMIT_BYTES = 56 * 1024 * 1024

TOKEN_TILE = 512
FF_CHUNK = 1024
HG_CHUNK = 128
HG_LEVELS = 7
HG_CHUNKS_PER_STEP = 8
HG_OUT_CHUNKS_PER_STEP = 8
NA_QT = 16
NA_KT = 32
NA_TILES = GRID_W // NA_QT
NA_GROUP_ROWS = 4
NA_BAND_ROWS = 12
NA_GROUPS_PER_STEP = 2
NA_BATCH_PER_STEP = 2
HG_BATCH_PER_STEP = 2
MASK_NEG = -1e30

_NA_QCOLS = [list(range(0, 8)) + list(range(56, 64)),
             list(range(8, 24)), list(range(24, 40)), list(range(40, 56))]
_NA_KSEGS = [[(0, 16), (48, 16)], [(0, 32)], [(16, 32)], [(32, 32)]]


def _nt_dot(a, b):
    return lax.dot_general(a, b, (((1,), (1,)), ((), ())), preferred_element_type=F32)


def _dot(a, b):
    return jnp.dot(a, b, preferred_element_type=F32)


def _split_bf16(x):
    hi = x.astype(BF16)
    lo = (x - hi.astype(F32)).astype(BF16)
    return hi, lo


def _in_proj_kernel(x_ref, nw_ref, w_ref, qkw_ref, o_ref):
    x = x_ref[...]
    ms = jnp.mean(x * x, axis=-1, keepdims=True)
    u = (x * lax.rsqrt(ms + RMS_EPS) * nw_ref[...]).astype(BF16)
    res = _dot(u, w_ref[...])
    first_head = lax.broadcasted_iota(jnp.int32, (1, V7X_LANES), 1) < NA_HEAD_DIM
    for j in range(D_IN_PROJ // V7X_LANES):
        blk = res[:, j * V7X_LANES:(j + 1) * V7X_LANES]
        if j < qkw_ref.shape[0]:
            sq = blk * blk
            s_all = jnp.sum(sq, axis=-1, keepdims=True)
            s_one = jnp.sum(jnp.where(first_head, sq, 0.0), axis=-1, keepdims=True)
            ms_h = jnp.where(first_head, s_one, s_all - s_one) * (1.0 / NA_HEAD_DIM)
            blk = blk * lax.rsqrt(ms_h + RMS_EPS) * qkw_ref[j]
        o_ref[j] = blk.astype(o_ref.dtype)


def _in_proj(x2, norm_w, w_in, q_norm_w, k_norm_w):
    m = x2.shape[0]
    pair_w = lambda w, mult: jnp.tile(w.astype(F32) * mult, 2).reshape(1, 1, V7X_LANES)
    n_blk = NA_WIDTH // V7X_LANES
    qkw = jnp.concatenate([jnp.tile(pair_w(q_norm_w, NA_HEAD_DIM ** -0.5), (n_blk, 1, 1)),
                           jnp.tile(pair_w(k_norm_w, 1.0), (n_blk, 1, 1))])
    return pl.pallas_call(
        _in_proj_kernel,
        grid=(m // TOKEN_TILE,),
        in_specs=[
            pl.BlockSpec((TOKEN_TILE, D_MODEL), lambda i: (i, 0)),
            pl.BlockSpec((1, D_MODEL), lambda i: (0, 0)),
            pl.BlockSpec((D_MODEL, D_IN_PROJ), lambda i: (0, 0), pipeline_mode=pl.Buffered(1)),
            pl.BlockSpec(qkw.shape, lambda i: (0, 0, 0)),
        ],
        out_specs=pl.BlockSpec((D_IN_PROJ // V7X_LANES, TOKEN_TILE, V7X_LANES), lambda i: (0, i, 0)),
        out_shape=jax.ShapeDtypeStruct((D_IN_PROJ // V7X_LANES, m, V7X_LANES), BF16),
        compiler_params=pltpu.CompilerParams(
            dimension_semantics=("arbitrary",), vmem_limit_bytes=V7X_VMEM_LIMIT_BYTES),
        name="in_proj",
    )(x2, norm_w, w_in, qkw)


def _na_group_plan(rows):
    wr = min(NA_WIN_R, rows)
    bases, types, sigs = [], [], []
    for g in range(rows // NA_GROUP_ROWS):
        r0 = g * NA_GROUP_ROWS
        starts = [min(max(r0 + j - wr // 2, 0), rows - wr) for j in range(NA_GROUP_ROWS)]
        base = min(max(starts[0], 0), rows - NA_BAND_ROWS)
        assert base <= starts[0] and starts[-1] + wr <= base + NA_BAND_ROWS
        sig = tuple((starts[j] - base, r0 + j - base) for j in range(NA_GROUP_ROWS))
        if sig not in sigs:
            sigs.append(sig)
        bases.append(base)
        types.append(sigs.index(sig))
    return bases, types, sigs


def _na_bias_table(rpb, sigs):
    n_ri, n_ci = 2 * NA_WIN_R - 1, 2 * NA_WIN_C - 1
    csel = np.zeros((n_ci, NA_TILES, NA_QT, NA_KT), np.float32)
    for t in range(NA_TILES):
        qc = np.array(_NA_QCOLS[t])
        kc = np.concatenate([np.arange(c0, c0 + n) for c0, n in _NA_KSEGS[t]])
        cs = np.clip(qc - NA_WIN_C // 2, 0, GRID_W - NA_WIN_C)
        valid = (kc[None, :] >= cs[:, None]) & (kc[None, :] < cs[:, None] + NA_WIN_C)
        cidx = kc[None, :] - qc[:, None] + NA_WIN_C - 1
        qi, ki = np.nonzero(valid)
        csel[cidx[qi, ki], t, qi, ki] = 1.0
    cols = jnp.einsum("hrc,ctqk->htqrk", rpb.astype(F32), jnp.asarray(csel),
                      precision=lax.Precision.HIGHEST)
    cols = cols + jnp.asarray(np.where(csel.sum(0) > 0, 0.0, MASK_NEG)[None, :, :, None, :], F32)
    cols = cols.reshape(NA_HEADS // 2, 2, NA_TILES, NA_QT, n_ri * NA_KT)
    cols = jnp.transpose(cols, (0, 2, 1, 3, 4))
    tabs = []
    for sig in sigs:
        per_row = []
        for first, qrow in sig:
            ri0 = first - qrow + NA_WIN_R - 1
            after = NA_BAND_ROWS - first - NA_WIN_R
            window = cols[..., ri0 * NA_KT:(ri0 + NA_WIN_R) * NA_KT]
            per_row.append(jnp.pad(window, [(0, 0)] * 4 + [(first * NA_KT, after * NA_KT)],
                                   constant_values=MASK_NEG))
        tabs.append(jnp.stack(per_row, axis=2))
    return jnp.stack(tabs).reshape(len(sigs), NA_HEADS // 2, NA_TILES,
                                   NA_GROUP_ROWS * 2 * NA_QT, NA_BAND_ROWS * NA_KT)


def _natten_kernel(*refs, n_batch, rows, bases, types):
    def one(bb, carry):
        _natten_one(*refs, bb * rows * GRID_W, rows=rows, bases=bases, types=types)
        return carry

    lax.fori_loop(0, n_batch, one, 0)


def _natten_one(q_ref, k_ref, v_ref, bias_ref, o_ref, tok0, *, rows, bases, types):
    lane = lax.broadcasted_iota(jnp.int32, (NA_QT, V7X_LANES), 1)
    first_head = lane < NA_HEAD_DIM
    gr = NA_GROUP_ROWS

    def row_slice(grid_row):
        return pl.ds(pl.multiple_of(tok0 + grid_row * GRID_W, GRID_W), GRID_W)

    def row_block(ref, grid_row):
        return ref[row_slice(grid_row), :]

    def band(blocks, t):
        return jnp.concatenate(
            [blk[c0:c0 + n] for blk in blocks for (c0, n) in _NA_KSEGS[t]], axis=0)

    def step(groups):
        scores = []
        for row0, base, y in groups:
            kblocks = [row_block(k_ref, base + u) for u in range(NA_BAND_ROWS)]
            qrows = [row_block(q_ref, row0 + j).astype(F32) for j in range(gr)]
            for t in range(NA_TILES):
                qc = _NA_QCOLS[t]
                q2 = []
                for j in range(gr):
                    if t == 0:
                        qt = jnp.concatenate([qrows[j][0:8], qrows[j][56:64]], axis=0)
                    else:
                        qt = qrows[j][qc[0]:qc[0] + NA_QT]
                    q2 += [jnp.where(first_head, qt, 0.0), jnp.where(first_head, 0.0, qt)]
                s = _nt_dot(jnp.concatenate(q2, axis=0).astype(BF16), band(kblocks, t))
                scores.append(s + bias_ref[y, t])
        probs, sums = [], []
        for s in scores:
            p = jnp.exp(s - jnp.max(s, axis=-1, keepdims=True))
            sums.append(jnp.sum(p, axis=-1, keepdims=True))
            probs.append(p.astype(BF16))
        for g, (row0, base, y) in enumerate(groups):
            vblocks = [row_block(v_ref, base + u) for u in range(NA_BAND_ROWS)]
            o_tiles = []
            for t in range(NA_TILES):
                o2 = _dot(probs[g * NA_TILES + t], band(vblocks, t)) / sums[g * NA_TILES + t]
                o_tiles.append(o2)
            for j in range(gr):
                sel = [jnp.where(first_head, o[2 * NA_QT * j:2 * NA_QT * j + NA_QT],
                                 o[2 * NA_QT * j + NA_QT:2 * NA_QT * (j + 1)]) for o in o_tiles]
                orow = jnp.concatenate([sel[0][0:8], sel[1], sel[2], sel[3], sel[0][8:16]], axis=0)
                o_ref[row_slice(row0 + j), :] = orow.astype(o_ref.dtype)

    n_groups = rows // gr
    shift = min(NA_WIN_R, rows) // 2
    regular = [bases[g] == g * gr - shift for g in range(n_groups)]
    lo = regular.index(True)
    hi = n_groups - regular[::-1].index(True)
    per = NA_GROUPS_PER_STEP
    assert all(regular[lo:hi]) and (hi - lo) % per == 0
    assert len({types[g] for g in range(lo, hi)}) == 1
    edge = [(g * gr, bases[g], types[g]) for g in list(range(lo)) + list(range(hi, n_groups))]
    for i in range(0, len(edge), per):
        step(edge[i:i + per])

    def regular_body(i, carry):
        g0 = lo + i * per
        step([((g0 + n) * gr, (g0 + n) * gr - shift, types[lo]) for n in range(per)])
        return carry

    lax.fori_loop(0, (hi - lo) // per, regular_body, 0)


def _natten(proj, rpb, batch, seq):
    rows = seq // GRID_W
    assert rows >= NA_BAND_ROWS and rows % NA_GROUP_ROWS == 0
    assert GRID_W == 64 and NA_HEAD_DIM * 2 == V7X_LANES
    pairs = NA_HEADS // 2
    bases, types, sigs = _na_group_plan(rows)
    bias = _na_bias_table(rpb, sigs)
    nb = NA_BATCH_PER_STEP
    assert batch % nb == 0
    col = lambda off: pl.BlockSpec((None, nb * seq, V7X_LANES),
                                   lambda p, b, off=off: (off + p, b, 0))
    return pl.pallas_call(
        functools.partial(_natten_kernel, n_batch=nb, rows=rows, bases=tuple(bases),
                          types=tuple(types)),
        grid=(pairs, batch // nb),
        in_specs=[
            col(0), col(pairs), col(2 * pairs),
            pl.BlockSpec((len(sigs), None, NA_TILES, NA_GROUP_ROWS * 2 * NA_QT,
                          NA_BAND_ROWS * NA_KT), lambda p, b: (0, p, 0, 0, 0)),
        ],
        out_specs=pl.BlockSpec((None, nb * seq, V7X_LANES), lambda p, b: (p, b, 0)),
        out_shape=jax.ShapeDtypeStruct((pairs, batch * seq, V7X_LANES), BF16),
        compiler_params=pltpu.CompilerParams(
            dimension_semantics=("arbitrary", "arbitrary"),
            vmem_limit_bytes=V7X_VMEM_LIMIT_BYTES),
        name="natten",
    )(proj, proj, proj, bias)


def _hg_tables():
    c = HG_CHUNK
    idx = np.arange(c)
    lower = (idx[None, :] <= idx[:, None]).astype(np.float32)
    cums = np.stack([np.concatenate([lower, lower], axis=1),
                     np.concatenate([lower.T, lower.T], axis=1)])
    masks = []
    for lvl in range(HG_LEVELS):
        s = 1 << lvl
        same_block = (idx[:, None] // (2 * s)) == (idx[None, :] // (2 * s))
        half = (idx // s) % 2
        masks.append(same_block & (half[:, None] != half[None, :]))
    masks.append(np.eye(c, dtype=bool))
    return jnp.asarray(cums, BF16), jnp.asarray(np.stack(masks), F32)


def _hgrn_kernel(*refs, n_batch, seq):
    def one(bb, carry):
        _hgrn_one(*refs, bb * seq, seq=seq)
        return carry

    lax.fori_loop(0, n_batch, one, 0)


def _hgrn_one(q_ref, ff_ref, fb_ref, v_ref, g_ref, lbf_ref, lbb_ref, gw_ref,
              cum_ref, mask_ref, o_ref, acc_s, qb_s, inc_s, etb_s, tok0, *, seq):
    c = HG_CHUNK
    n_chunks = seq // c
    n_step = HG_CHUNKS_PER_STEP
    assert n_chunks % n_step == 0
    sub = lax.broadcasted_iota(jnp.int32, (c, V7X_LANES), 0) & 7

    def lower_bound(lb_ref):
        a = lb_ref[...].astype(F32)
        a0, a1 = a[0:1], a[1:2]
        m = jnp.maximum(a0, a1)
        e0, e1 = jnp.exp(a0 - m), jnp.exp(a1 - m)
        return e0 / (e0 + e1)

    lbf = lower_bound(lbf_ref)
    lbb = lower_bound(lbb_ref)

    def rows(ci, tok=0):
        return pl.ds(pl.multiple_of(tok + ci * c, c), c)

    def group_rows(x, r):
        return jnp.concatenate(
            [jnp.broadcast_to(x[8 * j + r:8 * j + r + 1], (8, V7X_LANES)) for j in range(c // 8)],
            axis=0)

    def repeat8(x8, n_rows):
        return x8 if n_rows == 8 else jnp.concatenate([x8] * (n_rows // 8), axis=0)

    def gates(f_ref, sl, lb):
        hf = f_ref[sl, :].astype(F32)
        f = lb + (1.0 - lb) * (1.0 / (1.0 + jnp.exp(-hf)))
        hi, lo = _split_bf16(jnp.log2(f))
        return f, 1.0 - f, jnp.concatenate([hi, lo], axis=0)

    def stage_gates(ci):
        src = rows(ci, tok0)
        st = dict(sl=rows(ci), q=q_ref[src, :].astype(F32), v=v_ref[src, :])
        st["f_f"], st["k_f"], lf_f = gates(ff_ref, src, lbf)
        st["f_b"], st["k_b"], lf_b = gates(fb_ref, src, lbb)
        st["b_f"] = _dot(cum_ref[0], lf_f)
        st["b_b"] = _dot(cum_ref[1], lf_b)
        return st

    def level_operands(st, lvl):
        q, k_f, k_b, b_f, b_b = st["q"], st["k_f"], st["k_b"], st["b_f"], st["b_b"]
        if lvl == HG_LEVELS:
            return q, k_f + k_b
        if lvl == 0:
            odd = (sub & 1) == 1
            return q * jnp.where(odd, st["f_f"], st["f_b"]), jnp.where(odd, k_b, k_f)
        if lvl < 3:
            later = (sub & (1 << lvl)) != 0
            if lvl == 1:
                ref_f = jnp.where(sub < 4, group_rows(b_f, 1), group_rows(b_f, 5))
                ref_b = jnp.where(sub < 4, group_rows(b_b, 2), group_rows(b_b, 6))
            else:
                ref_f, ref_b = group_rows(b_f, 3), group_rows(b_b, 4)
            d_f, d_b = b_f - ref_f, b_b - ref_b
            return (q * jnp.exp2(jnp.where(later, d_f, d_b)),
                    jnp.where(later, k_b, k_f) * jnp.exp2(-jnp.where(later, d_b, d_f)))
        s = 1 << lvl
        xp, yp = [], []
        for r0 in range(0, c, 2 * s):
            m = r0 + s
            rf = repeat8(st["last_f"][m - 8:m], s)
            rb = repeat8(st["first_b"][m:m + 8], s)
            early, late = slice(r0, m), slice(m, m + s)
            xp += [q[early] * jnp.exp2(b_b[early] - rb), q[late] * jnp.exp2(b_f[late] - rf)]
            yp += [k_f[early] * jnp.exp2(rf - b_f[early]), k_b[late] * jnp.exp2(rb - b_b[late])]
        return jnp.concatenate(xp, axis=0), jnp.concatenate(yp, axis=0)

    def stage_operands(ci, st):
        q, k_f, k_b, b_f, b_b = st["q"], st["k_f"], st["k_b"], st["b_f"], st["b_b"]
        last_f = st["last_f"] = group_rows(b_f, 7)
        first_b = st["first_b"] = group_rows(b_b, 0)
        sl = st["sl"]
        tot_f = last_f[c - 8:c]
        tot_b = first_b[0:8]
        st["qe_f"] = (q * jnp.exp2(b_f)).astype(BF16)
        st["kd_f"] = (k_f * jnp.exp2(repeat8(tot_f, c) - b_f)).astype(BF16)
        st["tot_f"] = jnp.exp2(tot_f[0:1])
        qb_s[sl, :] = (q * jnp.exp2(b_b)).astype(BF16)
        st["kd_b"] = (k_b * jnp.exp2(repeat8(tot_b, c) - b_b)).astype(BF16)
        etb_s[pl.ds(pl.multiple_of(ci * 8, 8), 8), :] = jnp.exp2(tot_b)
        return st

    def fwd_body(i, state_t):
        cis = [i * n_step + j for j in range(n_step)]
        sts = [stage_gates(ci) for ci in cis]
        sts = [stage_operands(ci, st) for ci, st in zip(cis, sts)]
        for st in sts:
            for lvl in range(HG_LEVELS + 1):
                x, y = level_operands(st, lvl)
                term = mask_ref[lvl] * _nt_dot(x.astype(BF16), y.astype(BF16))
                st["a"] = term if lvl == 0 else st["a"] + term
        intra = [_dot(st["a"].astype(BF16), st["v"]) for st in sts]
        incs = [_dot(st["v"].T, jnp.concatenate([st["kd_f"], st["kd_b"]], axis=1)) for st in sts]
        for st, o, inc in zip(sts, intra, incs):
            acc_s[st["sl"], :] = o + _nt_dot(st["qe_f"], state_t.astype(BF16))
            state_t = state_t * st["tot_f"] + inc[:, 0:HG_DK]
            inc_s[st["sl"], :] = inc[:, HG_DK:2 * HG_DK]
        return state_t

    zero = jnp.zeros((HG_DK, HG_DK), F32)
    lax.fori_loop(0, n_chunks // n_step, fwd_body, zero)

    n_out = HG_OUT_CHUNKS_PER_STEP
    assert n_chunks % n_out == 0

    def bwd_body(i, state_t):
        cis = [n_chunks - 1 - (i * n_out + j) for j in range(n_out)]
        inter = []
        for ci in cis:
            inter.append(_nt_dot(qb_s[rows(ci), :], state_t.astype(BF16)))
            tot_b = etb_s[pl.ds(pl.multiple_of(ci * 8, 8), 1), :]
            state_t = state_t * tot_b + inc_s[rows(ci), :]
        for ci, o_inter in zip(cis, inter):
            sl = rows(ci)
            o = acc_s[sl, :] + o_inter
            ms = jnp.mean(o * o, axis=-1, keepdims=True)
            y = o * lax.rsqrt(ms + RMS_EPS) * gw_ref[...]
            g = g_ref[rows(ci, tok0), :].astype(F32)
            o_ref[rows(ci, tok0), :] = (y * (g * (1.0 / (1.0 + jnp.exp(-g))))).astype(o_ref.dtype)
        return state_t

    lax.fori_loop(0, n_chunks // n_out, bwd_body, zero)


def _hgrn2(proj, lb_fwd, lb_bwd, hg_norm_w, batch, seq):
    assert HG_DK == V7X_LANES and lb_fwd.shape[0] == 2
    cums, masks = _hg_tables()
    first = 3 * NA_WIDTH // V7X_LANES
    nb = HG_BATCH_PER_STEP
    assert batch % nb == 0
    col = lambda k: pl.BlockSpec((None, nb * seq, V7X_LANES),
                                 lambda b, h, k=k: (first + k * HG_HEADS + h, b, 0))
    lbs = pl.BlockSpec((2, V7X_LANES), lambda b, h: (0, h))
    const = lambda a: pl.BlockSpec(a.shape, lambda b, h, nd=a.ndim: (0,) * nd)
    gw = hg_norm_w.astype(F32).reshape(1, HG_DK)
    return pl.pallas_call(
        functools.partial(_hgrn_kernel, n_batch=nb, seq=seq),
        grid=(batch // nb, HG_HEADS),
        in_specs=[col(0), col(1), col(2), col(3), col(4), lbs, lbs, const(gw),
                  const(cums), const(masks)],
        out_specs=pl.BlockSpec((None, nb * seq, V7X_LANES), lambda b, h: (h, b, 0)),
        out_shape=jax.ShapeDtypeStruct((HG_HEADS, batch * seq, V7X_LANES), BF16),
        scratch_shapes=[pltpu.VMEM((seq, HG_DK), F32), pltpu.VMEM((seq, HG_DK), BF16),
                        pltpu.VMEM((seq, HG_DK), F32),
                        pltpu.VMEM((8 * seq // HG_CHUNK, HG_DK), F32)],
        compiler_params=pltpu.CompilerParams(
            dimension_semantics=("arbitrary", "arbitrary"),
            vmem_limit_bytes=V7X_VMEM_LIMIT_BYTES),
        name="hgrn2",
    )(proj, proj, proj, proj, proj, lb_fwd.astype(F32), lb_bwd.astype(F32), gw, cums, masks)


def _out_mlp_kernel(x_ref, na_ref, hg_ref, wo_na_ref, wo_hg_ref, nw_ref, wu_ref, wd_ref, o_ref):
    wide = lambda ref: jnp.concatenate([ref[j] for j in range(ref.shape[0])], axis=1)
    h = x_ref[...] + _dot(wide(na_ref), wo_na_ref[...]) + _dot(wide(hg_ref), wo_hg_ref[...])
    ms = jnp.mean(h * h, axis=-1, keepdims=True)
    u = (h * lax.rsqrt(ms + RMS_EPS) * nw_ref[...]).astype(BF16)
    mlp = None
    for j in range(D_FF // FF_CHUNK):
        cols = slice(j * FF_CHUNK, (j + 1) * FF_CHUNK)
        a = jnp.maximum(_dot(u, wu_ref[:, cols]), 0.0)
        part = _dot((a * a).astype(BF16), wd_ref[cols, :])
        mlp = part if mlp is None else mlp + part
    o_ref[...] = h + mlp


def _out_mlp(x2, y_na, y_hg, w_out, norm_w, w_up, w_down):
    m = x2.shape[0]
    tile = lambda width: pl.BlockSpec((TOKEN_TILE, width), lambda i: (i, 0))
    blocks = lambda a: pl.BlockSpec((a.shape[0], TOKEN_TILE, V7X_LANES), lambda i: (0, i, 0))
    resident = lambda a: pl.BlockSpec(a.shape, lambda i: (0, 0), pipeline_mode=pl.Buffered(1))
    wo_na, wo_hg = w_out[:NA_WIDTH], w_out[NA_WIDTH:]
    return pl.pallas_call(
        _out_mlp_kernel,
        grid=(m // TOKEN_TILE,),
        in_specs=[tile(D_MODEL), blocks(y_na), blocks(y_hg), resident(wo_na),
                  resident(wo_hg), pl.BlockSpec((1, D_MODEL), lambda i: (0, 0)),
                  resident(w_up), resident(w_down)],
        out_specs=tile(D_MODEL),
        out_shape=jax.ShapeDtypeStruct((m, D_MODEL), F32),
        compiler_params=pltpu.CompilerParams(
            dimension_semantics=("arbitrary",), vmem_limit_bytes=V7X_VMEM_LIMIT_BYTES),
        name="out_mlp",
    )(x2, y_na, y_hg, wo_na, wo_hg, norm_w, w_up, w_down)


def kernel(x, w_in, w_out, attn_norm_w, mlp_norm_w, q_norm_w, k_norm_w, rpb, hg_norm_w,
           lb_fwd, lb_bwd, w_up, w_down):
    batch, seq, d = x.shape
    assert d == D_MODEL and w_in.shape[0] == 1 and seq % GRID_W == 0 and seq % HG_CHUNK == 0
    x2 = x.reshape(batch * seq, d)
    proj = _in_proj(x2, attn_norm_w[0].astype(F32).reshape(1, d), w_in[0].astype(BF16),
                    q_norm_w[0], k_norm_w[0])
    y_na = _natten(proj, rpb[0], batch, seq)
    y_hg = _hgrn2(proj, lb_fwd, lb_bwd, hg_norm_w[0], batch, seq)
    out = _out_mlp(x2, y_na, y_hg, w_out[0].astype(BF16),
                   mlp_norm_w[0].astype(F32).reshape(1, d),
                   w_up[0].astype(BF16), w_down[0].astype(BF16))
    return out.reshape(batch, seq, d)
```

```python
import functools

import numpy as np
import jax
import jax.numpy as jnp
from jax import lax
from jax.experimental import pallas as pl
from jax.experimental.pallas import tpu as pltpu

F32 = jnp.float32
BF16 = jnp.bfloat16

D_MODEL = 1024
GRID_W = 64
NA_HEADS = 8
NA_HEAD_DIM = 64
NA_WIDTH = NA_HEADS * NA_HEAD_DIM
NA_WIN_R = 8
NA_WIN_C = 16
HG_HEADS = 4
HG_DK = 128
HG_WIDTH = HG_HEADS * HG_DK
D_FF = 4 * D_MODEL
D_IN_PROJ = 3 * NA_WIDTH + 5 * HG_WIDTH
RMS_EPS = 1e-6

V7X_LANES = 128
V7X_VMEM_LIMIT_BYTES = 56 * 1024 * 1024

TOKEN_TILE = 512
FF_CHUNK = 1024
HG_CHUNK = 128
HG_LEVELS = 7
HG_CHUNKS_PER_STEP = 8
HG_OUT_CHUNKS_PER_STEP = 8
NA_QT = 16
NA_KT = 32
NA_TILES = GRID_W // NA_QT
NA_GROUP_ROWS = 4
NA_BAND_ROWS = 12
NA_GROUPS_PER_STEP = 2
NA_BATCH_PER_STEP = 2
HG_BATCH_PER_STEP = 2
MASK_NEG = -1e30

_NA_QCOLS = [list(range(0, 8)) + list(range(56, 64)),
             list(range(8, 24)), list(range(24, 40)), list(range(40, 56))]
_NA_KSEGS = [[(0, 16), (48, 16)], [(0, 32)], [(16, 32)], [(32, 32)]]


def _nt_dot(a, b):
    return lax.dot_general(a, b, (((1,), (1,)), ((), ())), preferred_element_type=F32)


def _dot(a, b):
    return jnp.dot(a, b, preferred_element_type=F32)


def _split_bf16(x):
    hi = x.astype(BF16)
    lo = (x - hi.astype(F32)).astype(BF16)
    return hi, lo


def _in_proj_kernel(x_ref, nw_ref, w_ref, qkw_ref, o_ref):
    x = x_ref[...]
    ms = jnp.mean(x * x, axis=-1, keepdims=True)
    u = (x * lax.rsqrt(ms + RMS_EPS) * nw_ref[...]).astype(BF16)
    res = _dot(u, w_ref[...])
    first_head = lax.broadcasted_iota(jnp.int32, (1, V7X_LANES), 1) < NA_HEAD_DIM
    for j in range(D_IN_PROJ // V7X_LANES):
        blk = res[:, j * V7X_LANES:(j + 1) * V7X_LANES]
        if j < qkw_ref.shape[0]:
            sq = blk * blk
            s_all = jnp.sum(sq, axis=-1, keepdims=True)
            s_one = jnp.sum(jnp.where(first_head, sq, 0.0), axis=-1, keepdims=True)
            ms_h = jnp.where(first_head, s_one, s_all - s_one) * (1.0 / NA_HEAD_DIM)
            blk = blk * lax.rsqrt(ms_h + RMS_EPS) * qkw_ref[j]
        o_ref[j] = blk.astype(o_ref.dtype)


def _in_proj(x2, norm_w, w_in, q_norm_w, k_norm_w):
    m = x2.shape[0]
    pair_w = lambda w, mult: jnp.tile(w.astype(F32) * mult, 2).reshape(1, 1, V7X_LANES)
    n_blk = NA_WIDTH // V7X_LANES
    qkw = jnp.concatenate([jnp.tile(pair_w(q_norm_w, NA_HEAD_DIM ** -0.5), (n_blk, 1, 1)),
                           jnp.tile(pair_w(k_norm_w, 1.0), (n_blk, 1, 1))])
    return pl.pallas_call(
        _in_proj_kernel,
        grid=(m // TOKEN_TILE,),
        in_specs=[
            pl.BlockSpec((TOKEN_TILE, D_MODEL), lambda i: (i, 0)),
            pl.BlockSpec((1, D_MODEL), lambda i: (0, 0)),
            pl.BlockSpec((D_MODEL, D_IN_PROJ), lambda i: (0, 0), pipeline_mode=pl.Buffered(1)),
            pl.BlockSpec(qkw.shape, lambda i: (0, 0, 0)),
        ],
        out_specs=pl.BlockSpec((D_IN_PROJ // V7X_LANES, TOKEN_TILE, V7X_LANES), lambda i: (0, i, 0)),
        out_shape=jax.ShapeDtypeStruct((D_IN_PROJ // V7X_LANES, m, V7X_LANES), BF16),
        compiler_params=pltpu.CompilerParams(
            dimension_semantics=("arbitrary",), vmem_limit_bytes=V7X_VMEM_LIMIT_BYTES),
        name="in_proj",
    )(x2, norm_w, w_in, qkw)


def _na_group_plan(rows):
    wr = min(NA_WIN_R, rows)
    bases, types, sigs = [], [], []
    for g in range(rows // NA_GROUP_ROWS):
        r0 = g * NA_GROUP_ROWS
        starts = [min(max(r0 + j - wr // 2, 0), rows - wr) for j in range(NA_GROUP_ROWS)]
        base = min(max(starts[0], 0), rows - NA_BAND_ROWS)
        assert base <= starts[0] and starts[-1] + wr <= base + NA_BAND_ROWS
        sig = tuple((starts[j] - base, r0 + j - base) for j in range(NA_GROUP_ROWS))
        if sig not in sigs:
            sigs.append(sig)
        bases.append(base)
        types.append(sigs.index(sig))
    return bases, types, sigs


def _na_bias_table(rpb, sigs):
    n_ri, n_ci = 2 * NA_WIN_R - 1, 2 * NA_WIN_C - 1
    csel = np.zeros((n_ci, NA_TILES, NA_QT, NA_KT), np.float32)
    for t in range(NA_TILES):
        qc = np.array(_NA_QCOLS[t])
        kc = np.concatenate([np.arange(c0, c0 + n) for c0, n in _NA_KSEGS[t]])
        cs = np.clip(qc - NA_WIN_C // 2, 0, GRID_W - NA_WIN_C)
        valid = (kc[None, :] >= cs[:, None]) & (kc[None, :] < cs[:, None] + NA_WIN_C)
        cidx = kc[None, :] - qc[:, None] + NA_WIN_C - 1
        qi, ki = np.nonzero(valid)
        csel[cidx[qi, ki], t, qi, ki] = 1.0
    cols = jnp.einsum("hrc,ctqk->htqrk", rpb.astype(F32), jnp.asarray(csel),
                      precision=lax.Precision.HIGHEST)
    cols = cols + jnp.asarray(np.where(csel.sum(0) > 0, 0.0, MASK_NEG)[None, :, :, None, :], F32)
    cols = cols.reshape(NA_HEADS // 2, 2, NA_TILES, NA_QT, n_ri * NA_KT)
    cols = jnp.transpose(cols, (0, 2, 1, 3, 4))
    tabs = []
    for sig in sigs:
        per_row = []
        for first, qrow in sig:
            ri0 = first - qrow + NA_WIN_R - 1
            after = NA_BAND_ROWS - first - NA_WIN_R
            window = cols[..., ri0 * NA_KT:(ri0 + NA_WIN_R) * NA_KT]
            per_row.append(jnp.pad(window, [(0, 0)] * 4 + [(first * NA_KT, after * NA_KT)],
                                   constant_values=MASK_NEG))
        tabs.append(jnp.stack(per_row, axis=2))
    return jnp.stack(tabs).reshape(len(sigs), NA_HEADS // 2, NA_TILES,
                                   NA_GROUP_ROWS * 2 * NA_QT, NA_BAND_ROWS * NA_KT)


def _natten_kernel(*refs, n_batch, rows, bases, types):
    def one(bb, carry):
        _natten_one(*refs, bb * rows * GRID_W, rows=rows, bases=bases, types=types)
        return carry

    lax.fori_loop(0, n_batch, one, 0)


def _natten_one(q_ref, k_ref, v_ref, bias_ref, o_ref, tok0, *, rows, bases, types):
    lane = lax.broadcasted_iota(jnp.int32, (NA_QT, V7X_LANES), 1)
    first_head = lane < NA_HEAD_DIM
    gr = NA_GROUP_ROWS

    def row_slice(grid_row):
        return pl.ds(pl.multiple_of(tok0 + grid_row * GRID_W, GRID_W), GRID_W)

    def row_block(ref, grid_row):
        return ref[row_slice(grid_row), :]

    def band(blocks, t):
        return jnp.concatenate(
            [blk[c0:c0 + n] for blk in blocks for (c0, n) in _NA_KSEGS[t]], axis=0)

    def step(groups):
        scores = []
        for row0, base, y in groups:
            kblocks = [row_block(k_ref, base + u) for u in range(NA_BAND_ROWS)]
            qrows = [row_block(q_ref, row0 + j).astype(F32) for j in range(gr)]
            for t in range(NA_TILES):
                qc = _NA_QCOLS[t]
                q2 = []
                for j in range(gr):
                    if t == 0:
                        qt = jnp.concatenate([qrows[j][0:8], qrows[j][56:64]], axis=0)
                    else:
                        qt = qrows[j][qc[0]:qc[0] + NA_QT]
                    q2 += [jnp.where(first_head, qt, 0.0), jnp.where(first_head, 0.0, qt)]
                s = _nt_dot(jnp.concatenate(q2, axis=0).astype(BF16), band(kblocks, t))
                scores.append(s + bias_ref[y, t])
        probs, sums = [], []
        for s in scores:
            p = jnp.exp(s - jnp.max(s, axis=-1, keepdims=True))
            sums.append(jnp.sum(p, axis=-1, keepdims=True))
            probs.append(p.astype(BF16))
        for g, (row0, base, y) in enumerate(groups):
            vblocks = [row_block(v_ref, base + u) for u in range(NA_BAND_ROWS)]
            o_tiles = []
            for t in range(NA_TILES):
                o2 = _dot(probs[g * NA_TILES + t], band(vblocks, t)) / sums[g * NA_TILES + t]
                o_tiles.append(o2)
            for j in range(gr):
                sel = [jnp.where(first_head, o[2 * NA_QT * j:2 * NA_QT * j + NA_QT],
                                 o[2 * NA_QT * j + NA_QT:2 * NA_QT * (j + 1)]) for o in o_tiles]
                orow = jnp.concatenate([sel[0][0:8], sel[1], sel[2], sel[3], sel[0][8:16]], axis=0)
                o_ref[row_slice(row0 + j), :] = orow.astype(o_ref.dtype)

    n_groups = rows // gr
    shift = min(NA_WIN_R, rows) // 2
    regular = [bases[g] == g * gr - shift for g in range(n_groups)]
    lo = regular.index(True)
    hi = n_groups - regular[::-1].index(True)
    per = NA_GROUPS_PER_STEP
    assert all(regular[lo:hi]) and (hi - lo) % per == 0
    assert len({types[g] for g in range(lo, hi)}) == 1
    edge = [(g * gr, bases[g], types[g]) for g in list(range(lo)) + list(range(hi, n_groups))]
    for i in range(0, len(edge), per):
        step(edge[i:i + per])

    def regular_body(i, carry):
        g0 = lo + i * per
        step([((g0 + n) * gr, (g0 + n) * gr - shift, types[lo]) for n in range(per)])
        return carry

    lax.fori_loop(0, (hi - lo) // per, regular_body, 0)


def _natten(proj, rpb, batch, seq):
    rows = seq // GRID_W
    assert rows >= NA_BAND_ROWS and rows % NA_GROUP_ROWS == 0
    assert GRID_W == 64 and NA_HEAD_DIM * 2 == V7X_LANES
    pairs = NA_HEADS // 2
    bases, types, sigs = _na_group_plan(rows)
    bias = _na_bias_table(rpb, sigs)
    nb = NA_BATCH_PER_STEP
    assert batch % nb == 0
    col = lambda off: pl.BlockSpec((None, nb * seq, V7X_LANES),
                                   lambda p, b, off=off: (off + p, b, 0))
    return pl.pallas_call(
        functools.partial(_natten_kernel, n_batch=nb, rows=rows, bases=tuple(bases),
                          types=tuple(types)),
        grid=(pairs, batch // nb),
        in_specs=[
            col(0), col(pairs), col(2 * pairs),
            pl.BlockSpec((len(sigs), None, NA_TILES, NA_GROUP_ROWS * 2 * NA_QT,
                          NA_BAND_ROWS * NA_KT), lambda p, b: (0, p, 0, 0, 0)),
        ],
        out_specs=pl.BlockSpec((None, nb * seq, V7X_LANES), lambda p, b: (p, b, 0)),
        out_shape=jax.ShapeDtypeStruct((pairs, batch * seq, V7X_LANES), BF16),
        compiler_params=pltpu.CompilerParams(
            dimension_semantics=("arbitrary", "arbitrary"),
            vmem_limit_bytes=V7X_VMEM_LIMIT_BYTES),
        name="natten",
    )(proj, proj, proj, bias)


def _hg_tables():
    c = HG_CHUNK
    idx = np.arange(c)
    lower = (idx[None, :] <= idx[:, None]).astype(np.float32)
    cums = np.stack([np.concatenate([lower, lower], axis=1),
                     np.concatenate([lower.T, lower.T], axis=1)])
    masks = []
    for lvl in range(HG_LEVELS):
        s = 1 << lvl
        same_block = (idx[:, None] // (2 * s)) == (idx[None, :] // (2 * s))
        half = (idx // s) % 2
        masks.append(same_block & (half[:, None] != half[None, :]))
    masks.append(np.eye(c, dtype=bool))
    return jnp.asarray(cums, BF16), jnp.asarray(np.stack(masks), F32)


def _hgrn_kernel(*refs, n_batch, seq):
    def one(bb, carry):
        _hgrn_one(*refs, bb * seq, seq=seq)
        return carry

    lax.fori_loop(0, n_batch, one, 0)


def _hgrn_one(q_ref, ff_ref, fb_ref, v_ref, lbf_ref, lbb_ref,
              cum_ref, mask_ref, o_ref, acc_s, qb_s, inc_s, etb_s, tok0, *, seq):
    c = HG_CHUNK
    n_chunks = seq // c
    n_step = HG_CHUNKS_PER_STEP
    assert n_chunks % n_step == 0
    sub = lax.broadcasted_iota(jnp.int32, (c, V7X_LANES), 0) & 7

    def lower_bound(lb_ref):
        a = lb_ref[...].astype(F32)
        a0, a1 = a[0:1], a[1:2]
        m = jnp.maximum(a0, a1)
        e0, e1 = jnp.exp(a0 - m), jnp.exp(a1 - m)
        return e0 / (e0 + e1)

    lbf = lower_bound(lbf_ref)
    lbb = lower_bound(lbb_ref)

    def rows(ci, tok=0):
        return pl.ds(pl.multiple_of(tok + ci * c, c), c)

    def group_rows(x, r):
        return jnp.concatenate(
            [jnp.broadcast_to(x[8 * j + r:8 * j + r + 1], (8, V7X_LANES)) for j in range(c // 8)],
            axis=0)

    def repeat8(x8, n_rows):
        return x8 if n_rows == 8 else jnp.concatenate([x8] * (n_rows // 8), axis=0)

    def gates(f_ref, sl, lb):
        hf = f_ref[sl, :].astype(F32)
        f = lb + (1.0 - lb) * (1.0 / (1.0 + jnp.exp(-hf)))
        hi, lo = _split_bf16(jnp.log2(f))
        return f, 1.0 - f, jnp.concatenate([hi, lo], axis=0)

    def stage_gates(ci):
        src = rows(ci, tok0)
        st = dict(sl=rows(ci), q=q_ref[src, :].astype(F32), v=v_ref[src, :])
        st["f_f"], st["k_f"], lf_f = gates(ff_ref, src, lbf)
        st["f_b"], st["k_b"], lf_b = gates(fb_ref, src, lbb)
        st["b_f"] = _dot(cum_ref[0], lf_f)
        st["b_b"] = _dot(cum_ref[1], lf_b)
        return st

    def level_operands(st, lvl):
        q, k_f, k_b, b_f, b_b = st["q"], st["k_f"], st["k_b"], st["b_f"], st["b_b"]
        if lvl == HG_LEVELS:
            return q, k_f + k_b
        if lvl == 0:
            odd = (sub & 1) == 1
            return q * jnp.where(odd, st["f_f"], st["f_b"]), jnp.where(odd, k_b, k_f)
        if lvl < 3:
            later = (sub & (1 << lvl)) != 0
            if lvl == 1:
                ref_f = jnp.where(sub < 4, group_rows(b_f, 1), group_rows(b_f, 5))
                ref_b = jnp.where(sub < 4, group_rows(b_b, 2), group_rows(b_b, 6))
            else:
                ref_f, ref_b = group_rows(b_f, 3), group_rows(b_b, 4)
            d_f, d_b = b_f - ref_f, b_b - ref_b
            return (q * jnp.exp2(jnp.where(later, d_f, d_b)),
                    jnp.where(later, k_b, k_f) * jnp.exp2(-jnp.where(later, d_b, d_f)))
        s = 1 << lvl
        xp, yp = [], []
        for r0 in range(0, c, 2 * s):
            m = r0 + s
            rf = repeat8(st["last_f"][m - 8:m], s)
            rb = repeat8(st["first_b"][m:m + 8], s)
            early, late = slice(r0, m), slice(m, m + s)
            xp += [q[early] * jnp.exp2(b_b[early] - rb), q[late] * jnp.exp2(b_f[late] - rf)]
            yp += [k_f[early] * jnp.exp2(rf - b_f[early]), k_b[late] * jnp.exp2(rb - b_b[late])]
        return jnp.concatenate(xp, axis=0), jnp.concatenate(yp, axis=0)

    def stage_operands(ci, st):
        q, k_f, k_b, b_f, b_b = st["q"], st["k_f"], st["k_b"], st["b_f"], st["b_b"]
        last_f = st["last_f"] = group_rows(b_f, 7)
        first_b = st["first_b"] = group_rows(b_b, 0)
        sl = st["sl"]
        tot_f = last_f[c - 8:c]
        tot_b = first_b[0:8]
        st["qe_f"] = (q * jnp.exp2(b_f)).astype(BF16)
        st["kd_f"] = (k_f * jnp.exp2(repeat8(tot_f, c) - b_f)).astype(BF16)
        st["tot_f"] = jnp.exp2(tot_f[0:1])
        qb_s[sl, :] = (q * jnp.exp2(b_b)).astype(BF16)
        st["kd_b"] = (k_b * jnp.exp2(repeat8(tot_b, c) - b_b)).astype(BF16)
        etb_s[pl.ds(pl.multiple_of(ci * 8, 8), 8), :] = jnp.exp2(tot_b)
        return st

    def fwd_body(i, state_t):
        cis = [i * n_step + j for j in range(n_step)]
        sts = [stage_gates(ci) for ci in cis]
        sts = [stage_operands(ci, st) for ci, st in zip(cis, sts)]
        for st in sts:
            for lvl in range(HG_LEVELS + 1):
                x, y = level_operands(st, lvl)
                term = mask_ref[lvl] * _nt_dot(x.astype(BF16), y.astype(BF16))
                st["a"] = term if lvl == 0 else st["a"] + term
        intra = [_dot(st["a"].astype(BF16), st["v"]) for st in sts]
        incs = [_dot(st["v"].T, jnp.concatenate([st["kd_f"], st["kd_b"]], axis=1)) for st in sts]
        for st, o, inc in zip(sts, intra, incs):
            acc_s[st["sl"], :] = o + _nt_dot(st["qe_f"], state_t.astype(BF16))
            state_t = state_t * st["tot_f"] + inc[:, 0:HG_DK]
            inc_s[st["sl"], :] = inc[:, HG_DK:2 * HG_DK]
        return state_t

    zero = jnp.zeros((HG_DK, HG_DK), F32)
    lax.fori_loop(0, n_chunks // n_step, fwd_body, zero)

    n_out = HG_OUT_CHUNKS_PER_STEP
    assert n_chunks % n_out == 0

    def bwd_body(i, state_t):
        cis = [n_chunks - 1 - (i * n_out + j) for j in range(n_out)]
        inter = []
        for ci in cis:
            inter.append(_nt_dot(qb_s[rows(ci), :], state_t.astype(BF16)))
            tot_b = etb_s[pl.ds(pl.multiple_of(ci * 8, 8), 1), :]
            state_t = state_t * tot_b + inc_s[rows(ci), :]
        for ci, o_inter in zip(cis, inter):
            o_ref[rows(ci, tok0), :] = (acc_s[rows(ci), :] + o_inter).astype(o_ref.dtype)
        return state_t

    lax.fori_loop(0, n_chunks // n_out, bwd_body, zero)


def _hgrn2(proj, lb_fwd, lb_bwd, batch, seq):
    assert HG_DK == V7X_LANES and lb_fwd.shape[0] == 2
    cums, masks = _hg_tables()
    first = 3 * NA_WIDTH // V7X_LANES
    nb = HG_BATCH_PER_STEP
    assert batch % nb == 0
    col = lambda k: pl.BlockSpec((None, nb * seq, V7X_LANES),
                                 lambda b, h, k=k: (first + k * HG_HEADS + h, b, 0))
    lbs = pl.BlockSpec((2, V7X_LANES), lambda b, h: (0, h))
    const = lambda a: pl.BlockSpec(a.shape, lambda b, h, nd=a.ndim: (0,) * nd)
    return pl.pallas_call(
        functools.partial(_hgrn_kernel, n_batch=nb, seq=seq),
        grid=(batch // nb, HG_HEADS),
        in_specs=[col(0), col(1), col(2), col(3), lbs, lbs, const(cums), const(masks)],
        out_specs=pl.BlockSpec((None, nb * seq, V7X_LANES), lambda b, h: (h, b, 0)),
        out_shape=jax.ShapeDtypeStruct((HG_HEADS, batch * seq, V7X_LANES), BF16),
        scratch_shapes=[pltpu.VMEM((seq, HG_DK), F32), pltpu.VMEM((seq, HG_DK), BF16),
                        pltpu.VMEM((seq, HG_DK), F32),
                        pltpu.VMEM((8 * seq // HG_CHUNK, HG_DK), F32)],
        compiler_params=pltpu.CompilerParams(
            dimension_semantics=("arbitrary", "arbitrary"),
            vmem_limit_bytes=V7X_VMEM_LIMIT_BYTES),
        name="hgrn2",
    )(proj, proj, proj, proj, lb_fwd.astype(F32), lb_bwd.astype(F32), cums, masks)


def _out_mlp_kernel(x_ref, na_ref, hg_ref, g_ref, gw_ref, wo_na_ref, wo_hg_ref, nw_ref, wu_ref,
                    wd_ref, o_ref):
    def recurrence_head(j):
        o = hg_ref[j].astype(F32)
        ms = jnp.mean(o * o, axis=-1, keepdims=True)
        g = g_ref[j].astype(F32)
        return (o * lax.rsqrt(ms + RMS_EPS) * gw_ref[...]
                * (g * (1.0 / (1.0 + jnp.exp(-g))))).astype(BF16)

    y_na = jnp.concatenate([na_ref[j] for j in range(na_ref.shape[0])], axis=1)
    y_hg = jnp.concatenate([recurrence_head(j) for j in range(HG_HEADS)], axis=1)
    h = x_ref[...] + _dot(y_na, wo_na_ref[...]) + _dot(y_hg, wo_hg_ref[...])
    ms = jnp.mean(h * h, axis=-1, keepdims=True)
    u = (h * lax.rsqrt(ms + RMS_EPS) * nw_ref[...]).astype(BF16)
    mlp = None
    for j in range(D_FF // FF_CHUNK):
        cols = slice(j * FF_CHUNK, (j + 1) * FF_CHUNK)
        a = jnp.maximum(_dot(u, wu_ref[:, cols]), 0.0)
        part = _dot((a * a).astype(BF16), wd_ref[cols, :])
        mlp = part if mlp is None else mlp + part
    o_ref[...] = h + mlp


def _out_mlp(x2, y_na, o_hg, proj, hg_norm_w, w_out, norm_w, w_up, w_down):
    m = x2.shape[0]
    first_g = (3 * NA_WIDTH + 4 * HG_WIDTH) // V7X_LANES
    assert first_g % HG_HEADS == 0
    gate = pl.BlockSpec((HG_HEADS, TOKEN_TILE, V7X_LANES), lambda i: (first_g // HG_HEADS, i, 0))
    gw = hg_norm_w.astype(F32).reshape(1, HG_DK)
    tile = lambda width: pl.BlockSpec((TOKEN_TILE, width), lambda i: (i, 0))
    blocks = lambda a: pl.BlockSpec((a.shape[0], TOKEN_TILE, V7X_LANES), lambda i: (0, i, 0))
    resident = lambda a: pl.BlockSpec(a.shape, lambda i: (0, 0), pipeline_mode=pl.Buffered(1))
    wo_na, wo_hg = w_out[:NA_WIDTH], w_out[NA_WIDTH:]
    return pl.pallas_call(
        _out_mlp_kernel,
        grid=(m // TOKEN_TILE,),
        in_specs=[tile(D_MODEL), blocks(y_na), blocks(o_hg), gate,
                  pl.BlockSpec((1, HG_DK), lambda i: (0, 0)), resident(wo_na),
                  resident(wo_hg), pl.BlockSpec((1, D_MODEL), lambda i: (0, 0)),
                  resident(w_up), resident(w_down)],
        out_specs=tile(D_MODEL),
        out_shape=jax.ShapeDtypeStruct((m, D_MODEL), F32),
        compiler_params=pltpu.CompilerParams(
            dimension_semantics=("arbitrary",), vmem_limit_bytes=V7X_VMEM_LIMIT_BYTES),
        name="out_mlp",
    )(x2, y_na, o_hg, proj, gw, wo_na, wo_hg, norm_w, w_up, w_down)


def kernel(x, w_in, w_out, attn_norm_w, mlp_norm_w, q_norm_w, k_norm_w, rpb, hg_norm_w,
           lb_fwd, lb_bwd, w_up, w_down):
    batch, seq, d = x.shape
    assert d == D_MODEL and w_in.shape[0] == 1 and seq % GRID_W == 0 and seq % HG_CHUNK == 0
    x2 = x.reshape(batch * seq, d)
    proj = _in_proj(x2, attn_norm_w[0].astype(F32).reshape(1, d), w_in[0].astype(BF16),
                    q_norm_w[0], k_norm_w[0])
    y_na = _natten(proj, rpb[0], batch, seq)
    o_hg = _hgrn2(proj, lb_fwd, lb_bwd, batch, seq)
    out = _out_mlp(x2, y_na, o_hg, proj, hg_norm_w[0], w_out[0].astype(BF16),
                   mlp_norm_w[0].astype(F32).reshape(1, d),
                   w_up[0].astype(BF16), w_down[0].astype(BF16))
    return out.reshape(batch, seq, d)
```

```python
import functools

import numpy as np
import jax
import jax.numpy as jnp
from jax import lax
from jax.experimental import pallas as pl
from jax.experimental.pallas import tpu as pltpu

F32 = jnp.float32
BF16 = jnp.bfloat16

D_MODEL = 1024
GRID_W = 64
NA_HEADS = 8
NA_HEAD_DIM = 64
NA_WIDTH = NA_HEADS * NA_HEAD_DIM
NA_WIN_R = 8
NA_WIN_C = 16
HG_HEADS = 4
HG_DK = 128
HG_WIDTH = HG_HEADS * HG_DK
D_FF = 4 * D_MODEL
D_IN_PROJ = 3 * NA_WIDTH + 5 * HG_WIDTH
RMS_EPS = 1e-6

V7X_LANES = 128
V7X_VMEM_LIMIT_BYTES = 56 * 1024 * 1024

TOKEN_TILE = 1024
FF_CHUNK = 1024
HG_CHUNK = 128
HG_LEVELS = 7
HG_CHUNKS_PER_STEP = 8
HG_OUT_CHUNKS_PER_STEP = 8
NA_QT = 16
NA_KT = 32
NA_TILES = GRID_W // NA_QT
NA_GROUP_ROWS = 4
NA_BAND_ROWS = 12
NA_GROUPS_PER_STEP = 2
NA_BATCH_PER_STEP = 2
HG_BATCH_PER_STEP = 2
MASK_NEG = -1e30

_NA_QCOLS = [list(range(0, 8)) + list(range(56, 64)),
             list(range(8, 24)), list(range(24, 40)), list(range(40, 56))]
_NA_KSEGS = [[(0, 16), (48, 16)], [(0, 32)], [(16, 32)], [(32, 32)]]


def _nt_dot(a, b):
    return lax.dot_general(a, b, (((1,), (1,)), ((), ())), preferred_element_type=F32)


def _dot(a, b):
    return jnp.dot(a, b, preferred_element_type=F32)


def _split_bf16(x):
    hi = x.astype(BF16)
    lo = (x - hi.astype(F32)).astype(BF16)
    return hi, lo


def _in_proj_kernel(x_ref, nw_ref, w_ref, qkw_ref, o_ref):
    x = x_ref[...]
    ms = jnp.mean(x * x, axis=-1, keepdims=True)
    u = (x * lax.rsqrt(ms + RMS_EPS) * nw_ref[...]).astype(BF16)
    res = _dot(u, w_ref[...])
    first_head = lax.broadcasted_iota(jnp.int32, (1, V7X_LANES), 1) < NA_HEAD_DIM
    for j in range(D_IN_PROJ // V7X_LANES):
        blk = res[:, j * V7X_LANES:(j + 1) * V7X_LANES]
        if j < qkw_ref.shape[0]:
            sq = blk * blk
            s_all = jnp.sum(sq, axis=-1, keepdims=True)
            s_one = jnp.sum(jnp.where(first_head, sq, 0.0), axis=-1, keepdims=True)
            ms_h = jnp.where(first_head, s_one, s_all - s_one) * (1.0 / NA_HEAD_DIM)
            blk = blk * lax.rsqrt(ms_h + RMS_EPS) * qkw_ref[j]
        o_ref[j] = blk.astype(o_ref.dtype)


def _in_proj(x2, norm_w, w_in, q_norm_w, k_norm_w):
    m = x2.shape[0]
    pair_w = lambda w, mult: jnp.tile(w.astype(F32) * mult, 2).reshape(1, 1, V7X_LANES)
    n_blk = NA_WIDTH // V7X_LANES
    qkw = jnp.concatenate([jnp.tile(pair_w(q_norm_w, NA_HEAD_DIM ** -0.5), (n_blk, 1, 1)),
                           jnp.tile(pair_w(k_norm_w, 1.0), (n_blk, 1, 1))])
    return pl.pallas_call(
        _in_proj_kernel,
        grid=(m // TOKEN_TILE,),
        in_specs=[
            pl.BlockSpec((TOKEN_TILE, D_MODEL), lambda i: (i, 0)),
            pl.BlockSpec((1, D_MODEL), lambda i: (0, 0)),
            pl.BlockSpec((D_MODEL, D_IN_PROJ), lambda i: (0, 0), pipeline_mode=pl.Buffered(1)),
            pl.BlockSpec(qkw.shape, lambda i: (0, 0, 0)),
        ],
        out_specs=pl.BlockSpec((D_IN_PROJ // V7X_LANES, TOKEN_TILE, V7X_LANES), lambda i: (0, i, 0)),
        out_shape=jax.ShapeDtypeStruct((D_IN_PROJ // V7X_LANES, m, V7X_LANES), BF16),
        compiler_params=pltpu.CompilerParams(
            dimension_semantics=("arbitrary",), vmem_limit_bytes=V7X_VMEM_LIMIT_BYTES),
        name="in_proj",
    )(x2, norm_w, w_in, qkw)


def _na_group_plan(rows):
    wr = min(NA_WIN_R, rows)
    bases, types, sigs = [], [], []
    for g in range(rows // NA_GROUP_ROWS):
        r0 = g * NA_GROUP_ROWS
        starts = [min(max(r0 + j - wr // 2, 0), rows - wr) for j in range(NA_GROUP_ROWS)]
        base = min(max(starts[0], 0), rows - NA_BAND_ROWS)
        assert base <= starts[0] and starts[-1] + wr <= base + NA_BAND_ROWS
        sig = tuple((starts[j] - base, r0 + j - base) for j in range(NA_GROUP_ROWS))
        if sig not in sigs:
            sigs.append(sig)
        bases.append(base)
        types.append(sigs.index(sig))
    return bases, types, sigs


def _na_bias_table(rpb, sigs):
    n_ri, n_ci = 2 * NA_WIN_R - 1, 2 * NA_WIN_C - 1
    csel = np.zeros((n_ci, NA_TILES, NA_QT, NA_KT), np.float32)
    for t in range(NA_TILES):
        qc = np.array(_NA_QCOLS[t])
        kc = np.concatenate([np.arange(c0, c0 + n) for c0, n in _NA_KSEGS[t]])
        cs = np.clip(qc - NA_WIN_C // 2, 0, GRID_W - NA_WIN_C)
        valid = (kc[None, :] >= cs[:, None]) & (kc[None, :] < cs[:, None] + NA_WIN_C)
        cidx = kc[None, :] - qc[:, None] + NA_WIN_C - 1
        qi, ki = np.nonzero(valid)
        csel[cidx[qi, ki], t, qi, ki] = 1.0
    cols = jnp.einsum("hrc,ctqk->htqrk", rpb.astype(F32), jnp.asarray(csel),
                      precision=lax.Precision.HIGHEST)
    cols = cols + jnp.asarray(np.where(csel.sum(0) > 0, 0.0, MASK_NEG)[None, :, :, None, :], F32)
    cols = cols.reshape(NA_HEADS // 2, 2, NA_TILES, NA_QT, n_ri * NA_KT)
    cols = jnp.transpose(cols, (0, 2, 1, 3, 4))
    tabs = []
    for sig in sigs:
        per_row = []
        for first, qrow in sig:
            ri0 = first - qrow + NA_WIN_R - 1
            after = NA_BAND_ROWS - first - NA_WIN_R
            window = cols[..., ri0 * NA_KT:(ri0 + NA_WIN_R) * NA_KT]
            per_row.append(jnp.pad(window, [(0, 0)] * 4 + [(first * NA_KT, after * NA_KT)],
                                   constant_values=MASK_NEG))
        tabs.append(jnp.stack(per_row, axis=2))
    return jnp.stack(tabs).reshape(len(sigs), NA_HEADS // 2, NA_TILES,
                                   NA_GROUP_ROWS * 2 * NA_QT, NA_BAND_ROWS * NA_KT)


def _natten_kernel(*refs, n_batch, rows, bases, types):
    def one(bb, carry):
        _natten_one(*refs, bb * rows * GRID_W, rows=rows, bases=bases, types=types)
        return carry

    lax.fori_loop(0, n_batch, one, 0)


def _natten_one(q_ref, k_ref, v_ref, bias_ref, o_ref, tok0, *, rows, bases, types):
    lane = lax.broadcasted_iota(jnp.int32, (NA_QT, V7X_LANES), 1)
    first_head = lane < NA_HEAD_DIM
    gr = NA_GROUP_ROWS

    def row_slice(grid_row):
        return pl.ds(pl.multiple_of(tok0 + grid_row * GRID_W, GRID_W), GRID_W)

    def row_block(ref, grid_row):
        return ref[row_slice(grid_row), :]

    def band(blocks, t):
        return jnp.concatenate(
            [blk[c0:c0 + n] for blk in blocks for (c0, n) in _NA_KSEGS[t]], axis=0)

    def step(groups):
        scores = []
        for row0, base, y in groups:
            kblocks = [row_block(k_ref, base + u) for u in range(NA_BAND_ROWS)]
            qrows = [row_block(q_ref, row0 + j).astype(F32) for j in range(gr)]
            for t in range(NA_TILES):
                qc = _NA_QCOLS[t]
                q2 = []
                for j in range(gr):
                    if t == 0:
                        qt = jnp.concatenate([qrows[j][0:8], qrows[j][56:64]], axis=0)
                    else:
                        qt = qrows[j][qc[0]:qc[0] + NA_QT]
                    q2 += [jnp.where(first_head, qt, 0.0), jnp.where(first_head, 0.0, qt)]
                s = _nt_dot(jnp.concatenate(q2, axis=0).astype(BF16), band(kblocks, t))
                scores.append(s + bias_ref[y, t])
        probs, sums = [], []
        for s in scores:
            p = jnp.exp(s - jnp.max(s, axis=-1, keepdims=True))
            sums.append(jnp.sum(p, axis=-1, keepdims=True))
            probs.append(p.astype(BF16))
        for g, (row0, base, y) in enumerate(groups):
            vblocks = [row_block(v_ref, base + u) for u in range(NA_BAND_ROWS)]
            o_tiles = []
            for t in range(NA_TILES):
                o2 = _dot(probs[g * NA_TILES + t], band(vblocks, t)) / sums[g * NA_TILES + t]
                o_tiles.append(o2)
            for j in range(gr):
                sel = [jnp.where(first_head, o[2 * NA_QT * j:2 * NA_QT * j + NA_QT],
                                 o[2 * NA_QT * j + NA_QT:2 * NA_QT * (j + 1)]) for o in o_tiles]
                orow = jnp.concatenate([sel[0][0:8], sel[1], sel[2], sel[3], sel[0][8:16]], axis=0)
                o_ref[row_slice(row0 + j), :] = orow.astype(o_ref.dtype)

    n_groups = rows // gr
    shift = min(NA_WIN_R, rows) // 2
    regular = [bases[g] == g * gr - shift for g in range(n_groups)]
    lo = regular.index(True)
    hi = n_groups - regular[::-1].index(True)
    per = NA_GROUPS_PER_STEP
    assert all(regular[lo:hi]) and (hi - lo) % per == 0
    assert len({types[g] for g in range(lo, hi)}) == 1
    edge = [(g * gr, bases[g], types[g]) for g in list(range(lo)) + list(range(hi, n_groups))]
    for i in range(0, len(edge), per):
        step(edge[i:i + per])

    def regular_body(i, carry):
        g0 = lo + i * per
        step([((g0 + n) * gr, (g0 + n) * gr - shift, types[lo]) for n in range(per)])
        return carry

    lax.fori_loop(0, (hi - lo) // per, regular_body, 0)


def _natten(proj, rpb, batch, seq):
    rows = seq // GRID_W
    assert rows >= NA_BAND_ROWS and rows % NA_GROUP_ROWS == 0
    assert GRID_W == 64 and NA_HEAD_DIM * 2 == V7X_LANES
    pairs = NA_HEADS // 2
    bases, types, sigs = _na_group_plan(rows)
    bias = _na_bias_table(rpb, sigs)
    nb = NA_BATCH_PER_STEP
    assert batch % nb == 0
    col = lambda off: pl.BlockSpec((None, nb * seq, V7X_LANES),
                                   lambda p, b, off=off: (off + p, b, 0))
    return pl.pallas_call(
        functools.partial(_natten_kernel, n_batch=nb, rows=rows, bases=tuple(bases),
                          types=tuple(types)),
        grid=(pairs, batch // nb),
        in_specs=[
            col(0), col(pairs), col(2 * pairs),
            pl.BlockSpec((len(sigs), None, NA_TILES, NA_GROUP_ROWS * 2 * NA_QT,
                          NA_BAND_ROWS * NA_KT), lambda p, b: (0, p, 0, 0, 0)),
        ],
        out_specs=pl.BlockSpec((None, nb * seq, V7X_LANES), lambda p, b: (p, b, 0)),
        out_shape=jax.ShapeDtypeStruct((pairs, batch * seq, V7X_LANES), BF16),
        compiler_params=pltpu.CompilerParams(
            dimension_semantics=("arbitrary", "arbitrary"),
            vmem_limit_bytes=V7X_VMEM_LIMIT_BYTES),
        name="natten",
    )(proj, proj, proj, bias)


def _hg_tables():
    c = HG_CHUNK
    idx = np.arange(c)
    lower = (idx[None, :] <= idx[:, None]).astype(np.float32)
    cums = np.stack([np.concatenate([lower, lower], axis=1),
                     np.concatenate([lower.T, lower.T], axis=1)])
    masks = []
    for lvl in range(HG_LEVELS):
        s = 1 << lvl
        same_block = (idx[:, None] // (2 * s)) == (idx[None, :] // (2 * s))
        half = (idx // s) % 2
        masks.append(same_block & (half[:, None] != half[None, :]))
    masks.append(np.eye(c, dtype=bool))
    return jnp.asarray(cums, BF16), jnp.asarray(np.stack(masks), F32)


def _hgrn_kernel(*refs, n_batch, seq):
    def one(bb, carry):
        _hgrn_one(*refs, bb * seq, seq=seq)
        return carry

    lax.fori_loop(0, n_batch, one, 0)


def _hgrn_one(q_ref, ff_ref, fb_ref, v_ref, lbf_ref, lbb_ref,
              cum_ref, mask_ref, o_ref, acc_s, qb_s, inc_s, etb_s, tok0, *, seq):
    c = HG_CHUNK
    n_chunks = seq // c
    n_step = HG_CHUNKS_PER_STEP
    assert n_chunks % n_step == 0
    sub = lax.broadcasted_iota(jnp.int32, (c, V7X_LANES), 0) & 7

    def lower_bound(lb_ref):
        a = lb_ref[...].astype(F32)
        a0, a1 = a[0:1], a[1:2]
        m = jnp.maximum(a0, a1)
        e0, e1 = jnp.exp(a0 - m), jnp.exp(a1 - m)
        return e0 / (e0 + e1)

    lbf = lower_bound(lbf_ref)
    lbb = lower_bound(lbb_ref)

    def rows(ci, tok=0):
        return pl.ds(pl.multiple_of(tok + ci * c, c), c)

    def group_rows(x, r):
        return jnp.concatenate(
            [jnp.broadcast_to(x[8 * j + r:8 * j + r + 1], (8, V7X_LANES)) for j in range(c // 8)],
            axis=0)

    def repeat8(x8, n_rows):
        return x8 if n_rows == 8 else jnp.concatenate([x8] * (n_rows // 8), axis=0)

    def gates(f_ref, sl, lb):
        hf = f_ref[sl, :].astype(F32)
        f = lb + (1.0 - lb) * (1.0 / (1.0 + jnp.exp(-hf)))
        hi, lo = _split_bf16(jnp.log2(f))
        return f, 1.0 - f, jnp.concatenate([hi, lo], axis=0)

    def stage_gates(ci):
        src = rows(ci, tok0)
        st = dict(sl=rows(ci), q=q_ref[src, :].astype(F32), v=v_ref[src, :])
        st["f_f"], st["k_f"], lf_f = gates(ff_ref, src, lbf)
        st["f_b"], st["k_b"], lf_b = gates(fb_ref, src, lbb)
        st["b_f"] = _dot(cum_ref[0], lf_f)
        st["b_b"] = _dot(cum_ref[1], lf_b)
        return st

    def level_operands(st, lvl):
        q, k_f, k_b, b_f, b_b = st["q"], st["k_f"], st["k_b"], st["b_f"], st["b_b"]
        if lvl == HG_LEVELS:
            return q, k_f + k_b
        if lvl == 0:
            odd = (sub & 1) == 1
            return q * jnp.where(odd, st["f_f"], st["f_b"]), jnp.where(odd, k_b, k_f)
        if lvl < 3:
            later = (sub & (1 << lvl)) != 0
            if lvl == 1:
                ref_f = jnp.where(sub < 4, group_rows(b_f, 1), group_rows(b_f, 5))
                ref_b = jnp.where(sub < 4, group_rows(b_b, 2), group_rows(b_b, 6))
            else:
                ref_f, ref_b = group_rows(b_f, 3), group_rows(b_b, 4)
            d_f, d_b = b_f - ref_f, b_b - ref_b
            return (q * jnp.exp2(jnp.where(later, d_f, d_b)),
                    jnp.where(later, k_b, k_f) * jnp.exp2(-jnp.where(later, d_b, d_f)))
        s = 1 << lvl
        xp, yp = [], []
        for r0 in range(0, c, 2 * s):
            m = r0 + s
            rf = repeat8(st["last_f"][m - 8:m], s)
            rb = repeat8(st["first_b"][m:m + 8], s)
            early, late = slice(r0, m), slice(m, m + s)
            xp += [q[early] * jnp.exp2(b_b[early] - rb), q[late] * jnp.exp2(b_f[late] - rf)]
            yp += [k_f[early] * jnp.exp2(rf - b_f[early]), k_b[late] * jnp.exp2(rb - b_b[late])]
        return jnp.concatenate(xp, axis=0), jnp.concatenate(yp, axis=0)

    def stage_operands(ci, st):
        q, k_f, k_b, b_f, b_b = st["q"], st["k_f"], st["k_b"], st["b_f"], st["b_b"]
        last_f = st["last_f"] = group_rows(b_f, 7)
        first_b = st["first_b"] = group_rows(b_b, 0)
        sl = st["sl"]
        tot_f = last_f[c - 8:c]
        tot_b = first_b[0:8]
        st["qe_f"] = (q * jnp.exp2(b_f)).astype(BF16)
        st["kd_f"] = (k_f * jnp.exp2(repeat8(tot_f, c) - b_f)).astype(BF16)
        st["tot_f"] = jnp.exp2(tot_f[0:1])
        qb_s[sl, :] = (q * jnp.exp2(b_b)).astype(BF16)
        st["kd_b"] = (k_b * jnp.exp2(repeat8(tot_b, c) - b_b)).astype(BF16)
        etb_s[pl.ds(pl.multiple_of(ci * 8, 8), 8), :] = jnp.exp2(tot_b)
        return st

    def fwd_body(i, state_t):
        cis = [i * n_step + j for j in range(n_step)]
        sts = [stage_gates(ci) for ci in cis]
        sts = [stage_operands(ci, st) for ci, st in zip(cis, sts)]
        for st in sts:
            for lvl in range(HG_LEVELS + 1):
                x, y = level_operands(st, lvl)
                term = mask_ref[lvl] * _nt_dot(x.astype(BF16), y.astype(BF16))
                st["a"] = term if lvl == 0 else st["a"] + term
        intra = [_dot(st["a"].astype(BF16), st["v"]) for st in sts]
        incs = [_dot(st["v"].T, jnp.concatenate([st["kd_f"], st["kd_b"]], axis=1)) for st in sts]
        for st, o, inc in zip(sts, intra, incs):
            acc_s[st["sl"], :] = o + _nt_dot(st["qe_f"], state_t.astype(BF16))
            state_t = state_t * st["tot_f"] + inc[:, 0:HG_DK]
            inc_s[st["sl"], :] = inc[:, HG_DK:2 * HG_DK]
        return state_t

    zero = jnp.zeros((HG_DK, HG_DK), F32)
    lax.fori_loop(0, n_chunks // n_step, fwd_body, zero)

    n_out = HG_OUT_CHUNKS_PER_STEP
    assert n_chunks % n_out == 0

    def bwd_body(i, state_t):
        cis = [n_chunks - 1 - (i * n_out + j) for j in range(n_out)]
        inter = []
        for ci in cis:
            inter.append(_nt_dot(qb_s[rows(ci), :], state_t.astype(BF16)))
            tot_b = etb_s[pl.ds(pl.multiple_of(ci * 8, 8), 1), :]
            state_t = state_t * tot_b + inc_s[rows(ci), :]
        for ci, o_inter in zip(cis, inter):
            o_ref[rows(ci, tok0), :] = (acc_s[rows(ci), :] + o_inter).astype(o_ref.dtype)
        return state_t

    lax.fori_loop(0, n_chunks // n_out, bwd_body, zero)


def _hgrn2(proj, lb_fwd, lb_bwd, batch, seq):
    assert HG_DK == V7X_LANES and lb_fwd.shape[0] == 2
    cums, masks = _hg_tables()
    first = 3 * NA_WIDTH // V7X_LANES
    nb = HG_BATCH_PER_STEP
    assert batch % nb == 0
    col = lambda k: pl.BlockSpec((None, nb * seq, V7X_LANES),
                                 lambda b, h, k=k: (first + k * HG_HEADS + h, b, 0))
    lbs = pl.BlockSpec((2, V7X_LANES), lambda b, h: (0, h))
    const = lambda a: pl.BlockSpec(a.shape, lambda b, h, nd=a.ndim: (0,) * nd)
    return pl.pallas_call(
        functools.partial(_hgrn_kernel, n_batch=nb, seq=seq),
        grid=(batch // nb, HG_HEADS),
        in_specs=[col(0), col(1), col(2), col(3), lbs, lbs, const(cums), const(masks)],
        out_specs=pl.BlockSpec((None, nb * seq, V7X_LANES), lambda b, h: (h, b, 0)),
        out_shape=jax.ShapeDtypeStruct((HG_HEADS, batch * seq, V7X_LANES), BF16),
        scratch_shapes=[pltpu.VMEM((seq, HG_DK), F32), pltpu.VMEM((seq, HG_DK), BF16),
                        pltpu.VMEM((seq, HG_DK), F32),
                        pltpu.VMEM((8 * seq // HG_CHUNK, HG_DK), F32)],
        compiler_params=pltpu.CompilerParams(
            dimension_semantics=("arbitrary", "arbitrary"),
            vmem_limit_bytes=V7X_VMEM_LIMIT_BYTES),
        name="hgrn2",
    )(proj, proj, proj, proj, lb_fwd.astype(F32), lb_bwd.astype(F32), cums, masks)


def _out_mlp_kernel(x_ref, na_ref, hg_ref, g_ref, gw_ref, wo_na_ref, wo_hg_ref, nw_ref, wu_ref,
                    wd_ref, o_ref):
    def recurrence_head(j):
        o = hg_ref[j].astype(F32)
        ms = jnp.mean(o * o, axis=-1, keepdims=True)
        g = g_ref[j].astype(F32)
        return (o * lax.rsqrt(ms + RMS_EPS) * gw_ref[...]
                * (g * (1.0 / (1.0 + jnp.exp(-g))))).astype(BF16)

    y_na = jnp.concatenate([na_ref[j] for j in range(na_ref.shape[0])], axis=1)
    y_hg = jnp.concatenate([recurrence_head(j) for j in range(HG_HEADS)], axis=1)
    h = x_ref[...] + _dot(y_na, wo_na_ref[...]) + _dot(y_hg, wo_hg_ref[...])
    ms = jnp.mean(h * h, axis=-1, keepdims=True)
    u = (h * lax.rsqrt(ms + RMS_EPS) * nw_ref[...]).astype(BF16)
    mlp = None
    for j in range(D_FF // FF_CHUNK):
        cols = slice(j * FF_CHUNK, (j + 1) * FF_CHUNK)
        a = jnp.maximum(_dot(u, wu_ref[:, cols]), 0.0)
        part = _dot((a * a).astype(BF16), wd_ref[cols, :])
        mlp = part if mlp is None else mlp + part
    o_ref[...] = h + mlp


def _out_mlp(x2, y_na, o_hg, proj, hg_norm_w, w_out, norm_w, w_up, w_down):
    m = x2.shape[0]
    first_g = (3 * NA_WIDTH + 4 * HG_WIDTH) // V7X_LANES
    assert first_g % HG_HEADS == 0
    gate = pl.BlockSpec((HG_HEADS, TOKEN_TILE, V7X_LANES), lambda i: (first_g // HG_HEADS, i, 0))
    gw = hg_norm_w.astype(F32).reshape(1, HG_DK)
    tile = lambda width: pl.BlockSpec((TOKEN_TILE, width), lambda i: (i, 0))
    blocks = lambda a: pl.BlockSpec((a.shape[0], TOKEN_TILE, V7X_LANES), lambda i: (0, i, 0))
    resident = lambda a: pl.BlockSpec(a.shape, lambda i: (0, 0), pipeline_mode=pl.Buffered(1))
    wo_na, wo_hg = w_out[:NA_WIDTH], w_out[NA_WIDTH:]
    return pl.pallas_call(
        _out_mlp_kernel,
        grid=(m // TOKEN_TILE,),
        in_specs=[tile(D_MODEL), blocks(y_na), blocks(o_hg), gate,
                  pl.BlockSpec((1, HG_DK), lambda i: (0, 0)), resident(wo_na),
                  resident(wo_hg), pl.BlockSpec((1, D_MODEL), lambda i: (0, 0)),
                  resident(w_up), resident(w_down)],
        out_specs=tile(D_MODEL),
        out_shape=jax.ShapeDtypeStruct((m, D_MODEL), F32),
        compiler_params=pltpu.CompilerParams(
            dimension_semantics=("arbitrary",), vmem_limit_bytes=V7X_VMEM_LIMIT_BYTES),
        name="out_mlp",
    )(x2, y_na, o_hg, proj, gw, wo_na, wo_hg, norm_w, w_up, w_down)


def kernel(x, w_in, w_out, attn_norm_w, mlp_norm_w, q_norm_w, k_norm_w, rpb, hg_norm_w,
           lb_fwd, lb_bwd, w_up, w_down):
    batch, seq, d = x.shape
    assert d == D_MODEL and w_in.shape[0] == 1 and seq % GRID_W == 0 and seq % HG_CHUNK == 0
    x2 = x.reshape(batch * seq, d)
    proj = _in_proj(x2, attn_norm_w[0].astype(F32).reshape(1, d), w_in[0].astype(BF16),
                    q_norm_w[0], k_norm_w[0])
    y_na = _natten(proj, rpb[0], batch, seq)
    o_hg = _hgrn2(proj, lb_fwd, lb_bwd, batch, seq)
    out = _out_mlp(x2, y_na, o_hg, proj, hg_norm_w[0], w_out[0].astype(BF16),
                   mlp_norm_w[0].astype(F32).reshape(1, d),
                   w_up[0].astype(BF16), w_down[0].astype(BF16))
    return out.reshape(batch, seq, d)
```

```python
import functools

import numpy as np
import jax
import jax.numpy as jnp
from jax import lax
from jax.experimental import pallas as pl
from jax.experimental.pallas import tpu as pltpu

F32 = jnp.float32
BF16 = jnp.bfloat16

D_MODEL = 1024
GRID_W = 64
NA_HEADS = 8
NA_HEAD_DIM = 64
NA_WIDTH = NA_HEADS * NA_HEAD_DIM
NA_WIN_R = 8
NA_WIN_C = 16
HG_HEADS = 4
HG_DK = 128
HG_WIDTH = HG_HEADS * HG_DK
D_FF = 4 * D_MODEL
D_IN_PROJ = 3 * NA_WIDTH + 5 * HG_WIDTH
RMS_EPS = 1e-6

V7X_LANES = 128
V7X_SUBLANES = 8
V7X_VMEM_LIMIT_BYTES = 56 * 1024 * 1024

TOKEN_TILE = 1024
FF_CHUNK = 1024
HG_CHUNK = 128
HG_LEVELS = 7
HG_CHUNKS_PER_STEP = 8
HG_OUT_CHUNKS_PER_STEP = 16
NA_QT = 16
NA_KT = 32
NA_TILES = GRID_W // NA_QT
NA_GROUP_ROWS = 4
NA_BAND_ROWS = 12
NA_GROUPS_PER_STEP = 2
NA_BATCH_PER_STEP = 2
HG_BATCH_PER_STEP = 2
MASK_NEG = -1e30

NA_EDGE = NA_WIN_C // 2
_NA_QCOLS = [list(range(0, NA_EDGE)) + list(range(GRID_W - NA_EDGE, GRID_W))] + [
    list(range(NA_EDGE + NA_QT * t, NA_EDGE + NA_QT * (t + 1))) for t in range(NA_TILES - 1)]
_NA_KSEGS = [[(0, NA_WIN_C), (GRID_W - NA_WIN_C, NA_WIN_C)]] + [
    [(NA_QT * t, NA_KT)] for t in range(NA_TILES - 1)]


def _nt_dot(a, b):
    return lax.dot_general(a, b, (((1,), (1,)), ((), ())), preferred_element_type=F32)


def _dot(a, b):
    return jnp.dot(a, b, preferred_element_type=F32)


def _split_bf16(x):
    hi = x.astype(BF16)
    lo = (x - hi.astype(F32)).astype(BF16)
    return hi, lo


def _in_proj_kernel(x_ref, nw_ref, w_ref, qkw_ref, o_ref):
    x = x_ref[...]
    ms = jnp.mean(x * x, axis=-1, keepdims=True)
    u = (x * lax.rsqrt(ms + RMS_EPS) * nw_ref[...]).astype(BF16)
    res = _dot(u, w_ref[...])
    first_head = lax.broadcasted_iota(jnp.int32, (1, V7X_LANES), 1) < NA_HEAD_DIM
    for j in range(D_IN_PROJ // V7X_LANES):
        blk = res[:, j * V7X_LANES:(j + 1) * V7X_LANES]
        if j < qkw_ref.shape[0]:
            sq = blk * blk
            s_all = jnp.sum(sq, axis=-1, keepdims=True)
            s_one = jnp.sum(jnp.where(first_head, sq, 0.0), axis=-1, keepdims=True)
            ms_h = jnp.where(first_head, s_one, s_all - s_one) * (1.0 / NA_HEAD_DIM)
            blk = blk * lax.rsqrt(ms_h + RMS_EPS) * qkw_ref[j]
        o_ref[j] = blk.astype(o_ref.dtype)


def _in_proj(x2, norm_w, w_in, q_norm_w, k_norm_w):
    m = x2.shape[0]
    pair_w = lambda w, mult: jnp.tile(w.astype(F32) * mult, 2).reshape(1, 1, V7X_LANES)
    n_blk = NA_WIDTH // V7X_LANES
    qkw = jnp.concatenate([jnp.tile(pair_w(q_norm_w, NA_HEAD_DIM ** -0.5), (n_blk, 1, 1)),
                           jnp.tile(pair_w(k_norm_w, 1.0), (n_blk, 1, 1))])
    return pl.pallas_call(
        _in_proj_kernel,
        grid=(m // TOKEN_TILE,),
        in_specs=[
            pl.BlockSpec((TOKEN_TILE, D_MODEL), lambda i: (i, 0)),
            pl.BlockSpec((1, D_MODEL), lambda i: (0, 0)),
            pl.BlockSpec((D_MODEL, D_IN_PROJ), lambda i: (0, 0), pipeline_mode=pl.Buffered(1)),
            pl.BlockSpec(qkw.shape, lambda i: (0, 0, 0)),
        ],
        out_specs=pl.BlockSpec((D_IN_PROJ // V7X_LANES, TOKEN_TILE, V7X_LANES), lambda i: (0, i, 0)),
        out_shape=jax.ShapeDtypeStruct((D_IN_PROJ // V7X_LANES, m, V7X_LANES), BF16),
        compiler_params=pltpu.CompilerParams(
            dimension_semantics=("arbitrary",), vmem_limit_bytes=V7X_VMEM_LIMIT_BYTES),
        name="in_proj",
    )(x2, norm_w, w_in, qkw)


def _na_group_plan(rows):
    wr = min(NA_WIN_R, rows)
    bases, types, sigs = [], [], []
    for g in range(rows // NA_GROUP_ROWS):
        r0 = g * NA_GROUP_ROWS
        starts = [min(max(r0 + j - wr // 2, 0), rows - wr) for j in range(NA_GROUP_ROWS)]
        base = min(max(starts[0], 0), rows - NA_BAND_ROWS)
        assert base <= starts[0] and starts[-1] + wr <= base + NA_BAND_ROWS
        sig = tuple((starts[j] - base, r0 + j - base) for j in range(NA_GROUP_ROWS))
        if sig not in sigs:
            sigs.append(sig)
        bases.append(base)
        types.append(sigs.index(sig))
    return bases, types, sigs


def _na_bias_table(rpb, sigs):
    n_ri, n_ci = 2 * NA_WIN_R - 1, 2 * NA_WIN_C - 1
    csel = np.zeros((n_ci, NA_TILES, NA_QT, NA_KT), np.float32)
    for t in range(NA_TILES):
        qc = np.array(_NA_QCOLS[t])
        kc = np.concatenate([np.arange(c0, c0 + n) for c0, n in _NA_KSEGS[t]])
        cs = np.clip(qc - NA_WIN_C // 2, 0, GRID_W - NA_WIN_C)
        valid = (kc[None, :] >= cs[:, None]) & (kc[None, :] < cs[:, None] + NA_WIN_C)
        cidx = kc[None, :] - qc[:, None] + NA_WIN_C - 1
        qi, ki = np.nonzero(valid)
        csel[cidx[qi, ki], t, qi, ki] = 1.0
    cols = jnp.einsum("hrc,ctqk->htqrk", rpb.astype(F32), jnp.asarray(csel),
                      precision=lax.Precision.HIGHEST)
    cols = cols + jnp.asarray(np.where(csel.sum(0) > 0, 0.0, MASK_NEG)[None, :, :, None, :], F32)
    cols = cols.reshape(NA_HEADS // 2, 2, NA_TILES, NA_QT, n_ri * NA_KT)
    cols = jnp.transpose(cols, (0, 2, 1, 3, 4))
    tabs = []
    for sig in sigs:
        per_row = []
        for first, qrow in sig:
            ri0 = first - qrow + NA_WIN_R - 1
            after = NA_BAND_ROWS - first - NA_WIN_R
            window = cols[..., ri0 * NA_KT:(ri0 + NA_WIN_R) * NA_KT]
            per_row.append(jnp.pad(window, [(0, 0)] * 4 + [(first * NA_KT, after * NA_KT)],
                                   constant_values=MASK_NEG))
        tabs.append(jnp.stack(per_row, axis=2))
    return jnp.stack(tabs).reshape(len(sigs), NA_HEADS // 2, NA_TILES,
                                   NA_GROUP_ROWS * 2 * NA_QT, NA_BAND_ROWS * NA_KT)


def _natten_kernel(*refs, n_batch, rows, bases, types):
    def one(bb, carry):
        _natten_one(*refs, bb * rows * GRID_W, rows=rows, bases=bases, types=types)
        return carry

    lax.fori_loop(0, n_batch, one, 0)


def _natten_one(q_ref, k_ref, v_ref, bias_ref, o_ref, tok0, *, rows, bases, types):
    lane = lax.broadcasted_iota(jnp.int32, (NA_QT, V7X_LANES), 1)
    first_head = lane < NA_HEAD_DIM
    gr = NA_GROUP_ROWS

    def row_slice(grid_row):
        return pl.ds(pl.multiple_of(tok0 + grid_row * GRID_W, GRID_W), GRID_W)

    def row_block(ref, grid_row):
        return ref[row_slice(grid_row), :]

    def band(blocks, t):
        return jnp.concatenate(
            [blk[c0:c0 + n] for blk in blocks for (c0, n) in _NA_KSEGS[t]], axis=0)

    def step(groups):
        scores = []
        for row0, base, y in groups:
            kblocks = [row_block(k_ref, base + u) for u in range(NA_BAND_ROWS)]
            qrows = [row_block(q_ref, row0 + j).astype(F32) for j in range(gr)]
            for t in range(NA_TILES):
                qc = _NA_QCOLS[t]
                q2 = []
                for j in range(gr):
                    if t == 0:
                        qt = jnp.concatenate([qrows[j][0:NA_EDGE], qrows[j][GRID_W - NA_EDGE:]], axis=0)
                    else:
                        qt = qrows[j][qc[0]:qc[0] + NA_QT]
                    q2 += [jnp.where(first_head, qt, 0.0), jnp.where(first_head, 0.0, qt)]
                s = _nt_dot(jnp.concatenate(q2, axis=0).astype(BF16), band(kblocks, t))
                scores.append(s + bias_ref[y, t])
        probs, sums = [], []
        for s in scores:
            p = jnp.exp(s - jnp.max(s, axis=-1, keepdims=True))
            sums.append(jnp.sum(p, axis=-1, keepdims=True))
            probs.append(p.astype(BF16))
        for g, (row0, base, y) in enumerate(groups):
            vblocks = [row_block(v_ref, base + u) for u in range(NA_BAND_ROWS)]
            o_tiles = []
            for t in range(NA_TILES):
                o2 = _dot(probs[g * NA_TILES + t], band(vblocks, t)) / sums[g * NA_TILES + t]
                o_tiles.append(o2)
            for j in range(gr):
                sel = [jnp.where(first_head, o[2 * NA_QT * j:2 * NA_QT * j + NA_QT],
                                 o[2 * NA_QT * j + NA_QT:2 * NA_QT * (j + 1)]) for o in o_tiles]
                orow = jnp.concatenate([sel[0][0:NA_EDGE], *sel[1:], sel[0][NA_EDGE:]], axis=0)
                o_ref[row_slice(row0 + j), :] = orow.astype(o_ref.dtype)

    n_groups = rows // gr
    shift = min(NA_WIN_R, rows) // 2
    regular = [bases[g] == g * gr - shift for g in range(n_groups)]
    lo = regular.index(True)
    hi = n_groups - regular[::-1].index(True)
    per = NA_GROUPS_PER_STEP
    assert all(regular[lo:hi]) and (hi - lo) % per == 0
    assert len({types[g] for g in range(lo, hi)}) == 1
    edge = [(g * gr, bases[g], types[g]) for g in list(range(lo)) + list(range(hi, n_groups))]
    for i in range(0, len(edge), per):
        step(edge[i:i + per])

    def regular_body(i, carry):
        g0 = lo + i * per
        step([((g0 + n) * gr, (g0 + n) * gr - shift, types[lo]) for n in range(per)])
        return carry

    lax.fori_loop(0, (hi - lo) // per, regular_body, 0)


def _natten(proj, rpb, batch, seq):
    rows = seq // GRID_W
    assert rows >= NA_BAND_ROWS and rows % NA_GROUP_ROWS == 0
    assert GRID_W == 64 and NA_HEAD_DIM * 2 == V7X_LANES
    pairs = NA_HEADS // 2
    bases, types, sigs = _na_group_plan(rows)
    bias = _na_bias_table(rpb, sigs)
    nb = NA_BATCH_PER_STEP
    assert batch % nb == 0
    col = lambda off: pl.BlockSpec((None, nb * seq, V7X_LANES),
                                   lambda p, b, off=off: (off + p, b, 0))
    return pl.pallas_call(
        functools.partial(_natten_kernel, n_batch=nb, rows=rows, bases=tuple(bases),
                          types=tuple(types)),
        grid=(pairs, batch // nb),
        in_specs=[
            col(0), col(pairs), col(2 * pairs),
            pl.BlockSpec((len(sigs), None, NA_TILES, NA_GROUP_ROWS * 2 * NA_QT,
                          NA_BAND_ROWS * NA_KT), lambda p, b: (0, p, 0, 0, 0)),
        ],
        out_specs=pl.BlockSpec((None, nb * seq, V7X_LANES), lambda p, b: (p, b, 0)),
        out_shape=jax.ShapeDtypeStruct((pairs, batch * seq, V7X_LANES), BF16),
        compiler_params=pltpu.CompilerParams(
            dimension_semantics=("arbitrary", "arbitrary"),
            vmem_limit_bytes=V7X_VMEM_LIMIT_BYTES),
        name="natten",
    )(proj, proj, proj, bias)


def _hg_tables():
    c = HG_CHUNK
    idx = np.arange(c)
    lower = (idx[None, :] <= idx[:, None]).astype(np.float32)
    cums = np.stack([np.concatenate([lower, lower], axis=1),
                     np.concatenate([lower.T, lower.T], axis=1)])
    masks = []
    for lvl in range(HG_LEVELS):
        s = 1 << lvl
        same_block = (idx[:, None] // (2 * s)) == (idx[None, :] // (2 * s))
        half = (idx // s) % 2
        masks.append(same_block & (half[:, None] != half[None, :]))
    masks.append(np.eye(c, dtype=bool))
    return jnp.asarray(cums, BF16), jnp.asarray(np.stack(masks), F32)


def _hgrn_kernel(*refs, n_batch, seq):
    def one(bb, carry):
        _hgrn_one(*refs, bb * seq, seq=seq)
        return carry

    lax.fori_loop(0, n_batch, one, 0)


def _hgrn_one(q_ref, ff_ref, fb_ref, v_ref, lbf_ref, lbb_ref,
              cum_ref, mask_ref, o_ref, acc_s, qb_s, inc_s, etb_s, tok0, *, seq):
    c = HG_CHUNK
    n_chunks = seq // c
    n_step = HG_CHUNKS_PER_STEP
    assert n_chunks % n_step == 0
    grp = V7X_SUBLANES
    sub = lax.broadcasted_iota(jnp.int32, (c, V7X_LANES), 0) & (grp - 1)

    def lower_bound(lb_ref):
        a = lb_ref[...].astype(F32)
        a0, a1 = a[0:1], a[1:2]
        m = jnp.maximum(a0, a1)
        e0, e1 = jnp.exp(a0 - m), jnp.exp(a1 - m)
        return e0 / (e0 + e1)

    lbf = lower_bound(lbf_ref)
    lbb = lower_bound(lbb_ref)

    def rows(ci, tok=0):
        return pl.ds(pl.multiple_of(tok + ci * c, c), c)

    def group_rows(x, r):
        return jnp.concatenate(
            [jnp.broadcast_to(x[grp * j + r:grp * j + r + 1], (grp, V7X_LANES))
             for j in range(c // grp)], axis=0)

    def repeat_group(xg, n_rows):
        return xg if n_rows == grp else jnp.concatenate([xg] * (n_rows // grp), axis=0)

    def boundary_rows(x, s, offset):
        parts = [group_rows(x, 2 * s * blk + s - 1 + offset) for blk in range(grp // (2 * s))]
        assert len(parts) <= 2
        return parts[0] if len(parts) == 1 else jnp.where(sub < grp // 2, parts[0], parts[1])

    def gates(f_ref, sl, lb):
        hf = f_ref[sl, :].astype(F32)
        f = lb + (1.0 - lb) * (1.0 / (1.0 + jnp.exp(-hf)))
        hi, lo = _split_bf16(jnp.log2(f))
        return f, 1.0 - f, jnp.concatenate([hi, lo], axis=0)

    def stage_gates(ci):
        src = rows(ci, tok0)
        st = dict(sl=rows(ci), q=q_ref[src, :].astype(F32), v=v_ref[src, :])
        st["f_f"], st["k_f"], lf_f = gates(ff_ref, src, lbf)
        st["f_b"], st["k_b"], lf_b = gates(fb_ref, src, lbb)
        st["b_f"] = _dot(cum_ref[0], lf_f)
        st["b_b"] = _dot(cum_ref[1], lf_b)
        return st

    def level_operands(st, lvl):
        q, k_f, k_b, b_f, b_b = st["q"], st["k_f"], st["k_b"], st["b_f"], st["b_b"]
        if lvl == HG_LEVELS:
            return q, k_f + k_b
        if lvl == 0:
            odd = (sub & 1) == 1
            return q * jnp.where(odd, st["f_f"], st["f_b"]), jnp.where(odd, k_b, k_f)
        if (1 << lvl) < grp:
            later = (sub & (1 << lvl)) != 0
            d_f = b_f - boundary_rows(b_f, 1 << lvl, 0)
            d_b = b_b - boundary_rows(b_b, 1 << lvl, 1)
            return (q * jnp.exp2(jnp.where(later, d_f, d_b)),
                    jnp.where(later, k_b, k_f) * jnp.exp2(-jnp.where(later, d_b, d_f)))
        s = 1 << lvl
        xp, yp = [], []
        for r0 in range(0, c, 2 * s):
            m = r0 + s
            rf = repeat_group(st["last_f"][m - grp:m], s)
            rb = repeat_group(st["first_b"][m:m + grp], s)
            early, late = slice(r0, m), slice(m, m + s)
            xp += [q[early] * jnp.exp2(b_b[early] - rb), q[late] * jnp.exp2(b_f[late] - rf)]
            yp += [k_f[early] * jnp.exp2(rf - b_f[early]), k_b[late] * jnp.exp2(rb - b_b[late])]
        return jnp.concatenate(xp, axis=0), jnp.concatenate(yp, axis=0)

    def stage_operands(ci, st):
        q, k_f, k_b, b_f, b_b = st["q"], st["k_f"], st["k_b"], st["b_f"], st["b_b"]
        last_f = st["last_f"] = group_rows(b_f, grp - 1)
        first_b = st["first_b"] = group_rows(b_b, 0)
        sl = st["sl"]
        tot_f = last_f[c - grp:c]
        tot_b = first_b[0:grp]
        st["qe_f"] = (q * jnp.exp2(b_f)).astype(BF16)
        st["kd_f"] = (k_f * jnp.exp2(repeat_group(tot_f, c) - b_f)).astype(BF16)
        st["tot_f"] = jnp.exp2(tot_f[0:1])
        qb_s[sl, :] = (q * jnp.exp2(b_b)).astype(BF16)
        st["kd_b"] = (k_b * jnp.exp2(repeat_group(tot_b, c) - b_b)).astype(BF16)
        etb_s[pl.ds(pl.multiple_of(ci * grp, grp), grp), :] = jnp.exp2(tot_b)
        return st

    def fwd_body(i, state_t):
        cis = [i * n_step + j for j in range(n_step)]
        sts = [stage_gates(ci) for ci in cis]
        sts = [stage_operands(ci, st) for ci, st in zip(cis, sts)]
        for st in sts:
            for lvl in range(HG_LEVELS + 1):
                x, y = level_operands(st, lvl)
                term = mask_ref[lvl] * _nt_dot(x.astype(BF16), y.astype(BF16))
                st["a"] = term if lvl == 0 else st["a"] + term
        intra = [_dot(st["a"].astype(BF16), st["v"]) for st in sts]
        incs = [_dot(st["v"].T, jnp.concatenate([st["kd_f"], st["kd_b"]], axis=1)) for st in sts]
        for st, o, inc in zip(sts, intra, incs):
            acc_s[st["sl"], :] = o + _nt_dot(st["qe_f"], state_t.astype(BF16))
            state_t = state_t * st["tot_f"] + inc[:, 0:HG_DK]
            inc_s[st["sl"], :] = inc[:, HG_DK:2 * HG_DK]
        return state_t

    zero = jnp.zeros((HG_DK, HG_DK), F32)
    lax.fori_loop(0, n_chunks // n_step, fwd_body, zero)

    n_out = HG_OUT_CHUNKS_PER_STEP
    assert n_chunks % n_out == 0

    def bwd_body(i, state_t):
        cis = [n_chunks - 1 - (i * n_out + j) for j in range(n_out)]
        inter = []
        for ci in cis:
            inter.append(_nt_dot(qb_s[rows(ci), :], state_t.astype(BF16)))
            tot_b = etb_s[pl.ds(pl.multiple_of(ci * grp, grp), 1), :]
            state_t = state_t * tot_b + inc_s[rows(ci), :]
        for ci, o_inter in zip(cis, inter):
            o_ref[rows(ci, tok0), :] = (acc_s[rows(ci), :] + o_inter).astype(o_ref.dtype)
        return state_t

    lax.fori_loop(0, n_chunks // n_out, bwd_body, zero)


def _hgrn2(proj, lb_fwd, lb_bwd, batch, seq):
    assert HG_DK == V7X_LANES and lb_fwd.shape[0] == 2
    cums, masks = _hg_tables()
    first = 3 * NA_WIDTH // V7X_LANES
    nb = HG_BATCH_PER_STEP
    assert batch % nb == 0
    col = lambda k: pl.BlockSpec((None, nb * seq, V7X_LANES),
                                 lambda b, h, k=k: (first + k * HG_HEADS + h, b, 0))
    lbs = pl.BlockSpec((2, V7X_LANES), lambda b, h: (0, h))
    const = lambda a: pl.BlockSpec(a.shape, lambda b, h, nd=a.ndim: (0,) * nd)
    return pl.pallas_call(
        functools.partial(_hgrn_kernel, n_batch=nb, seq=seq),
        grid=(batch // nb, HG_HEADS),
        in_specs=[col(0), col(1), col(2), col(3), lbs, lbs, const(cums), const(masks)],
        out_specs=pl.BlockSpec((None, nb * seq, V7X_LANES), lambda b, h: (h, b, 0)),
        out_shape=jax.ShapeDtypeStruct((HG_HEADS, batch * seq, V7X_LANES), BF16),
        scratch_shapes=[pltpu.VMEM((seq, HG_DK), F32), pltpu.VMEM((seq, HG_DK), BF16),
                        pltpu.VMEM((seq, HG_DK), F32),
                        pltpu.VMEM((V7X_SUBLANES * seq // HG_CHUNK, HG_DK), F32)],
        compiler_params=pltpu.CompilerParams(
            dimension_semantics=("arbitrary", "arbitrary"),
            vmem_limit_bytes=V7X_VMEM_LIMIT_BYTES),
        name="hgrn2",
    )(proj, proj, proj, proj, lb_fwd.astype(F32), lb_bwd.astype(F32), cums, masks)


def _out_mlp_kernel(x_ref, na_ref, hg_ref, g_ref, gw_ref, wo_na_ref, wo_hg_ref, nw_ref, wu_ref,
                    wd_ref, o_ref):
    def recurrence_head(j):
        o = hg_ref[j].astype(F32)
        ms = jnp.mean(o * o, axis=-1, keepdims=True)
        g = g_ref[j].astype(F32)
        return (o * lax.rsqrt(ms + RMS_EPS) * gw_ref[...]
                * (g * (1.0 / (1.0 + jnp.exp(-g))))).astype(BF16)

    y_na = jnp.concatenate([na_ref[j] for j in range(na_ref.shape[0])], axis=1)
    y_hg = jnp.concatenate([recurrence_head(j) for j in range(HG_HEADS)], axis=1)
    h = x_ref[...] + _dot(y_na, wo_na_ref[...]) + _dot(y_hg, wo_hg_ref[...])
    ms = jnp.mean(h * h, axis=-1, keepdims=True)
    u = (h * lax.rsqrt(ms + RMS_EPS) * nw_ref[...]).astype(BF16)
    mlp = None
    for j in range(D_FF // FF_CHUNK):
        cols = slice(j * FF_CHUNK, (j + 1) * FF_CHUNK)
        a = jnp.maximum(_dot(u, wu_ref[:, cols]), 0.0)
        part = _dot((a * a).astype(BF16), wd_ref[cols, :])
        mlp = part if mlp is None else mlp + part
    o_ref[...] = h + mlp


def _out_mlp(x2, y_na, o_hg, proj, hg_norm_w, w_out, norm_w, w_up, w_down):
    m = x2.shape[0]
    first_g = (3 * NA_WIDTH + 4 * HG_WIDTH) // V7X_LANES
    assert first_g % HG_HEADS == 0
    gate = pl.BlockSpec((HG_HEADS, TOKEN_TILE, V7X_LANES), lambda i: (first_g // HG_HEADS, i, 0))
    gw = hg_norm_w.astype(F32).reshape(1, HG_DK)
    tile = lambda width: pl.BlockSpec((TOKEN_TILE, width), lambda i: (i, 0))
    blocks = lambda a: pl.BlockSpec((a.shape[0], TOKEN_TILE, V7X_LANES), lambda i: (0, i, 0))
    resident = lambda a: pl.BlockSpec(a.shape, lambda i: (0, 0), pipeline_mode=pl.Buffered(1))
    wo_na, wo_hg = w_out[:NA_WIDTH], w_out[NA_WIDTH:]
    return pl.pallas_call(
        _out_mlp_kernel,
        grid=(m // TOKEN_TILE,),
        in_specs=[tile(D_MODEL), blocks(y_na), blocks(o_hg), gate,
                  pl.BlockSpec((1, HG_DK), lambda i: (0, 0)), resident(wo_na),
                  resident(wo_hg), pl.BlockSpec((1, D_MODEL), lambda i: (0, 0)),
                  resident(w_up), resident(w_down)],
        out_specs=tile(D_MODEL),
        out_shape=jax.ShapeDtypeStruct((m, D_MODEL), F32),
        compiler_params=pltpu.CompilerParams(
            dimension_semantics=("arbitrary",), vmem_limit_bytes=V7X_VMEM_LIMIT_BYTES),
        name="out_mlp",
    )(x2, y_na, o_hg, proj, gw, wo_na, wo_hg, norm_w, w_up, w_down)


def kernel(x, w_in, w_out, attn_norm_w, mlp_norm_w, q_norm_w, k_norm_w, rpb, hg_norm_w,
           lb_fwd, lb_bwd, w_up, w_down):
    batch, seq, d = x.shape
    assert d == D_MODEL and w_in.shape[0] == 1 and seq % GRID_W == 0 and seq % HG_CHUNK == 0
    x2 = x.reshape(batch * seq, d)
    proj = _in_proj(x2, attn_norm_w[0].astype(F32).reshape(1, d), w_in[0].astype(BF16),
                    q_norm_w[0], k_norm_w[0])
    y_na = _natten(proj, rpb[0], batch, seq)
    o_hg = _hgrn2(proj, lb_fwd, lb_bwd, batch, seq)
    out = _out_mlp(x2, y_na, o_hg, proj, hg_norm_w[0], w_out[0].astype(BF16),
                   mlp_norm_w[0].astype(F32).reshape(1, d),
                   w_up[0].astype(BF16), w_down[0].astype(BF16))
    return out.reshape(batch, seq, d)
```

```python
import functools

import numpy as np
import jax
import jax.numpy as jnp
from jax import lax
from jax.experimental import pallas as pl
from jax.experimental.pallas import tpu as pltpu

F32 = jnp.float32
BF16 = jnp.bfloat16

D_MODEL = 1024
GRID_W = 64
NA_HEADS = 8
NA_HEAD_DIM = 64
NA_WIDTH = NA_HEADS * NA_HEAD_DIM
NA_WIN_R = 8
NA_WIN_C = 16
HG_HEADS = 4
HG_DK = 128
HG_WIDTH = HG_HEADS * HG_DK
D_FF = 4 * D_MODEL
D_IN_PROJ = 3 * NA_WIDTH + 5 * HG_WIDTH
RMS_EPS = 1e-6

V7X_LANES = 128
V7X_SUBLANES = 8
V7X_VMEM_LIMIT_BYTES = 56 * 1024 * 1024

TOKEN_TILE = 1024
FF_CHUNK = 1024
HG_CHUNK = 128
HG_LEVELS = 7
HG_CHUNKS_PER_STEP = 16
HG_OUT_CHUNKS_PER_STEP = 16
NA_QT = 16
NA_KT = 32
NA_TILES = GRID_W // NA_QT
NA_GROUP_ROWS = 4
NA_BAND_ROWS = 12
NA_GROUPS_PER_STEP = 2
NA_BATCH_PER_STEP = 2
HG_BATCH_PER_STEP = 2
MASK_NEG = -1e30

NA_EDGE = NA_WIN_C // 2
_NA_QCOLS = [list(range(0, NA_EDGE)) + list(range(GRID_W - NA_EDGE, GRID_W))] + [
    list(range(NA_EDGE + NA_QT * t, NA_EDGE + NA_QT * (t + 1))) for t in range(NA_TILES - 1)]
_NA_KSEGS = [[(0, NA_WIN_C), (GRID_W - NA_WIN_C, NA_WIN_C)]] + [
    [(NA_QT * t, NA_KT)] for t in range(NA_TILES - 1)]


def _nt_dot(a, b):
    return lax.dot_general(a, b, (((1,), (1,)), ((), ())), preferred_element_type=F32)


def _dot(a, b):
    return jnp.dot(a, b, preferred_element_type=F32)


def _split_bf16(x):
    hi = x.astype(BF16)
    lo = (x - hi.astype(F32)).astype(BF16)
    return hi, lo


def _in_proj_kernel(x_ref, nw_ref, w_ref, qkw_ref, o_ref):
    x = x_ref[...]
    ms = jnp.mean(x * x, axis=-1, keepdims=True)
    u = (x * lax.rsqrt(ms + RMS_EPS) * nw_ref[...]).astype(BF16)
    res = _dot(u, w_ref[...])
    first_head = lax.broadcasted_iota(jnp.int32, (1, V7X_LANES), 1) < NA_HEAD_DIM
    for j in range(D_IN_PROJ // V7X_LANES):
        blk = res[:, j * V7X_LANES:(j + 1) * V7X_LANES]
        if j < qkw_ref.shape[0]:
            sq = blk * blk
            s_all = jnp.sum(sq, axis=-1, keepdims=True)
            s_one = jnp.sum(jnp.where(first_head, sq, 0.0), axis=-1, keepdims=True)
            ms_h = jnp.where(first_head, s_one, s_all - s_one) * (1.0 / NA_HEAD_DIM)
            blk = blk * lax.rsqrt(ms_h + RMS_EPS) * qkw_ref[j]
        o_ref[j] = blk.astype(o_ref.dtype)


def _in_proj(x2, norm_w, w_in, q_norm_w, k_norm_w):
    m = x2.shape[0]
    pair_w = lambda w, mult: jnp.tile(w.astype(F32) * mult, 2).reshape(1, 1, V7X_LANES)
    n_blk = NA_WIDTH // V7X_LANES
    qkw = jnp.concatenate([jnp.tile(pair_w(q_norm_w, NA_HEAD_DIM ** -0.5), (n_blk, 1, 1)),
                           jnp.tile(pair_w(k_norm_w, 1.0), (n_blk, 1, 1))])
    return pl.pallas_call(
        _in_proj_kernel,
        grid=(m // TOKEN_TILE,),
        in_specs=[
            pl.BlockSpec((TOKEN_TILE, D_MODEL), lambda i: (i, 0)),
            pl.BlockSpec((1, D_MODEL), lambda i: (0, 0)),
            pl.BlockSpec((D_MODEL, D_IN_PROJ), lambda i: (0, 0), pipeline_mode=pl.Buffered(1)),
            pl.BlockSpec(qkw.shape, lambda i: (0, 0, 0)),
        ],
        out_specs=pl.BlockSpec((D_IN_PROJ // V7X_LANES, TOKEN_TILE, V7X_LANES), lambda i: (0, i, 0)),
        out_shape=jax.ShapeDtypeStruct((D_IN_PROJ // V7X_LANES, m, V7X_LANES), BF16),
        compiler_params=pltpu.CompilerParams(
            dimension_semantics=("arbitrary",), vmem_limit_bytes=V7X_VMEM_LIMIT_BYTES),
        name="in_proj",
    )(x2, norm_w, w_in, qkw)


def _na_group_plan(rows):
    wr = min(NA_WIN_R, rows)
    bases, types, sigs = [], [], []
    for g in range(rows // NA_GROUP_ROWS):
        r0 = g * NA_GROUP_ROWS
        starts = [min(max(r0 + j - wr // 2, 0), rows - wr) for j in range(NA_GROUP_ROWS)]
        base = min(max(starts[0], 0), rows - NA_BAND_ROWS)
        assert base <= starts[0] and starts[-1] + wr <= base + NA_BAND_ROWS
        sig = tuple((starts[j] - base, r0 + j - base) for j in range(NA_GROUP_ROWS))
        if sig not in sigs:
            sigs.append(sig)
        bases.append(base)
        types.append(sigs.index(sig))
    return bases, types, sigs


def _na_bias_table(rpb, sigs):
    n_ri, n_ci = 2 * NA_WIN_R - 1, 2 * NA_WIN_C - 1
    csel = np.zeros((n_ci, NA_TILES, NA_QT, NA_KT), np.float32)
    for t in range(NA_TILES):
        qc = np.array(_NA_QCOLS[t])
        kc = np.concatenate([np.arange(c0, c0 + n) for c0, n in _NA_KSEGS[t]])
        cs = np.clip(qc - NA_WIN_C // 2, 0, GRID_W - NA_WIN_C)
        valid = (kc[None, :] >= cs[:, None]) & (kc[None, :] < cs[:, None] + NA_WIN_C)
        cidx = kc[None, :] - qc[:, None] + NA_WIN_C - 1
        qi, ki = np.nonzero(valid)
        csel[cidx[qi, ki], t, qi, ki] = 1.0
    cols = jnp.einsum("hrc,ctqk->htqrk", rpb.astype(F32), jnp.asarray(csel),
                      precision=lax.Precision.HIGHEST)
    cols = cols + jnp.asarray(np.where(csel.sum(0) > 0, 0.0, MASK_NEG)[None, :, :, None, :], F32)
    cols = cols.reshape(NA_HEADS // 2, 2, NA_TILES, NA_QT, n_ri * NA_KT)
    cols = jnp.transpose(cols, (0, 2, 1, 3, 4))
    tabs = []
    for sig in sigs:
        per_row = []
        for first, qrow in sig:
            ri0 = first - qrow + NA_WIN_R - 1
            after = NA_BAND_ROWS - first - NA_WIN_R
            window = cols[..., ri0 * NA_KT:(ri0 + NA_WIN_R) * NA_KT]
            per_row.append(jnp.pad(window, [(0, 0)] * 4 + [(first * NA_KT, after * NA_KT)],
                                   constant_values=MASK_NEG))
        tabs.append(jnp.stack(per_row, axis=2))
    return jnp.stack(tabs).reshape(len(sigs), NA_HEADS // 2, NA_TILES,
                                   NA_GROUP_ROWS * 2 * NA_QT, NA_BAND_ROWS * NA_KT)


def _natten_kernel(*refs, n_batch, rows, bases, types):
    def one(bb, carry):
        _natten_one(*refs, bb * rows * GRID_W, rows=rows, bases=bases, types=types)
        return carry

    lax.fori_loop(0, n_batch, one, 0)


def _natten_one(q_ref, k_ref, v_ref, bias_ref, o_ref, tok0, *, rows, bases, types):
    lane = lax.broadcasted_iota(jnp.int32, (NA_QT, V7X_LANES), 1)
    first_head = lane < NA_HEAD_DIM
    gr = NA_GROUP_ROWS

    def row_slice(grid_row):
        return pl.ds(pl.multiple_of(tok0 + grid_row * GRID_W, GRID_W), GRID_W)

    def row_block(ref, grid_row):
        return ref[row_slice(grid_row), :]

    def band(blocks, t):
        return jnp.concatenate(
            [blk[c0:c0 + n] for blk in blocks for (c0, n) in _NA_KSEGS[t]], axis=0)

    def step(groups):
        scores = []
        for row0, base, y in groups:
            kblocks = [row_block(k_ref, base + u) for u in range(NA_BAND_ROWS)]
            qrows = [row_block(q_ref, row0 + j).astype(F32) for j in range(gr)]
            for t in range(NA_TILES):
                qc = _NA_QCOLS[t]
                q2 = []
                for j in range(gr):
                    if t == 0:
                        qt = jnp.concatenate([qrows[j][0:NA_EDGE], qrows[j][GRID_W - NA_EDGE:]], axis=0)
                    else:
                        qt = qrows[j][qc[0]:qc[0] + NA_QT]
                    q2 += [jnp.where(first_head, qt, 0.0), jnp.where(first_head, 0.0, qt)]
                s = _nt_dot(jnp.concatenate(q2, axis=0).astype(BF16), band(kblocks, t))
                scores.append(s + bias_ref[y, t])
        probs, sums = [], []
        for s in scores:
            p = jnp.exp(s - jnp.max(s, axis=-1, keepdims=True))
            sums.append(jnp.sum(p, axis=-1, keepdims=True))
            probs.append(p.astype(BF16))
        for g, (row0, base, y) in enumerate(groups):
            vblocks = [row_block(v_ref, base + u) for u in range(NA_BAND_ROWS)]
            o_tiles = []
            for t in range(NA_TILES):
                o2 = _dot(probs[g * NA_TILES + t], band(vblocks, t)) / sums[g * NA_TILES + t]
                o_tiles.append(o2)
            for j in range(gr):
                sel = [jnp.where(first_head, o[2 * NA_QT * j:2 * NA_QT * j + NA_QT],
                                 o[2 * NA_QT * j + NA_QT:2 * NA_QT * (j + 1)]) for o in o_tiles]
                orow = jnp.concatenate([sel[0][0:NA_EDGE], *sel[1:], sel[0][NA_EDGE:]], axis=0)
                o_ref[row_slice(row0 + j), :] = orow.astype(o_ref.dtype)

    n_groups = rows // gr
    shift = min(NA_WIN_R, rows) // 2
    regular = [bases[g] == g * gr - shift for g in range(n_groups)]
    lo = regular.index(True)
    hi = n_groups - regular[::-1].index(True)
    per = NA_GROUPS_PER_STEP
    assert all(regular[lo:hi]) and (hi - lo) % per == 0
    assert len({types[g] for g in range(lo, hi)}) == 1
    edge = [(g * gr, bases[g], types[g]) for g in list(range(lo)) + list(range(hi, n_groups))]
    for i in range(0, len(edge), per):
        step(edge[i:i + per])

    for g0 in range(lo, hi, per):
        step([((g0 + n) * gr, (g0 + n) * gr - shift, types[lo]) for n in range(per)])


def _natten(proj, rpb, batch, seq):
    rows = seq // GRID_W
    assert rows >= NA_BAND_ROWS and rows % NA_GROUP_ROWS == 0
    assert GRID_W == 64 and NA_HEAD_DIM * 2 == V7X_LANES
    pairs = NA_HEADS // 2
    bases, types, sigs = _na_group_plan(rows)
    bias = _na_bias_table(rpb, sigs)
    nb = NA_BATCH_PER_STEP
    assert batch % nb == 0
    col = lambda off: pl.BlockSpec((None, nb * seq, V7X_LANES),
                                   lambda p, b, off=off: (off + p, b, 0))
    return pl.pallas_call(
        functools.partial(_natten_kernel, n_batch=nb, rows=rows, bases=tuple(bases),
                          types=tuple(types)),
        grid=(pairs, batch // nb),
        in_specs=[
            col(0), col(pairs), col(2 * pairs),
            pl.BlockSpec((len(sigs), None, NA_TILES, NA_GROUP_ROWS * 2 * NA_QT,
                          NA_BAND_ROWS * NA_KT), lambda p, b: (0, p, 0, 0, 0)),
        ],
        out_specs=pl.BlockSpec((None, nb * seq, V7X_LANES), lambda p, b: (p, b, 0)),
        out_shape=jax.ShapeDtypeStruct((pairs, batch * seq, V7X_LANES), BF16),
        compiler_params=pltpu.CompilerParams(
            dimension_semantics=("arbitrary", "arbitrary"),
            vmem_limit_bytes=V7X_VMEM_LIMIT_BYTES),
        name="natten",
    )(proj, proj, proj, bias)


def _hg_tables():
    c = HG_CHUNK
    idx = np.arange(c)
    lower = (idx[None, :] <= idx[:, None]).astype(np.float32)
    cums = np.stack([np.concatenate([lower, lower], axis=1),
                     np.concatenate([lower.T, lower.T], axis=1)])
    masks = []
    for lvl in range(HG_LEVELS):
        s = 1 << lvl
        same_block = (idx[:, None] // (2 * s)) == (idx[None, :] // (2 * s))
        half = (idx // s) % 2
        masks.append(same_block & (half[:, None] != half[None, :]))
    masks.append(np.eye(c, dtype=bool))
    return jnp.asarray(cums, BF16), jnp.asarray(np.stack(masks), F32)


def _hgrn_kernel(*refs, n_batch, seq):
    def one(bb, carry):
        _hgrn_one(*refs, bb * seq, seq=seq)
        return carry

    lax.fori_loop(0, n_batch, one, 0)


def _hgrn_one(q_ref, ff_ref, fb_ref, v_ref, lbf_ref, lbb_ref,
              cum_ref, mask_ref, o_ref, acc_s, qb_s, inc_s, etb_s, tok0, *, seq):
    c = HG_CHUNK
    n_chunks = seq // c
    n_step = HG_CHUNKS_PER_STEP
    assert n_chunks % n_step == 0
    grp = V7X_SUBLANES
    sub = lax.broadcasted_iota(jnp.int32, (c, V7X_LANES), 0) & (grp - 1)

    def lower_bound(lb_ref):
        a = lb_ref[...].astype(F32)
        a0, a1 = a[0:1], a[1:2]
        m = jnp.maximum(a0, a1)
        e0, e1 = jnp.exp(a0 - m), jnp.exp(a1 - m)
        return e0 / (e0 + e1)

    lbf = lower_bound(lbf_ref)
    lbb = lower_bound(lbb_ref)

    def rows(ci, tok=0):
        return pl.ds(pl.multiple_of(tok + ci * c, c), c)

    def group_rows(x, r):
        return jnp.concatenate(
            [jnp.broadcast_to(x[grp * j + r:grp * j + r + 1], (grp, V7X_LANES))
             for j in range(c // grp)], axis=0)

    def repeat_group(xg, n_rows):
        return xg if n_rows == grp else jnp.concatenate([xg] * (n_rows // grp), axis=0)

    def boundary_rows(x, s, offset):
        parts = [group_rows(x, 2 * s * blk + s - 1 + offset) for blk in range(grp // (2 * s))]
        assert len(parts) <= 2
        return parts[0] if len(parts) == 1 else jnp.where(sub < grp // 2, parts[0], parts[1])

    def gates(f_ref, sl, lb):
        hf = f_ref[sl, :].astype(F32)
        f = lb + (1.0 - lb) * (1.0 / (1.0 + jnp.exp(-hf)))
        hi, lo = _split_bf16(jnp.log2(f))
        return f, 1.0 - f, jnp.concatenate([hi, lo], axis=0)

    def stage_gates(ci):
        src = rows(ci, tok0)
        st = dict(sl=rows(ci), q=q_ref[src, :].astype(F32), v=v_ref[src, :])
        st["f_f"], st["k_f"], lf_f = gates(ff_ref, src, lbf)
        st["f_b"], st["k_b"], lf_b = gates(fb_ref, src, lbb)
        st["b_f"] = _dot(cum_ref[0], lf_f)
        st["b_b"] = _dot(cum_ref[1], lf_b)
        return st

    def level_operands(st, lvl):
        q, k_f, k_b, b_f, b_b = st["q"], st["k_f"], st["k_b"], st["b_f"], st["b_b"]
        if lvl == HG_LEVELS:
            return q, k_f + k_b
        if lvl == 0:
            odd = (sub & 1) == 1
            return q * jnp.where(odd, st["f_f"], st["f_b"]), jnp.where(odd, k_b, k_f)
        if (1 << lvl) < grp:
            later = (sub & (1 << lvl)) != 0
            d_f = b_f - boundary_rows(b_f, 1 << lvl, 0)
            d_b = b_b - boundary_rows(b_b, 1 << lvl, 1)
            return (q * jnp.exp2(jnp.where(later, d_f, d_b)),
                    jnp.where(later, k_b, k_f) * jnp.exp2(-jnp.where(later, d_b, d_f)))
        s = 1 << lvl
        xp, yp = [], []
        for r0 in range(0, c, 2 * s):
            m = r0 + s
            rf = repeat_group(st["last_f"][m - grp:m], s)
            rb = repeat_group(st["first_b"][m:m + grp], s)
            early, late = slice(r0, m), slice(m, m + s)
            xp += [q[early] * jnp.exp2(b_b[early] - rb), q[late] * jnp.exp2(b_f[late] - rf)]
            yp += [k_f[early] * jnp.exp2(rf - b_f[early]), k_b[late] * jnp.exp2(rb - b_b[late])]
        return jnp.concatenate(xp, axis=0), jnp.concatenate(yp, axis=0)

    def stage_operands(ci, st):
        q, k_f, k_b, b_f, b_b = st["q"], st["k_f"], st["k_b"], st["b_f"], st["b_b"]
        last_f = st["last_f"] = group_rows(b_f, grp - 1)
        first_b = st["first_b"] = group_rows(b_b, 0)
        sl = st["sl"]
        tot_f = last_f[c - grp:c]
        tot_b = first_b[0:grp]
        st["qe_f"] = (q * jnp.exp2(b_f)).astype(BF16)
        st["kd_f"] = (k_f * jnp.exp2(repeat_group(tot_f, c) - b_f)).astype(BF16)
        st["tot_f"] = jnp.exp2(tot_f[0:1])
        qb_s[sl, :] = (q * jnp.exp2(b_b)).astype(BF16)
        st["kd_b"] = (k_b * jnp.exp2(repeat_group(tot_b, c) - b_b)).astype(BF16)
        etb_s[pl.ds(pl.multiple_of(ci * grp, grp), grp), :] = jnp.exp2(tot_b)
        return st

    def fwd_body(i, state_t):
        cis = [i * n_step + j for j in range(n_step)]
        sts = [stage_gates(ci) for ci in cis]
        sts = [stage_operands(ci, st) for ci, st in zip(cis, sts)]
        for st in sts:
            for lvl in range(HG_LEVELS + 1):
                x, y = level_operands(st, lvl)
                term = mask_ref[lvl] * _nt_dot(x.astype(BF16), y.astype(BF16))
                st["a"] = term if lvl == 0 else st["a"] + term
        intra = [_dot(st["a"].astype(BF16), st["v"]) for st in sts]
        incs = [_dot(st["v"].T, jnp.concatenate([st["kd_f"], st["kd_b"]], axis=1)) for st in sts]
        for st, o, inc in zip(sts, intra, incs):
            acc_s[st["sl"], :] = o + _nt_dot(st["qe_f"], state_t.astype(BF16))
            state_t = state_t * st["tot_f"] + inc[:, 0:HG_DK]
            inc_s[st["sl"], :] = inc[:, HG_DK:2 * HG_DK]
        return state_t

    zero = jnp.zeros((HG_DK, HG_DK), F32)
    lax.fori_loop(0, n_chunks // n_step, fwd_body, zero)

    n_out = HG_OUT_CHUNKS_PER_STEP
    assert n_chunks % n_out == 0

    def bwd_body(i, state_t):
        cis = [n_chunks - 1 - (i * n_out + j) for j in range(n_out)]
        inter = []
        for ci in cis:
            inter.append(_nt_dot(qb_s[rows(ci), :], state_t.astype(BF16)))
            tot_b = etb_s[pl.ds(pl.multiple_of(ci * grp, grp), 1), :]
            state_t = state_t * tot_b + inc_s[rows(ci), :]
        for ci, o_inter in zip(cis, inter):
            o_ref[rows(ci, tok0), :] = (acc_s[rows(ci), :] + o_inter).astype(o_ref.dtype)
        return state_t

    lax.fori_loop(0, n_chunks // n_out, bwd_body, zero)


def _hgrn2(proj, lb_fwd, lb_bwd, batch, seq):
    assert HG_DK == V7X_LANES and lb_fwd.shape[0] == 2
    cums, masks = _hg_tables()
    first = 3 * NA_WIDTH // V7X_LANES
    nb = HG_BATCH_PER_STEP
    assert batch % nb == 0
    col = lambda k: pl.BlockSpec((None, nb * seq, V7X_LANES),
                                 lambda b, h, k=k: (first + k * HG_HEADS + h, b, 0))
    lbs = pl.BlockSpec((2, V7X_LANES), lambda b, h: (0, h))
    const = lambda a: pl.BlockSpec(a.shape, lambda b, h, nd=a.ndim: (0,) * nd)
    return pl.pallas_call(
        functools.partial(_hgrn_kernel, n_batch=nb, seq=seq),
        grid=(batch // nb, HG_HEADS),
        in_specs=[col(0), col(1), col(2), col(3), lbs, lbs, const(cums), const(masks)],
        out_specs=pl.BlockSpec((None, nb * seq, V7X_LANES), lambda b, h: (h, b, 0)),
        out_shape=jax.ShapeDtypeStruct((HG_HEADS, batch * seq, V7X_LANES), BF16),
        scratch_shapes=[pltpu.VMEM((seq, HG_DK), F32), pltpu.VMEM((seq, HG_DK), BF16),
                        pltpu.VMEM((seq, HG_DK), F32),
                        pltpu.VMEM((V7X_SUBLANES * seq // HG_CHUNK, HG_DK), F32)],
        compiler_params=pltpu.CompilerParams(
            dimension_semantics=("arbitrary", "arbitrary"),
            vmem_limit_bytes=V7X_VMEM_LIMIT_BYTES),
        name="hgrn2",
    )(proj, proj, proj, proj, lb_fwd.astype(F32), lb_bwd.astype(F32), cums, masks)


def _out_mlp_kernel(x_ref, na_ref, hg_ref, g_ref, gw_ref, wo_na_ref, wo_hg_ref, nw_ref, wu_ref,
                    wd_ref, o_ref):
    def recurrence_head(j):
        o = hg_ref[j].astype(F32)
        ms = jnp.mean(o * o, axis=-1, keepdims=True)
        g = g_ref[j].astype(F32)
        return (o * lax.rsqrt(ms + RMS_EPS) * gw_ref[...]
                * (g * (1.0 / (1.0 + jnp.exp(-g))))).astype(BF16)

    y_na = jnp.concatenate([na_ref[j] for j in range(na_ref.shape[0])], axis=1)
    y_hg = jnp.concatenate([recurrence_head(j) for j in range(HG_HEADS)], axis=1)
    h = x_ref[...] + _dot(y_na, wo_na_ref[...]) + _dot(y_hg, wo_hg_ref[...])
    ms = jnp.mean(h * h, axis=-1, keepdims=True)
    u = (h * lax.rsqrt(ms + RMS_EPS) * nw_ref[...]).astype(BF16)
    mlp = None
    for j in range(D_FF // FF_CHUNK):
        cols = slice(j * FF_CHUNK, (j + 1) * FF_CHUNK)
        a = jnp.maximum(_dot(u, wu_ref[:, cols]), 0.0)
        part = _dot((a * a).astype(BF16), wd_ref[cols, :])
        mlp = part if mlp is None else mlp + part
    o_ref[...] = h + mlp


def _out_mlp(x2, y_na, o_hg, proj, hg_norm_w, w_out, norm_w, w_up, w_down):
    m = x2.shape[0]
    first_g = (3 * NA_WIDTH + 4 * HG_WIDTH) // V7X_LANES
    assert first_g % HG_HEADS == 0
    gate = pl.BlockSpec((HG_HEADS, TOKEN_TILE, V7X_LANES), lambda i: (first_g // HG_HEADS, i, 0))
    gw = hg_norm_w.astype(F32).reshape(1, HG_DK)
    tile = lambda width: pl.BlockSpec((TOKEN_TILE, width), lambda i: (i, 0))
    blocks = lambda a: pl.BlockSpec((a.shape[0], TOKEN_TILE, V7X_LANES), lambda i: (0, i, 0))
    resident = lambda a: pl.BlockSpec(a.shape, lambda i: (0, 0), pipeline_mode=pl.Buffered(1))
    wo_na, wo_hg = w_out[:NA_WIDTH], w_out[NA_WIDTH:]
    return pl.pallas_call(
        _out_mlp_kernel,
        grid=(m // TOKEN_TILE,),
        in_specs=[tile(D_MODEL), blocks(y_na), blocks(o_hg), gate,
                  pl.BlockSpec((1, HG_DK), lambda i: (0, 0)), resident(wo_na),
                  resident(wo_hg), pl.BlockSpec((1, D_MODEL), lambda i: (0, 0)),
                  resident(w_up), resident(w_down)],
        out_specs=tile(D_MODEL),
        out_shape=jax.ShapeDtypeStruct((m, D_MODEL), F32),
        compiler_params=pltpu.CompilerParams(
            dimension_semantics=("arbitrary",), vmem_limit_bytes=V7X_VMEM_LIMIT_BYTES),
        name="out_mlp",
    )(x2, y_na, o_hg, proj, gw, wo_na, wo_hg, norm_w, w_up, w_down)


def kernel(x, w_in, w_out, attn_norm_w, mlp_norm_w, q_norm_w, k_norm_w, rpb, hg_norm_w,
           lb_fwd, lb_bwd, w_up, w_down):
    batch, seq, d = x.shape
    assert d == D_MODEL and w_in.shape[0] == 1 and seq % GRID_W == 0 and seq % HG_CHUNK == 0
    x2 = x.reshape(batch * seq, d)
    proj = _in_proj(x2, attn_norm_w[0].astype(F32).reshape(1, d), w_in[0].astype(BF16),
                    q_norm_w[0], k_norm_w[0])
    y_na = _natten(proj, rpb[0], batch, seq)
    o_hg = _hgrn2(proj, lb_fwd, lb_bwd, batch, seq)
    out = _out_mlp(x2, y_na, o_hg, proj, hg_norm_w[0], w_out[0].astype(BF16),
                   mlp_norm_w[0].astype(F32).reshape(1, d),
                   w_up[0].astype(BF16), w_down[0].astype(BF16))
    return out.reshape(batch, seq, d)
```

```python
import functools

import numpy as np
import jax
import jax.numpy as jnp
from jax import lax
from jax.experimental import pallas as pl
from jax.experimental.pallas import tpu as pltpu

F32 = jnp.float32
BF16 = jnp.bfloat16

D_MODEL = 1024
GRID_W = 64
NA_HEADS = 8
NA_HEAD_DIM = 64
NA_WIDTH = NA_HEADS * NA_HEAD_DIM
NA_WIN_R = 8
NA_WIN_C = 16
HG_HEADS = 4
HG_DK = 128
HG_WIDTH = HG_HEADS * HG_DK
D_FF = 4 * D_MODEL
D_IN_PROJ = 3 * NA_WIDTH + 5 * HG_WIDTH
RMS_EPS = 1e-6

V7X_LANES = 128
V7X_SUBLANES = 8
V7X_VMEM_LIMIT_BYTES = 56 * 1024 * 1024

TOKEN_TILE = 1024
FF_CHUNK = 1024
HG_CHUNK = 128
HG_LEVELS = 7
HG_CHUNKS_PER_STEP = 16
HG_OUT_CHUNKS_PER_STEP = 16
NA_QT = 16
NA_KT = 32
NA_TILES = GRID_W // NA_QT
NA_GROUP_ROWS = 4
NA_BAND_ROWS = 12
NA_GROUPS_PER_STEP = 2
NA_BATCH_PER_STEP = 2
HG_BATCH_PER_STEP = 2
MASK_NEG = -1e30

NA_EDGE = NA_WIN_C // 2
_NA_QCOLS = [list(range(0, NA_EDGE)) + list(range(GRID_W - NA_EDGE, GRID_W))] + [
    list(range(NA_EDGE + NA_QT * t, NA_EDGE + NA_QT * (t + 1))) for t in range(NA_TILES - 1)]
_NA_KSEGS = [[(0, NA_WIN_C), (GRID_W - NA_WIN_C, NA_WIN_C)]] + [
    [(NA_QT * t, NA_KT)] for t in range(NA_TILES - 1)]


def _nt_dot(a, b):
    return lax.dot_general(a, b, (((1,), (1,)), ((), ())), preferred_element_type=F32)


def _dot(a, b):
    return jnp.dot(a, b, preferred_element_type=F32)


def _split_bf16(x):
    hi = x.astype(BF16)
    lo = (x - hi.astype(F32)).astype(BF16)
    return hi, lo


def _in_proj_kernel(x_ref, nw_ref, w_ref, qkw_ref, o_ref):
    x = x_ref[...]
    ms = jnp.mean(x * x, axis=-1, keepdims=True)
    u = (x * lax.rsqrt(ms + RMS_EPS) * nw_ref[...]).astype(BF16)
    res = _dot(u, w_ref[...])
    first_head = lax.broadcasted_iota(jnp.int32, (1, V7X_LANES), 1) < NA_HEAD_DIM
    for j in range(D_IN_PROJ // V7X_LANES):
        blk = res[:, j * V7X_LANES:(j + 1) * V7X_LANES]
        if j < qkw_ref.shape[0]:
            sq = blk * blk
            s_all = jnp.sum(sq, axis=-1, keepdims=True)
            s_one = jnp.sum(jnp.where(first_head, sq, 0.0), axis=-1, keepdims=True)
            ms_h = jnp.where(first_head, s_one, s_all - s_one) * (1.0 / NA_HEAD_DIM)
            blk = blk * lax.rsqrt(ms_h + RMS_EPS) * qkw_ref[j]
        o_ref[j] = blk.astype(o_ref.dtype)


def _in_proj(x2, norm_w, w_in, q_norm_w, k_norm_w):
    m = x2.shape[0]
    pair_w = lambda w, mult: jnp.tile(w.astype(F32) * mult, 2).reshape(1, 1, V7X_LANES)
    n_blk = NA_WIDTH // V7X_LANES
    qkw = jnp.concatenate([jnp.tile(pair_w(q_norm_w, NA_HEAD_DIM ** -0.5), (n_blk, 1, 1)),
                           jnp.tile(pair_w(k_norm_w, 1.0), (n_blk, 1, 1))])
    return pl.pallas_call(
        _in_proj_kernel,
        grid=(m // TOKEN_TILE,),
        in_specs=[
            pl.BlockSpec((TOKEN_TILE, D_MODEL), lambda i: (i, 0)),
            pl.BlockSpec((1, D_MODEL), lambda i: (0, 0)),
            pl.BlockSpec((D_MODEL, D_IN_PROJ), lambda i: (0, 0), pipeline_mode=pl.Buffered(1)),
            pl.BlockSpec(qkw.shape, lambda i: (0, 0, 0)),
        ],
        out_specs=pl.BlockSpec((D_IN_PROJ // V7X_LANES, TOKEN_TILE, V7X_LANES), lambda i: (0, i, 0)),
        out_shape=jax.ShapeDtypeStruct((D_IN_PROJ // V7X_LANES, m, V7X_LANES), BF16),
        compiler_params=pltpu.CompilerParams(
            dimension_semantics=("arbitrary",), vmem_limit_bytes=V7X_VMEM_LIMIT_BYTES),
        name="in_proj",
    )(x2, norm_w, w_in, qkw)


def _na_group_plan(rows):
    wr = min(NA_WIN_R, rows)
    bases, types, sigs = [], [], []
    for g in range(rows // NA_GROUP_ROWS):
        r0 = g * NA_GROUP_ROWS
        starts = [min(max(r0 + j - wr // 2, 0), rows - wr) for j in range(NA_GROUP_ROWS)]
        base = min(max(starts[0], 0), rows - NA_BAND_ROWS)
        assert base <= starts[0] and starts[-1] + wr <= base + NA_BAND_ROWS
        sig = tuple((starts[j] - base, r0 + j - base) for j in range(NA_GROUP_ROWS))
        if sig not in sigs:
            sigs.append(sig)
        bases.append(base)
        types.append(sigs.index(sig))
    return bases, types, sigs


def _na_bias_table(rpb, sigs):
    n_ri, n_ci = 2 * NA_WIN_R - 1, 2 * NA_WIN_C - 1
    csel = np.zeros((n_ci, NA_TILES, NA_QT, NA_KT), np.float32)
    for t in range(NA_TILES):
        qc = np.array(_NA_QCOLS[t])
        kc = np.concatenate([np.arange(c0, c0 + n) for c0, n in _NA_KSEGS[t]])
        cs = np.clip(qc - NA_WIN_C // 2, 0, GRID_W - NA_WIN_C)
        valid = (kc[None, :] >= cs[:, None]) & (kc[None, :] < cs[:, None] + NA_WIN_C)
        cidx = kc[None, :] - qc[:, None] + NA_WIN_C - 1
        qi, ki = np.nonzero(valid)
        csel[cidx[qi, ki], t, qi, ki] = 1.0
    cols = jnp.einsum("hrc,ctqk->htqrk", rpb.astype(F32), jnp.asarray(csel),
                      precision=lax.Precision.HIGHEST)
    cols = cols + jnp.asarray(np.where(csel.sum(0) > 0, 0.0, MASK_NEG)[None, :, :, None, :], F32)
    cols = cols.reshape(NA_HEADS // 2, 2, NA_TILES, NA_QT, n_ri * NA_KT)
    cols = jnp.transpose(cols, (0, 2, 1, 3, 4))
    tabs = []
    for sig in sigs:
        per_row = []
        for first, qrow in sig:
            ri0 = first - qrow + NA_WIN_R - 1
            after = NA_BAND_ROWS - first - NA_WIN_R
            window = cols[..., ri0 * NA_KT:(ri0 + NA_WIN_R) * NA_KT]
            per_row.append(jnp.pad(window, [(0, 0)] * 4 + [(first * NA_KT, after * NA_KT)],
                                   constant_values=MASK_NEG))
        tabs.append(jnp.stack(per_row, axis=2))
    return jnp.stack(tabs).reshape(len(sigs), NA_HEADS // 2, NA_TILES,
                                   NA_GROUP_ROWS * 2 * NA_QT, NA_BAND_ROWS * NA_KT)


def _natten_kernel(*refs, n_batch, rows, bases, types):
    for bb in range(n_batch):
        _natten_one(*refs, bb * rows * GRID_W, rows=rows, bases=bases, types=types)


def _natten_one(q_ref, k_ref, v_ref, bias_ref, o_ref, tok0, *, rows, bases, types):
    lane = lax.broadcasted_iota(jnp.int32, (NA_QT, V7X_LANES), 1)
    first_head = lane < NA_HEAD_DIM
    gr = NA_GROUP_ROWS

    def row_slice(grid_row):
        start = tok0 + grid_row * GRID_W
        if isinstance(start, int):
            return slice(start, start + GRID_W)
        return pl.ds(pl.multiple_of(start, GRID_W), GRID_W)

    def row_block(ref, grid_row):
        return ref[row_slice(grid_row), :]

    def band(blocks, t):
        return jnp.concatenate(
            [blk[c0:c0 + n] for blk in blocks for (c0, n) in _NA_KSEGS[t]], axis=0)

    def step(groups):
        scores = []
        for row0, base, y in groups:
            kblocks = [row_block(k_ref, base + u) for u in range(NA_BAND_ROWS)]
            qrows = [row_block(q_ref, row0 + j).astype(F32) for j in range(gr)]
            for t in range(NA_TILES):
                qc = _NA_QCOLS[t]
                q2 = []
                for j in range(gr):
                    if t == 0:
                        qt = jnp.concatenate([qrows[j][0:NA_EDGE], qrows[j][GRID_W - NA_EDGE:]], axis=0)
                    else:
                        qt = qrows[j][qc[0]:qc[0] + NA_QT]
                    q2 += [jnp.where(first_head, qt, 0.0), jnp.where(first_head, 0.0, qt)]
                s = _nt_dot(jnp.concatenate(q2, axis=0).astype(BF16), band(kblocks, t))
                scores.append(s + bias_ref[y, t])
        probs, sums = [], []
        for s in scores:
            p = jnp.exp(s - jnp.max(s, axis=-1, keepdims=True))
            sums.append(jnp.sum(p, axis=-1, keepdims=True))
            probs.append(p.astype(BF16))
        for g, (row0, base, y) in enumerate(groups):
            vblocks = [row_block(v_ref, base + u) for u in range(NA_BAND_ROWS)]
            o_tiles = []
            for t in range(NA_TILES):
                o2 = _dot(probs[g * NA_TILES + t], band(vblocks, t)) / sums[g * NA_TILES + t]
                o_tiles.append(o2)
            for j in range(gr):
                sel = [jnp.where(first_head, o[2 * NA_QT * j:2 * NA_QT * j + NA_QT],
                                 o[2 * NA_QT * j + NA_QT:2 * NA_QT * (j + 1)]) for o in o_tiles]
                orow = jnp.concatenate([sel[0][0:NA_EDGE], *sel[1:], sel[0][NA_EDGE:]], axis=0)
                o_ref[row_slice(row0 + j), :] = orow.astype(o_ref.dtype)

    n_groups = rows // gr
    shift = min(NA_WIN_R, rows) // 2
    regular = [bases[g] == g * gr - shift for g in range(n_groups)]
    lo = regular.index(True)
    hi = n_groups - regular[::-1].index(True)
    per = NA_GROUPS_PER_STEP
    assert all(regular[lo:hi]) and (hi - lo) % per == 0
    assert len({types[g] for g in range(lo, hi)}) == 1
    edge = [(g * gr, bases[g], types[g]) for g in list(range(lo)) + list(range(hi, n_groups))]
    for i in range(0, len(edge), per):
        step(edge[i:i + per])

    for g0 in range(lo, hi, per):
        step([((g0 + n) * gr, (g0 + n) * gr - shift, types[lo]) for n in range(per)])


def _natten(proj, rpb, batch, seq):
    rows = seq // GRID_W
    assert rows >= NA_BAND_ROWS and rows % NA_GROUP_ROWS == 0
    assert GRID_W == 64 and NA_HEAD_DIM * 2 == V7X_LANES
    pairs = NA_HEADS // 2
    bases, types, sigs = _na_group_plan(rows)
    bias = _na_bias_table(rpb, sigs)
    nb = NA_BATCH_PER_STEP
    assert batch % nb == 0
    col = lambda off: pl.BlockSpec((None, nb * seq, V7X_LANES),
                                   lambda p, b, off=off: (off + p, b, 0))
    return pl.pallas_call(
        functools.partial(_natten_kernel, n_batch=nb, rows=rows, bases=tuple(bases),
                          types=tuple(types)),
        grid=(pairs, batch // nb),
        in_specs=[
            col(0), col(pairs), col(2 * pairs),
            pl.BlockSpec((len(sigs), None, NA_TILES, NA_GROUP_ROWS * 2 * NA_QT,
                          NA_BAND_ROWS * NA_KT), lambda p, b: (0, p, 0, 0, 0)),
        ],
        out_specs=pl.BlockSpec((None, nb * seq, V7X_LANES), lambda p, b: (p, b, 0)),
        out_shape=jax.ShapeDtypeStruct((pairs, batch * seq, V7X_LANES), BF16),
        compiler_params=pltpu.CompilerParams(
            dimension_semantics=("arbitrary", "arbitrary"),
            vmem_limit_bytes=V7X_VMEM_LIMIT_BYTES),
        name="natten",
    )(proj, proj, proj, bias)


def _hg_tables():
    c = HG_CHUNK
    idx = np.arange(c)
    lower = (idx[None, :] <= idx[:, None]).astype(np.float32)
    cums = np.stack([np.concatenate([lower, lower], axis=1),
                     np.concatenate([lower.T, lower.T], axis=1)])
    masks = []
    for lvl in range(HG_LEVELS):
        s = 1 << lvl
        same_block = (idx[:, None] // (2 * s)) == (idx[None, :] // (2 * s))
        half = (idx // s) % 2
        masks.append(same_block & (half[:, None] != half[None, :]))
    masks.append(np.eye(c, dtype=bool))
    return jnp.asarray(cums, BF16), jnp.asarray(np.stack(masks), F32)


def _hgrn_kernel(*refs, n_batch, seq):
    def one(bb, carry):
        _hgrn_one(*refs, bb * seq, seq=seq)
        return carry

    lax.fori_loop(0, n_batch, one, 0)


def _hgrn_one(q_ref, ff_ref, fb_ref, v_ref, lbf_ref, lbb_ref,
              cum_ref, mask_ref, o_ref, acc_s, qb_s, inc_s, etb_s, tok0, *, seq):
    c = HG_CHUNK
    n_chunks = seq // c
    n_step = HG_CHUNKS_PER_STEP
    assert n_chunks % n_step == 0
    grp = V7X_SUBLANES
    sub = lax.broadcasted_iota(jnp.int32, (c, V7X_LANES), 0) & (grp - 1)

    def lower_bound(lb_ref):
        a = lb_ref[...].astype(F32)
        a0, a1 = a[0:1], a[1:2]
        m = jnp.maximum(a0, a1)
        e0, e1 = jnp.exp(a0 - m), jnp.exp(a1 - m)
        return e0 / (e0 + e1)

    lbf = lower_bound(lbf_ref)
    lbb = lower_bound(lbb_ref)

    def rows(ci, tok=0):
        return pl.ds(pl.multiple_of(tok + ci * c, c), c)

    def group_rows(x, r):
        return jnp.concatenate(
            [jnp.broadcast_to(x[grp * j + r:grp * j + r + 1], (grp, V7X_LANES))
             for j in range(c // grp)], axis=0)

    def repeat_group(xg, n_rows):
        return xg if n_rows == grp else jnp.concatenate([xg] * (n_rows // grp), axis=0)

    def boundary_rows(x, s, offset):
        parts = [group_rows(x, 2 * s * blk + s - 1 + offset) for blk in range(grp // (2 * s))]
        assert len(parts) <= 2
        return parts[0] if len(parts) == 1 else jnp.where(sub < grp // 2, parts[0], parts[1])

    def gates(f_ref, sl, lb):
        hf = f_ref[sl, :].astype(F32)
        f = lb + (1.0 - lb) * (1.0 / (1.0 + jnp.exp(-hf)))
        hi, lo = _split_bf16(jnp.log2(f))
        return f, 1.0 - f, jnp.concatenate([hi, lo], axis=0)

    def stage_gates(ci):
        src = rows(ci, tok0)
        st = dict(sl=rows(ci), q=q_ref[src, :].astype(F32), v=v_ref[src, :])
        st["f_f"], st["k_f"], lf_f = gates(ff_ref, src, lbf)
        st["f_b"], st["k_b"], lf_b = gates(fb_ref, src, lbb)
        st["b_f"] = _dot(cum_ref[0], lf_f)
        st["b_b"] = _dot(cum_ref[1], lf_b)
        return st

    def level_operands(st, lvl):
        q, k_f, k_b, b_f, b_b = st["q"], st["k_f"], st["k_b"], st["b_f"], st["b_b"]
        if lvl == HG_LEVELS:
            return q, k_f + k_b
        if lvl == 0:
            odd = (sub & 1) == 1
            return q * jnp.where(odd, st["f_f"], st["f_b"]), jnp.where(odd, k_b, k_f)
        if (1 << lvl) < grp:
            later = (sub & (1 << lvl)) != 0
            d_f = b_f - boundary_rows(b_f, 1 << lvl, 0)
            d_b = b_b - boundary_rows(b_b, 1 << lvl, 1)
            return (q * jnp.exp2(jnp.where(later, d_f, d_b)),
                    jnp.where(later, k_b, k_f) * jnp.exp2(-jnp.where(later, d_b, d_f)))
        s = 1 << lvl
        xp, yp = [], []
        for r0 in range(0, c, 2 * s):
            m = r0 + s
            rf = repeat_group(st["last_f"][m - grp:m], s)
            rb = repeat_group(st["first_b"][m:m + grp], s)
            early, late = slice(r0, m), slice(m, m + s)
            xp += [q[early] * jnp.exp2(b_b[early] - rb), q[late] * jnp.exp2(b_f[late] - rf)]
            yp += [k_f[early] * jnp.exp2(rf - b_f[early]), k_b[late] * jnp.exp2(rb - b_b[late])]
        return jnp.concatenate(xp, axis=0), jnp.concatenate(yp, axis=0)

    def stage_operands(ci, st):
        q, k_f, k_b, b_f, b_b = st["q"], st["k_f"], st["k_b"], st["b_f"], st["b_b"]
        last_f = st["last_f"] = group_rows(b_f, grp - 1)
        first_b = st["first_b"] = group_rows(b_b, 0)
        sl = st["sl"]
        tot_f = last_f[c - grp:c]
        tot_b = first_b[0:grp]
        st["qe_f"] = (q * jnp.exp2(b_f)).astype(BF16)
        st["kd_f"] = (k_f * jnp.exp2(repeat_group(tot_f, c) - b_f)).astype(BF16)
        st["tot_f"] = jnp.exp2(tot_f[0:1])
        qb_s[sl, :] = (q * jnp.exp2(b_b)).astype(BF16)
        st["kd_b"] = (k_b * jnp.exp2(repeat_group(tot_b, c) - b_b)).astype(BF16)
        etb_s[pl.ds(pl.multiple_of(ci * grp, grp), grp), :] = jnp.exp2(tot_b)
        return st

    def fwd_body(i, state_t):
        cis = [i * n_step + j for j in range(n_step)]
        sts = [stage_gates(ci) for ci in cis]
        sts = [stage_operands(ci, st) for ci, st in zip(cis, sts)]
        for st in sts:
            for lvl in range(HG_LEVELS + 1):
                x, y = level_operands(st, lvl)
                term = mask_ref[lvl] * _nt_dot(x.astype(BF16), y.astype(BF16))
                st["a"] = term if lvl == 0 else st["a"] + term
        intra = [_dot(st["a"].astype(BF16), st["v"]) for st in sts]
        incs = [_dot(st["v"].T, jnp.concatenate([st["kd_f"], st["kd_b"]], axis=1)) for st in sts]
        for st, o, inc in zip(sts, intra, incs):
            acc_s[st["sl"], :] = o + _nt_dot(st["qe_f"], state_t.astype(BF16))
            state_t = state_t * st["tot_f"] + inc[:, 0:HG_DK]
            inc_s[st["sl"], :] = inc[:, HG_DK:2 * HG_DK]
        return state_t

    zero = jnp.zeros((HG_DK, HG_DK), F32)
    lax.fori_loop(0, n_chunks // n_step, fwd_body, zero)

    n_out = HG_OUT_CHUNKS_PER_STEP
    assert n_chunks % n_out == 0

    def bwd_body(i, state_t):
        cis = [n_chunks - 1 - (i * n_out + j) for j in range(n_out)]
        inter = []
        for ci in cis:
            inter.append(_nt_dot(qb_s[rows(ci), :], state_t.astype(BF16)))
            tot_b = etb_s[pl.ds(pl.multiple_of(ci * grp, grp), 1), :]
            state_t = state_t * tot_b + inc_s[rows(ci), :]
        for ci, o_inter in zip(cis, inter):
            o_ref[rows(ci, tok0), :] = (acc_s[rows(ci), :] + o_inter).astype(o_ref.dtype)
        return state_t

    lax.fori_loop(0, n_chunks // n_out, bwd_body, zero)


def _hgrn2(proj, lb_fwd, lb_bwd, batch, seq):
    assert HG_DK == V7X_LANES and lb_fwd.shape[0] == 2
    cums, masks = _hg_tables()
    first = 3 * NA_WIDTH // V7X_LANES
    nb = HG_BATCH_PER_STEP
    assert batch % nb == 0
    col = lambda k: pl.BlockSpec((None, nb * seq, V7X_LANES),
                                 lambda b, h, k=k: (first + k * HG_HEADS + h, b, 0))
    lbs = pl.BlockSpec((2, V7X_LANES), lambda b, h: (0, h))
    const = lambda a: pl.BlockSpec(a.shape, lambda b, h, nd=a.ndim: (0,) * nd)
    return pl.pallas_call(
        functools.partial(_hgrn_kernel, n_batch=nb, seq=seq),
        grid=(batch // nb, HG_HEADS),
        in_specs=[col(0), col(1), col(2), col(3), lbs, lbs, const(cums), const(masks)],
        out_specs=pl.BlockSpec((None, nb * seq, V7X_LANES), lambda b, h: (h, b, 0)),
        out_shape=jax.ShapeDtypeStruct((HG_HEADS, batch * seq, V7X_LANES), BF16),
        scratch_shapes=[pltpu.VMEM((seq, HG_DK), F32), pltpu.VMEM((seq, HG_DK), BF16),
                        pltpu.VMEM((seq, HG_DK), F32),
                        pltpu.VMEM((V7X_SUBLANES * seq // HG_CHUNK, HG_DK), F32)],
        compiler_params=pltpu.CompilerParams(
            dimension_semantics=("arbitrary", "arbitrary"),
            vmem_limit_bytes=V7X_VMEM_LIMIT_BYTES),
        name="hgrn2",
    )(proj, proj, proj, proj, lb_fwd.astype(F32), lb_bwd.astype(F32), cums, masks)


def _out_mlp_kernel(x_ref, na_ref, hg_ref, g_ref, gw_ref, wo_na_ref, wo_hg_ref, nw_ref, wu_ref,
                    wd_ref, o_ref):
    def recurrence_head(j):
        o = hg_ref[j].astype(F32)
        ms = jnp.mean(o * o, axis=-1, keepdims=True)
        g = g_ref[j].astype(F32)
        return (o * lax.rsqrt(ms + RMS_EPS) * gw_ref[...]
                * (g * (1.0 / (1.0 + jnp.exp(-g))))).astype(BF16)

    y_na = jnp.concatenate([na_ref[j] for j in range(na_ref.shape[0])], axis=1)
    y_hg = jnp.concatenate([recurrence_head(j) for j in range(HG_HEADS)], axis=1)
    h = x_ref[...] + _dot(y_na, wo_na_ref[...]) + _dot(y_hg, wo_hg_ref[...])
    ms = jnp.mean(h * h, axis=-1, keepdims=True)
    u = (h * lax.rsqrt(ms + RMS_EPS) * nw_ref[...]).astype(BF16)
    mlp = None
    for j in range(D_FF // FF_CHUNK):
        cols = slice(j * FF_CHUNK, (j + 1) * FF_CHUNK)
        a = jnp.maximum(_dot(u, wu_ref[:, cols]), 0.0)
        part = _dot((a * a).astype(BF16), wd_ref[cols, :])
        mlp = part if mlp is None else mlp + part
    o_ref[...] = h + mlp


def _out_mlp(x2, y_na, o_hg, proj, hg_norm_w, w_out, norm_w, w_up, w_down):
    m = x2.shape[0]
    first_g = (3 * NA_WIDTH + 4 * HG_WIDTH) // V7X_LANES
    assert first_g % HG_HEADS == 0
    gate = pl.BlockSpec((HG_HEADS, TOKEN_TILE, V7X_LANES), lambda i: (first_g // HG_HEADS, i, 0))
    gw = hg_norm_w.astype(F32).reshape(1, HG_DK)
    tile = lambda width: pl.BlockSpec((TOKEN_TILE, width), lambda i: (i, 0))
    blocks = lambda a: pl.BlockSpec((a.shape[0], TOKEN_TILE, V7X_LANES), lambda i: (0, i, 0))
    resident = lambda a: pl.BlockSpec(a.shape, lambda i: (0, 0), pipeline_mode=pl.Buffered(1))
    wo_na, wo_hg = w_out[:NA_WIDTH], w_out[NA_WIDTH:]
    return pl.pallas_call(
        _out_mlp_kernel,
        grid=(m // TOKEN_TILE,),
        in_specs=[tile(D_MODEL), blocks(y_na), blocks(o_hg), gate,
                  pl.BlockSpec((1, HG_DK), lambda i: (0, 0)), resident(wo_na),
                  resident(wo_hg), pl.BlockSpec((1, D_MODEL), lambda i: (0, 0)),
                  resident(w_up), resident(w_down)],
        out_specs=tile(D_MODEL),
        out_shape=jax.ShapeDtypeStruct((m, D_MODEL), F32),
        compiler_params=pltpu.CompilerParams(
            dimension_semantics=("arbitrary",), vmem_limit_bytes=V7X_VMEM_LIMIT_BYTES),
        name="out_mlp",
    )(x2, y_na, o_hg, proj, gw, wo_na, wo_hg, norm_w, w_up, w_down)


def kernel(x, w_in, w_out, attn_norm_w, mlp_norm_w, q_norm_w, k_norm_w, rpb, hg_norm_w,
           lb_fwd, lb_bwd, w_up, w_down):
    batch, seq, d = x.shape
    assert d == D_MODEL and w_in.shape[0] == 1 and seq % GRID_W == 0 and seq % HG_CHUNK == 0
    x2 = x.reshape(batch * seq, d)
    proj = _in_proj(x2, attn_norm_w[0].astype(F32).reshape(1, d), w_in[0].astype(BF16),
                    q_norm_w[0], k_norm_w[0])
    y_na = _natten(proj, rpb[0], batch, seq)
    o_hg = _hgrn2(proj, lb_fwd, lb_bwd, batch, seq)
    out = _out_mlp(x2, y_na, o_hg, proj, hg_norm_w[0], w_out[0].astype(BF16),
                   mlp_norm_w[0].astype(F32).reshape(1, d),
                   w_up[0].astype(BF16), w_down[0].astype(BF16))
    return out.reshape(batch, seq, d)
```

```python
import functools

import numpy as np
import jax
import jax.numpy as jnp
from jax import lax
from jax.experimental import pallas as pl
from jax.experimental.pallas import tpu as pltpu

F32 = jnp.float32
BF16 = jnp.bfloat16

D_MODEL = 1024
GRID_W = 64
NA_HEADS = 8
NA_HEAD_DIM = 64
NA_WIDTH = NA_HEADS * NA_HEAD_DIM
NA_WIN_R = 8
NA_WIN_C = 16
HG_HEADS = 4
HG_DK = 128
HG_WIDTH = HG_HEADS * HG_DK
D_FF = 4 * D_MODEL
D_IN_PROJ = 3 * NA_WIDTH + 5 * HG_WIDTH
RMS_EPS = 1e-6

V7X_LANES = 128
V7X_SUBLANES = 8
V7X_VMEM_LIMIT_BYTES = 56 * 1024 * 1024

TOKEN_TILE = 1024
FF_CHUNK = 1024
HG_CHUNK = 128
HG_LEVELS = 7
HG_CHUNKS_PER_STEP = 16
HG_OUT_CHUNKS_PER_STEP = 16
NA_QT = 16
NA_KT = 32
NA_TILES = GRID_W // NA_QT
NA_GROUP_ROWS = 4
NA_BAND_ROWS = 12
NA_GROUPS_PER_STEP = 2
NA_BATCH_PER_STEP = 2
HG_BATCH_PER_STEP = 1
MASK_NEG = -1e30

NA_EDGE = NA_WIN_C // 2
_NA_QCOLS = [list(range(0, NA_EDGE)) + list(range(GRID_W - NA_EDGE, GRID_W))] + [
    list(range(NA_EDGE + NA_QT * t, NA_EDGE + NA_QT * (t + 1))) for t in range(NA_TILES - 1)]
_NA_KSEGS = [[(0, NA_WIN_C), (GRID_W - NA_WIN_C, NA_WIN_C)]] + [
    [(NA_QT * t, NA_KT)] for t in range(NA_TILES - 1)]


def _nt_dot(a, b):
    return lax.dot_general(a, b, (((1,), (1,)), ((), ())), preferred_element_type=F32)


def _dot(a, b):
    return jnp.dot(a, b, preferred_element_type=F32)


def _split_bf16(x):
    hi = x.astype(BF16)
    lo = (x - hi.astype(F32)).astype(BF16)
    return hi, lo


def _in_proj_kernel(x_ref, nw_ref, w_ref, qkw_ref, o_ref):
    x = x_ref[...]
    ms = jnp.mean(x * x, axis=-1, keepdims=True)
    u = (x * lax.rsqrt(ms + RMS_EPS) * nw_ref[...]).astype(BF16)
    res = _dot(u, w_ref[...])
    first_head = lax.broadcasted_iota(jnp.int32, (1, V7X_LANES), 1) < NA_HEAD_DIM
    for j in range(D_IN_PROJ // V7X_LANES):
        blk = res[:, j * V7X_LANES:(j + 1) * V7X_LANES]
        if j < qkw_ref.shape[0]:
            sq = blk * blk
            s_all = jnp.sum(sq, axis=-1, keepdims=True)
            s_one = jnp.sum(jnp.where(first_head, sq, 0.0), axis=-1, keepdims=True)
            ms_h = jnp.where(first_head, s_one, s_all - s_one) * (1.0 / NA_HEAD_DIM)
            blk = blk * lax.rsqrt(ms_h + RMS_EPS) * qkw_ref[j]
        o_ref[j] = blk.astype(o_ref.dtype)


def _in_proj(x2, norm_w, w_in, q_norm_w, k_norm_w):
    m = x2.shape[0]
    pair_w = lambda w, mult: jnp.tile(w.astype(F32) * mult, 2).reshape(1, 1, V7X_LANES)
    n_blk = NA_WIDTH // V7X_LANES
    qkw = jnp.concatenate([jnp.tile(pair_w(q_norm_w, NA_HEAD_DIM ** -0.5), (n_blk, 1, 1)),
                           jnp.tile(pair_w(k_norm_w, 1.0), (n_blk, 1, 1))])
    return pl.pallas_call(
        _in_proj_kernel,
        grid=(m // TOKEN_TILE,),
        in_specs=[
            pl.BlockSpec((TOKEN_TILE, D_MODEL), lambda i: (i, 0)),
            pl.BlockSpec((1, D_MODEL), lambda i: (0, 0)),
            pl.BlockSpec((D_MODEL, D_IN_PROJ), lambda i: (0, 0), pipeline_mode=pl.Buffered(1)),
            pl.BlockSpec(qkw.shape, lambda i: (0, 0, 0)),
        ],
        out_specs=pl.BlockSpec((D_IN_PROJ // V7X_LANES, TOKEN_TILE, V7X_LANES), lambda i: (0, i, 0)),
        out_shape=jax.ShapeDtypeStruct((D_IN_PROJ // V7X_LANES, m, V7X_LANES), BF16),
        compiler_params=pltpu.CompilerParams(
            dimension_semantics=("arbitrary",), vmem_limit_bytes=V7X_VMEM_LIMIT_BYTES),
        name="in_proj",
    )(x2, norm_w, w_in, qkw)


def _na_group_plan(rows):
    wr = min(NA_WIN_R, rows)
    bases, types, sigs = [], [], []
    for g in range(rows // NA_GROUP_ROWS):
        r0 = g * NA_GROUP_ROWS
        starts = [min(max(r0 + j - wr // 2, 0), rows - wr) for j in range(NA_GROUP_ROWS)]
        base = min(max(starts[0], 0), rows - NA_BAND_ROWS)
        assert base <= starts[0] and starts[-1] + wr <= base + NA_BAND_ROWS
        sig = tuple((starts[j] - base, r0 + j - base) for j in range(NA_GROUP_ROWS))
        if sig not in sigs:
            sigs.append(sig)
        bases.append(base)
        types.append(sigs.index(sig))
    return bases, types, sigs


def _na_bias_table(rpb, sigs):
    n_ri, n_ci = 2 * NA_WIN_R - 1, 2 * NA_WIN_C - 1
    csel = np.zeros((n_ci, NA_TILES, NA_QT, NA_KT), np.float32)
    for t in range(NA_TILES):
        qc = np.array(_NA_QCOLS[t])
        kc = np.concatenate([np.arange(c0, c0 + n) for c0, n in _NA_KSEGS[t]])
        cs = np.clip(qc - NA_WIN_C // 2, 0, GRID_W - NA_WIN_C)
        valid = (kc[None, :] >= cs[:, None]) & (kc[None, :] < cs[:, None] + NA_WIN_C)
        cidx = kc[None, :] - qc[:, None] + NA_WIN_C - 1
        qi, ki = np.nonzero(valid)
        csel[cidx[qi, ki], t, qi, ki] = 1.0
    cols = jnp.einsum("hrc,ctqk->htqrk", rpb.astype(F32), jnp.asarray(csel),
                      precision=lax.Precision.HIGHEST)
    cols = cols + jnp.asarray(np.where(csel.sum(0) > 0, 0.0, MASK_NEG)[None, :, :, None, :], F32)
    cols = cols.reshape(NA_HEADS // 2, 2, NA_TILES, NA_QT, n_ri * NA_KT)
    cols = jnp.transpose(cols, (0, 2, 1, 3, 4))
    tabs = []
    for sig in sigs:
        per_row = []
        for first, qrow in sig:
            ri0 = first - qrow + NA_WIN_R - 1
            after = NA_BAND_ROWS - first - NA_WIN_R
            window = cols[..., ri0 * NA_KT:(ri0 + NA_WIN_R) * NA_KT]
            per_row.append(jnp.pad(window, [(0, 0)] * 4 + [(first * NA_KT, after * NA_KT)],
                                   constant_values=MASK_NEG))
        tabs.append(jnp.stack(per_row, axis=2))
    return jnp.stack(tabs).reshape(len(sigs), NA_HEADS // 2, NA_TILES,
                                   NA_GROUP_ROWS * 2 * NA_QT, NA_BAND_ROWS * NA_KT)


def _natten_kernel(*refs, n_batch, rows, bases, types):
    for bb in range(n_batch):
        _natten_one(*refs, bb * rows * GRID_W, rows=rows, bases=bases, types=types)


def _natten_one(q_ref, k_ref, v_ref, bias_ref, o_ref, tok0, *, rows, bases, types):
    lane = lax.broadcasted_iota(jnp.int32, (NA_QT, V7X_LANES), 1)
    first_head = lane < NA_HEAD_DIM
    gr = NA_GROUP_ROWS

    def row_slice(grid_row):
        start = tok0 + grid_row * GRID_W
        if isinstance(start, int):
            return slice(start, start + GRID_W)
        return pl.ds(pl.multiple_of(start, GRID_W), GRID_W)

    def row_block(ref, grid_row):
        return ref[row_slice(grid_row), :]

    def band(blocks, t):
        return jnp.concatenate(
            [blk[c0:c0 + n] for blk in blocks for (c0, n) in _NA_KSEGS[t]], axis=0)

    def step(groups):
        scores = []
        for row0, base, y in groups:
            kblocks = [row_block(k_ref, base + u) for u in range(NA_BAND_ROWS)]
            qrows = [row_block(q_ref, row0 + j).astype(F32) for j in range(gr)]
            for t in range(NA_TILES):
                qc = _NA_QCOLS[t]
                q2 = []
                for j in range(gr):
                    if t == 0:
                        qt = jnp.concatenate([qrows[j][0:NA_EDGE], qrows[j][GRID_W - NA_EDGE:]], axis=0)
                    else:
                        qt = qrows[j][qc[0]:qc[0] + NA_QT]
                    q2 += [jnp.where(first_head, qt, 0.0), jnp.where(first_head, 0.0, qt)]
                s = _nt_dot(jnp.concatenate(q2, axis=0).astype(BF16), band(kblocks, t))
                scores.append(s + bias_ref[y, t])
        probs, sums = [], []
        for s in scores:
            p = jnp.exp(s - jnp.max(s, axis=-1, keepdims=True))
            sums.append(jnp.sum(p, axis=-1, keepdims=True))
            probs.append(p.astype(BF16))
        for g, (row0, base, y) in enumerate(groups):
            vblocks = [row_block(v_ref, base + u) for u in range(NA_BAND_ROWS)]
            o_tiles = []
            for t in range(NA_TILES):
                o2 = _dot(probs[g * NA_TILES + t], band(vblocks, t)) / sums[g * NA_TILES + t]
                o_tiles.append(o2)
            for j in range(gr):
                sel = [jnp.where(first_head, o[2 * NA_QT * j:2 * NA_QT * j + NA_QT],
                                 o[2 * NA_QT * j + NA_QT:2 * NA_QT * (j + 1)]) for o in o_tiles]
                orow = jnp.concatenate([sel[0][0:NA_EDGE], *sel[1:], sel[0][NA_EDGE:]], axis=0)
                o_ref[row_slice(row0 + j), :] = orow.astype(o_ref.dtype)

    n_groups = rows // gr
    shift = min(NA_WIN_R, rows) // 2
    regular = [bases[g] == g * gr - shift for g in range(n_groups)]
    lo = regular.index(True)
    hi = n_groups - regular[::-1].index(True)
    per = NA_GROUPS_PER_STEP
    assert all(regular[lo:hi]) and (hi - lo) % per == 0
    assert len({types[g] for g in range(lo, hi)}) == 1
    edge = [(g * gr, bases[g], types[g]) for g in list(range(lo)) + list(range(hi, n_groups))]
    for i in range(0, len(edge), per):
        step(edge[i:i + per])

    for g0 in range(lo, hi, per):
        step([((g0 + n) * gr, (g0 + n) * gr - shift, types[lo]) for n in range(per)])


def _natten(proj, rpb, batch, seq):
    rows = seq // GRID_W
    assert rows >= NA_BAND_ROWS and rows % NA_GROUP_ROWS == 0
    assert GRID_W == 64 and NA_HEAD_DIM * 2 == V7X_LANES
    pairs = NA_HEADS // 2
    bases, types, sigs = _na_group_plan(rows)
    bias = _na_bias_table(rpb, sigs)
    nb = NA_BATCH_PER_STEP
    assert batch % nb == 0
    col = lambda off: pl.BlockSpec((None, nb * seq, V7X_LANES),
                                   lambda p, b, off=off: (off + p, b, 0))
    return pl.pallas_call(
        functools.partial(_natten_kernel, n_batch=nb, rows=rows, bases=tuple(bases),
                          types=tuple(types)),
        grid=(pairs, batch // nb),
        in_specs=[
            col(0), col(pairs), col(2 * pairs),
            pl.BlockSpec((len(sigs), None, NA_TILES, NA_GROUP_ROWS * 2 * NA_QT,
                          NA_BAND_ROWS * NA_KT), lambda p, b: (0, p, 0, 0, 0)),
        ],
        out_specs=pl.BlockSpec((None, nb * seq, V7X_LANES), lambda p, b: (p, b, 0)),
        out_shape=jax.ShapeDtypeStruct((pairs, batch * seq, V7X_LANES), BF16),
        compiler_params=pltpu.CompilerParams(
            dimension_semantics=("arbitrary", "arbitrary"),
            vmem_limit_bytes=V7X_VMEM_LIMIT_BYTES),
        name="natten",
    )(proj, proj, proj, bias)


def _hg_tables():
    c = HG_CHUNK
    idx = np.arange(c)
    lower = (idx[None, :] <= idx[:, None]).astype(np.float32)
    cums = np.stack([np.concatenate([lower, lower], axis=1),
                     np.concatenate([lower.T, lower.T], axis=1)])
    masks = []
    for lvl in range(HG_LEVELS):
        s = 1 << lvl
        same_block = (idx[:, None] // (2 * s)) == (idx[None, :] // (2 * s))
        half = (idx // s) % 2
        masks.append(same_block & (half[:, None] != half[None, :]))
    masks.append(np.eye(c, dtype=bool))
    return jnp.asarray(cums, BF16), jnp.asarray(np.stack(masks), F32)


def _hgrn_kernel(*refs, n_batch, seq):
    for bb in range(n_batch):
        _hgrn_one(*refs, bb * seq, seq=seq)


def _static_loop(n, body, init):
    return body(0, init) if n == 1 else lax.fori_loop(0, n, body, init)


def _hgrn_one(q_ref, ff_ref, fb_ref, v_ref, lbf_ref, lbb_ref,
              cum_ref, mask_ref, o_ref, acc_s, qb_s, inc_s, etb_s, tok0, *, seq):
    c = HG_CHUNK
    n_chunks = seq // c
    n_step = HG_CHUNKS_PER_STEP
    assert n_chunks % n_step == 0
    grp = V7X_SUBLANES
    sub = lax.broadcasted_iota(jnp.int32, (c, V7X_LANES), 0) & (grp - 1)

    def lower_bound(lb_ref):
        a = lb_ref[...].astype(F32)
        a0, a1 = a[0:1], a[1:2]
        m = jnp.maximum(a0, a1)
        e0, e1 = jnp.exp(a0 - m), jnp.exp(a1 - m)
        return e0 / (e0 + e1)

    lbf = lower_bound(lbf_ref)
    lbb = lower_bound(lbb_ref)

    def aligned(start, align, size):
        if isinstance(start, int):
            return slice(start, start + size)
        return pl.ds(pl.multiple_of(start, align), size)

    def rows(ci, tok=0):
        return aligned(tok + ci * c, c, c)

    def group_rows(x, r):
        return jnp.concatenate(
            [jnp.broadcast_to(x[grp * j + r:grp * j + r + 1], (grp, V7X_LANES))
             for j in range(c // grp)], axis=0)

    def repeat_group(xg, n_rows):
        return xg if n_rows == grp else jnp.concatenate([xg] * (n_rows // grp), axis=0)

    def boundary_rows(x, s, offset):
        parts = [group_rows(x, 2 * s * blk + s - 1 + offset) for blk in range(grp // (2 * s))]
        assert len(parts) <= 2
        return parts[0] if len(parts) == 1 else jnp.where(sub < grp // 2, parts[0], parts[1])

    def gates(f_ref, sl, lb):
        hf = f_ref[sl, :].astype(F32)
        f = lb + (1.0 - lb) * (1.0 / (1.0 + jnp.exp(-hf)))
        hi, lo = _split_bf16(jnp.log2(f))
        return f, 1.0 - f, jnp.concatenate([hi, lo], axis=0)

    def stage_gates(ci):
        src = rows(ci, tok0)
        st = dict(sl=rows(ci), q=q_ref[src, :].astype(F32), v=v_ref[src, :])
        st["f_f"], st["k_f"], lf_f = gates(ff_ref, src, lbf)
        st["f_b"], st["k_b"], lf_b = gates(fb_ref, src, lbb)
        st["b_f"] = _dot(cum_ref[0], lf_f)
        st["b_b"] = _dot(cum_ref[1], lf_b)
        return st

    def level_operands(st, lvl):
        q, k_f, k_b, b_f, b_b = st["q"], st["k_f"], st["k_b"], st["b_f"], st["b_b"]
        if lvl == HG_LEVELS:
            return q, k_f + k_b
        if lvl == 0:
            odd = (sub & 1) == 1
            return q * jnp.where(odd, st["f_f"], st["f_b"]), jnp.where(odd, k_b, k_f)
        if (1 << lvl) < grp:
            later = (sub & (1 << lvl)) != 0
            d_f = b_f - boundary_rows(b_f, 1 << lvl, 0)
            d_b = b_b - boundary_rows(b_b, 1 << lvl, 1)
            return (q * jnp.exp2(jnp.where(later, d_f, d_b)),
                    jnp.where(later, k_b, k_f) * jnp.exp2(-jnp.where(later, d_b, d_f)))
        s = 1 << lvl
        xp, yp = [], []
        for r0 in range(0, c, 2 * s):
            m = r0 + s
            rf = repeat_group(st["last_f"][m - grp:m], s)
            rb = repeat_group(st["first_b"][m:m + grp], s)
            early, late = slice(r0, m), slice(m, m + s)
            xp += [q[early] * jnp.exp2(b_b[early] - rb), q[late] * jnp.exp2(b_f[late] - rf)]
            yp += [k_f[early] * jnp.exp2(rf - b_f[early]), k_b[late] * jnp.exp2(rb - b_b[late])]
        return jnp.concatenate(xp, axis=0), jnp.concatenate(yp, axis=0)

    def stage_operands(ci, st):
        q, k_f, k_b, b_f, b_b = st["q"], st["k_f"], st["k_b"], st["b_f"], st["b_b"]
        last_f = st["last_f"] = group_rows(b_f, grp - 1)
        first_b = st["first_b"] = group_rows(b_b, 0)
        sl = st["sl"]
        tot_f = last_f[c - grp:c]
        tot_b = first_b[0:grp]
        st["qe_f"] = (q * jnp.exp2(b_f)).astype(BF16)
        st["kd_f"] = (k_f * jnp.exp2(repeat_group(tot_f, c) - b_f)).astype(BF16)
        st["tot_f"] = jnp.exp2(tot_f[0:1])
        qb_s[sl, :] = (q * jnp.exp2(b_b)).astype(BF16)
        st["kd_b"] = (k_b * jnp.exp2(repeat_group(tot_b, c) - b_b)).astype(BF16)
        etb_s[aligned(ci * grp, grp, grp), :] = jnp.exp2(tot_b)
        return st

    def fwd_body(i, state_t):
        cis = [i * n_step + j for j in range(n_step)]
        sts = [stage_gates(ci) for ci in cis]
        sts = [stage_operands(ci, st) for ci, st in zip(cis, sts)]
        for st in sts:
            for lvl in range(HG_LEVELS + 1):
                x, y = level_operands(st, lvl)
                term = mask_ref[lvl] * _nt_dot(x.astype(BF16), y.astype(BF16))
                st["a"] = term if lvl == 0 else st["a"] + term
        intra = [_dot(st["a"].astype(BF16), st["v"]) for st in sts]
        incs = [_dot(st["v"].T, jnp.concatenate([st["kd_f"], st["kd_b"]], axis=1)) for st in sts]
        for st, o, inc in zip(sts, intra, incs):
            acc_s[st["sl"], :] = o + _nt_dot(st["qe_f"], state_t.astype(BF16))
            state_t = state_t * st["tot_f"] + inc[:, 0:HG_DK]
            inc_s[st["sl"], :] = inc[:, HG_DK:2 * HG_DK]
        return state_t

    zero = jnp.zeros((HG_DK, HG_DK), F32)
    _static_loop(n_chunks // n_step, fwd_body, zero)

    n_out = HG_OUT_CHUNKS_PER_STEP
    assert n_chunks % n_out == 0

    def bwd_body(i, state_t):
        cis = [n_chunks - 1 - (i * n_out + j) for j in range(n_out)]
        inter = []
        for ci in cis:
            inter.append(_nt_dot(qb_s[rows(ci), :], state_t.astype(BF16)))
            tot_b = etb_s[aligned(ci * grp, grp, 1), :]
            state_t = state_t * tot_b + inc_s[rows(ci), :]
        for ci, o_inter in zip(cis, inter):
            o_ref[rows(ci, tok0), :] = (acc_s[rows(ci), :] + o_inter).astype(o_ref.dtype)
        return state_t

    _static_loop(n_chunks // n_out, bwd_body, zero)


def _hgrn2(proj, lb_fwd, lb_bwd, batch, seq):
    assert HG_DK == V7X_LANES and lb_fwd.shape[0] == 2
    cums, masks = _hg_tables()
    first = 3 * NA_WIDTH // V7X_LANES
    nb = HG_BATCH_PER_STEP
    assert batch % nb == 0
    col = lambda k: pl.BlockSpec((None, nb * seq, V7X_LANES),
                                 lambda b, h, k=k: (first + k * HG_HEADS + h, b, 0))
    lbs = pl.BlockSpec((2, V7X_LANES), lambda b, h: (0, h))
    const = lambda a: pl.BlockSpec(a.shape, lambda b, h, nd=a.ndim: (0,) * nd)
    return pl.pallas_call(
        functools.partial(_hgrn_kernel, n_batch=nb, seq=seq),
        grid=(batch // nb, HG_HEADS),
        in_specs=[col(0), col(1), col(2), col(3), lbs, lbs, const(cums), const(masks)],
        out_specs=pl.BlockSpec((None, nb * seq, V7X_LANES), lambda b, h: (h, b, 0)),
        out_shape=jax.ShapeDtypeStruct((HG_HEADS, batch * seq, V7X_LANES), BF16),
        scratch_shapes=[pltpu.VMEM((seq, HG_DK), F32), pltpu.VMEM((seq, HG_DK), BF16),
                        pltpu.VMEM((seq, HG_DK), F32),
                        pltpu.VMEM((V7X_SUBLANES * seq // HG_CHUNK, HG_DK), F32)],
        compiler_params=pltpu.CompilerParams(
            dimension_semantics=("arbitrary", "arbitrary"),
            vmem_limit_bytes=V7X_VMEM_LIMIT_BYTES),
        name="hgrn2",
    )(proj, proj, proj, proj, lb_fwd.astype(F32), lb_bwd.astype(F32), cums, masks)


def _out_mlp_kernel(x_ref, na_ref, hg_ref, g_ref, gw_ref, wo_na_ref, wo_hg_ref, nw_ref, wu_ref,
                    wd_ref, o_ref):
    def recurrence_head(j):
        o = hg_ref[j].astype(F32)
        ms = jnp.mean(o * o, axis=-1, keepdims=True)
        g = g_ref[j].astype(F32)
        return (o * lax.rsqrt(ms + RMS_EPS) * gw_ref[...]
                * (g * (1.0 / (1.0 + jnp.exp(-g))))).astype(BF16)

    y_na = jnp.concatenate([na_ref[j] for j in range(na_ref.shape[0])], axis=1)
    y_hg = jnp.concatenate([recurrence_head(j) for j in range(HG_HEADS)], axis=1)
    h = x_ref[...] + _dot(y_na, wo_na_ref[...]) + _dot(y_hg, wo_hg_ref[...])
    ms = jnp.mean(h * h, axis=-1, keepdims=True)
    u = (h * lax.rsqrt(ms + RMS_EPS) * nw_ref[...]).astype(BF16)
    mlp = None
    for j in range(D_FF // FF_CHUNK):
        cols = slice(j * FF_CHUNK, (j + 1) * FF_CHUNK)
        a = jnp.maximum(_dot(u, wu_ref[:, cols]), 0.0)
        part = _dot((a * a).astype(BF16), wd_ref[cols, :])
        mlp = part if mlp is None else mlp + part
    o_ref[...] = h + mlp


def _out_mlp(x2, y_na, o_hg, proj, hg_norm_w, w_out, norm_w, w_up, w_down):
    m = x2.shape[0]
    first_g = (3 * NA_WIDTH + 4 * HG_WIDTH) // V7X_LANES
    assert first_g % HG_HEADS == 0
    gate = pl.BlockSpec((HG_HEADS, TOKEN_TILE, V7X_LANES), lambda i: (first_g // HG_HEADS, i, 0))
    gw = hg_norm_w.astype(F32).reshape(1, HG_DK)
    tile = lambda width: pl.BlockSpec((TOKEN_TILE, width), lambda i: (i, 0))
    blocks = lambda a: pl.BlockSpec((a.shape[0], TOKEN_TILE, V7X_LANES), lambda i: (0, i, 0))
    resident = lambda a: pl.BlockSpec(a.shape, lambda i: (0, 0), pipeline_mode=pl.Buffered(1))
    wo_na, wo_hg = w_out[:NA_WIDTH], w_out[NA_WIDTH:]
    return pl.pallas_call(
        _out_mlp_kernel,
        grid=(m // TOKEN_TILE,),
        in_specs=[tile(D_MODEL), blocks(y_na), blocks(o_hg), gate,
                  pl.BlockSpec((1, HG_DK), lambda i: (0, 0)), resident(wo_na),
                  resident(wo_hg), pl.BlockSpec((1, D_MODEL), lambda i: (0, 0)),
                  resident(w_up), resident(w_down)],
        out_specs=tile(D_MODEL),
        out_shape=jax.ShapeDtypeStruct((m, D_MODEL), F32),
        compiler_params=pltpu.CompilerParams(
            dimension_semantics=("arbitrary",), vmem_limit_bytes=V7X_VMEM_LIMIT_BYTES),
        name="out_mlp",
    )(x2, y_na, o_hg, proj, gw, wo_na, wo_hg, norm_w, w_up, w_down)


def kernel(x, w_in, w_out, attn_norm_w, mlp_norm_w, q_norm_w, k_norm_w, rpb, hg_norm_w,
           lb_fwd, lb_bwd, w_up, w_down):
    batch, seq, d = x.shape
    assert d == D_MODEL and w_in.shape[0] == 1 and seq % GRID_W == 0 and seq % HG_CHUNK == 0
    x2 = x.reshape(batch * seq, d)
    proj = _in_proj(x2, attn_norm_w[0].astype(F32).reshape(1, d), w_in[0].astype(BF16),
                    q_norm_w[0], k_norm_w[0])
    y_na = _natten(proj, rpb[0], batch, seq)
    o_hg = _hgrn2(proj, lb_fwd, lb_bwd, batch, seq)
    out = _out_mlp(x2, y_na, o_hg, proj, hg_norm_w[0], w_out[0].astype(BF16),
                   mlp_norm_w[0].astype(F32).reshape(1, d),
                   w_up[0].astype(BF16), w_down[0].astype(BF16))
    return out.reshape(batch, seq, d)
```

```python
import functools

import numpy as np
import jax
import jax.numpy as jnp
from jax import lax
from jax.experimental import pallas as pl
from jax.experimental.pallas import tpu as pltpu

F32 = jnp.float32
BF16 = jnp.bfloat16

D_MODEL = 1024
GRID_W = 64
NA_HEADS = 8
NA_HEAD_DIM = 64
NA_WIDTH = NA_HEADS * NA_HEAD_DIM
NA_WIN_R = 8
NA_WIN_C = 16
HG_HEADS = 4
HG_DK = 128
HG_WIDTH = HG_HEADS * HG_DK
D_FF = 4 * D_MODEL
D_IN_PROJ = 3 * NA_WIDTH + 5 * HG_WIDTH
RMS_EPS = 1e-6

V7X_LANES = 128
V7X_SUBLANES = 8
V7X_VMEM_LIMIT_BYTES = 56 * 1024 * 1024

TOKEN_TILE = 1024
FF_CHUNK = 1024
HG_CHUNK = 128
HG_LEVELS = 7
HG_CHUNKS_PER_STEP = 16
HG_OUT_CHUNKS_PER_STEP = 16
NA_QT = 16
NA_KT = 32
NA_TILES = GRID_W // NA_QT
NA_GROUP_ROWS = 4
NA_BAND_ROWS = 12
NA_GROUPS_PER_STEP = 2
NA_BATCH_PER_STEP = 2
HG_BATCH_PER_STEP = 2
MASK_NEG = -1e30

NA_EDGE = NA_WIN_C // 2
_NA_QCOLS = [list(range(0, NA_EDGE)) + list(range(GRID_W - NA_EDGE, GRID_W))] + [
    list(range(NA_EDGE + NA_QT * t, NA_EDGE + NA_QT * (t + 1))) for t in range(NA_TILES - 1)]
_NA_KSEGS = [[(0, NA_WIN_C), (GRID_W - NA_WIN_C, NA_WIN_C)]] + [
    [(NA_QT * t, NA_KT)] for t in range(NA_TILES - 1)]


def _nt_dot(a, b):
    return lax.dot_general(a, b, (((1,), (1,)), ((), ())), preferred_element_type=F32)


def _dot(a, b):
    return jnp.dot(a, b, preferred_element_type=F32)


def _split_bf16(x):
    hi = x.astype(BF16)
    lo = (x - hi.astype(F32)).astype(BF16)
    return hi, lo


def _in_proj_kernel(x_ref, nw_ref, w_ref, qkw_ref, o_ref):
    x = x_ref[...]
    ms = jnp.mean(x * x, axis=-1, keepdims=True)
    u = (x * lax.rsqrt(ms + RMS_EPS) * nw_ref[...]).astype(BF16)
    res = _dot(u, w_ref[...])
    first_head = lax.broadcasted_iota(jnp.int32, (1, V7X_LANES), 1) < NA_HEAD_DIM
    for j in range(D_IN_PROJ // V7X_LANES):
        blk = res[:, j * V7X_LANES:(j + 1) * V7X_LANES]
        if j < qkw_ref.shape[0]:
            sq = blk * blk
            s_all = jnp.sum(sq, axis=-1, keepdims=True)
            s_one = jnp.sum(jnp.where(first_head, sq, 0.0), axis=-1, keepdims=True)
            ms_h = jnp.where(first_head, s_one, s_all - s_one) * (1.0 / NA_HEAD_DIM)
            blk = blk * lax.rsqrt(ms_h + RMS_EPS) * qkw_ref[j]
        o_ref[j] = blk.astype(o_ref.dtype)


def _in_proj(x2, norm_w, w_in, q_norm_w, k_norm_w):
    m = x2.shape[0]
    pair_w = lambda w, mult: jnp.tile(w.astype(F32) * mult, 2).reshape(1, 1, V7X_LANES)
    n_blk = NA_WIDTH // V7X_LANES
    qkw = jnp.concatenate([jnp.tile(pair_w(q_norm_w, NA_HEAD_DIM ** -0.5), (n_blk, 1, 1)),
                           jnp.tile(pair_w(k_norm_w, 1.0), (n_blk, 1, 1))])
    return pl.pallas_call(
        _in_proj_kernel,
        grid=(m // TOKEN_TILE,),
        in_specs=[
            pl.BlockSpec((TOKEN_TILE, D_MODEL), lambda i: (i, 0)),
            pl.BlockSpec((1, D_MODEL), lambda i: (0, 0)),
            pl.BlockSpec((D_MODEL, D_IN_PROJ), lambda i: (0, 0), pipeline_mode=pl.Buffered(1)),
            pl.BlockSpec(qkw.shape, lambda i: (0, 0, 0)),
        ],
        out_specs=pl.BlockSpec((D_IN_PROJ // V7X_LANES, TOKEN_TILE, V7X_LANES), lambda i: (0, i, 0)),
        out_shape=jax.ShapeDtypeStruct((D_IN_PROJ // V7X_LANES, m, V7X_LANES), BF16),
        compiler_params=pltpu.CompilerParams(
            dimension_semantics=("arbitrary",), vmem_limit_bytes=V7X_VMEM_LIMIT_BYTES),
        name="in_proj",
    )(x2, norm_w, w_in, qkw)


def _na_group_plan(rows):
    wr = min(NA_WIN_R, rows)
    bases, types, sigs = [], [], []
    for g in range(rows // NA_GROUP_ROWS):
        r0 = g * NA_GROUP_ROWS
        starts = [min(max(r0 + j - wr // 2, 0), rows - wr) for j in range(NA_GROUP_ROWS)]
        base = min(max(starts[0], 0), rows - NA_BAND_ROWS)
        assert base <= starts[0] and starts[-1] + wr <= base + NA_BAND_ROWS
        sig = tuple((starts[j] - base, r0 + j - base) for j in range(NA_GROUP_ROWS))
        if sig not in sigs:
            sigs.append(sig)
        bases.append(base)
        types.append(sigs.index(sig))
    return bases, types, sigs


def _na_bias_table(rpb, sigs):
    n_ri, n_ci = 2 * NA_WIN_R - 1, 2 * NA_WIN_C - 1
    csel = np.zeros((n_ci, NA_TILES, NA_QT, NA_KT), np.float32)
    for t in range(NA_TILES):
        qc = np.array(_NA_QCOLS[t])
        kc = np.concatenate([np.arange(c0, c0 + n) for c0, n in _NA_KSEGS[t]])
        cs = np.clip(qc - NA_WIN_C // 2, 0, GRID_W - NA_WIN_C)
        valid = (kc[None, :] >= cs[:, None]) & (kc[None, :] < cs[:, None] + NA_WIN_C)
        cidx = kc[None, :] - qc[:, None] + NA_WIN_C - 1
        qi, ki = np.nonzero(valid)
        csel[cidx[qi, ki], t, qi, ki] = 1.0
    cols = jnp.einsum("hrc,ctqk->htqrk", rpb.astype(F32), jnp.asarray(csel),
                      precision=lax.Precision.HIGHEST)
    cols = cols + jnp.asarray(np.where(csel.sum(0) > 0, 0.0, MASK_NEG)[None, :, :, None, :], F32)
    cols = cols.reshape(NA_HEADS // 2, 2, NA_TILES, NA_QT, n_ri * NA_KT)
    cols = jnp.transpose(cols, (0, 2, 1, 3, 4))
    tabs = []
    for sig in sigs:
        per_row = []
        for first, qrow in sig:
            ri0 = first - qrow + NA_WIN_R - 1
            after = NA_BAND_ROWS - first - NA_WIN_R
            window = cols[..., ri0 * NA_KT:(ri0 + NA_WIN_R) * NA_KT]
            per_row.append(jnp.pad(window, [(0, 0)] * 4 + [(first * NA_KT, after * NA_KT)],
                                   constant_values=MASK_NEG))
        tabs.append(jnp.stack(per_row, axis=2))
    return jnp.stack(tabs).reshape(len(sigs), NA_HEADS // 2, NA_TILES,
                                   NA_GROUP_ROWS * 2 * NA_QT, NA_BAND_ROWS * NA_KT)


def _natten_kernel(*refs, n_batch, rows, bases, types):
    for bb in range(n_batch):
        _natten_one(*refs, bb * rows * GRID_W, rows=rows, bases=bases, types=types)


def _natten_one(q_ref, k_ref, v_ref, bias_ref, o_ref, tok0, *, rows, bases, types):
    lane = lax.broadcasted_iota(jnp.int32, (NA_QT, V7X_LANES), 1)
    first_head = lane < NA_HEAD_DIM
    gr = NA_GROUP_ROWS

    def row_slice(grid_row):
        start = tok0 + grid_row * GRID_W
        if isinstance(start, int):
            return slice(start, start + GRID_W)
        return pl.ds(pl.multiple_of(start, GRID_W), GRID_W)

    def row_block(ref, grid_row):
        return ref[row_slice(grid_row), :]

    def band(blocks, t):
        return jnp.concatenate(
            [blk[c0:c0 + n] for blk in blocks for (c0, n) in _NA_KSEGS[t]], axis=0)

    def step(groups):
        scores = []
        for row0, base, y in groups:
            kblocks = [row_block(k_ref, base + u) for u in range(NA_BAND_ROWS)]
            qrows = [row_block(q_ref, row0 + j).astype(F32) for j in range(gr)]
            for t in range(NA_TILES):
                qc = _NA_QCOLS[t]
                q2 = []
                for j in range(gr):
                    if t == 0:
                        qt = jnp.concatenate([qrows[j][0:NA_EDGE], qrows[j][GRID_W - NA_EDGE:]], axis=0)
                    else:
                        qt = qrows[j][qc[0]:qc[0] + NA_QT]
                    q2 += [jnp.where(first_head, qt, 0.0), jnp.where(first_head, 0.0, qt)]
                s = _nt_dot(jnp.concatenate(q2, axis=0).astype(BF16), band(kblocks, t))
                scores.append(s + bias_ref[y, t])
        probs, sums = [], []
        for s in scores:
            p = jnp.exp(s - jnp.max(s, axis=-1, keepdims=True))
            sums.append(jnp.sum(p, axis=-1, keepdims=True))
            probs.append(p.astype(BF16))
        for g, (row0, base, y) in enumerate(groups):
            vblocks = [row_block(v_ref, base + u) for u in range(NA_BAND_ROWS)]
            o_tiles = []
            for t in range(NA_TILES):
                o2 = _dot(probs[g * NA_TILES + t], band(vblocks, t)) / sums[g * NA_TILES + t]
                o_tiles.append(o2)
            for j in range(gr):
                sel = [jnp.where(first_head, o[2 * NA_QT * j:2 * NA_QT * j + NA_QT],
                                 o[2 * NA_QT * j + NA_QT:2 * NA_QT * (j + 1)]) for o in o_tiles]
                orow = jnp.concatenate([sel[0][0:NA_EDGE], *sel[1:], sel[0][NA_EDGE:]], axis=0)
                o_ref[row_slice(row0 + j), :] = orow.astype(o_ref.dtype)

    n_groups = rows // gr
    shift = min(NA_WIN_R, rows) // 2
    regular = [bases[g] == g * gr - shift for g in range(n_groups)]
    lo = regular.index(True)
    hi = n_groups - regular[::-1].index(True)
    per = NA_GROUPS_PER_STEP
    assert all(regular[lo:hi]) and (hi - lo) % per == 0
    assert len({types[g] for g in range(lo, hi)}) == 1
    edge = [(g * gr, bases[g], types[g]) for g in list(range(lo)) + list(range(hi, n_groups))]
    for i in range(0, len(edge), per):
        step(edge[i:i + per])

    for g0 in range(lo, hi, per):
        step([((g0 + n) * gr, (g0 + n) * gr - shift, types[lo]) for n in range(per)])


def _natten(proj, rpb, batch, seq):
    rows = seq // GRID_W
    assert rows >= NA_BAND_ROWS and rows % NA_GROUP_ROWS == 0
    assert GRID_W == 64 and NA_HEAD_DIM * 2 == V7X_LANES
    pairs = NA_HEADS // 2
    bases, types, sigs = _na_group_plan(rows)
    bias = _na_bias_table(rpb, sigs)
    nb = NA_BATCH_PER_STEP
    assert batch % nb == 0
    col = lambda off: pl.BlockSpec((None, nb * seq, V7X_LANES),
                                   lambda p, b, off=off: (off + p, b, 0))
    return pl.pallas_call(
        functools.partial(_natten_kernel, n_batch=nb, rows=rows, bases=tuple(bases),
                          types=tuple(types)),
        grid=(pairs, batch // nb),
        in_specs=[
            col(0), col(pairs), col(2 * pairs),
            pl.BlockSpec((len(sigs), None, NA_TILES, NA_GROUP_ROWS * 2 * NA_QT,
                          NA_BAND_ROWS * NA_KT), lambda p, b: (0, p, 0, 0, 0)),
        ],
        out_specs=pl.BlockSpec((None, nb * seq, V7X_LANES), lambda p, b: (p, b, 0)),
        out_shape=jax.ShapeDtypeStruct((pairs, batch * seq, V7X_LANES), BF16),
        compiler_params=pltpu.CompilerParams(
            dimension_semantics=("arbitrary", "arbitrary"),
            vmem_limit_bytes=V7X_VMEM_LIMIT_BYTES),
        name="natten",
    )(proj, proj, proj, bias)


def _hg_tables():
    c = HG_CHUNK
    idx = np.arange(c)
    lower = (idx[None, :] <= idx[:, None]).astype(np.float32)
    cums = np.stack([np.concatenate([lower, lower], axis=1),
                     np.concatenate([lower.T, lower.T], axis=1)])
    masks = []
    for lvl in range(HG_LEVELS):
        s = 1 << lvl
        same_block = (idx[:, None] // (2 * s)) == (idx[None, :] // (2 * s))
        half = (idx // s) % 2
        masks.append(same_block & (half[:, None] != half[None, :]))
    masks.append(np.eye(c, dtype=bool))
    return jnp.asarray(cums, BF16), jnp.asarray(np.stack(masks), F32)


def _hgrn_kernel(*refs, n_batch, seq):
    for bb in range(n_batch):
        _hgrn_one(*refs, bb * seq, seq=seq)


def _hgrn_one(q_ref, ff_ref, fb_ref, v_ref, lbf_ref, lbb_ref,
              cum_ref, mask_ref, o_ref, acc_s, qb_s, inc_s, etb_s, tok0, *, seq):
    c = HG_CHUNK
    n_chunks = seq // c
    n_step = HG_CHUNKS_PER_STEP
    assert n_chunks % n_step == 0
    grp = V7X_SUBLANES
    sub = lax.broadcasted_iota(jnp.int32, (c, V7X_LANES), 0) & (grp - 1)

    def lower_bound(lb_ref):
        a = lb_ref[...].astype(F32)
        a0, a1 = a[0:1], a[1:2]
        m = jnp.maximum(a0, a1)
        e0, e1 = jnp.exp(a0 - m), jnp.exp(a1 - m)
        return e0 / (e0 + e1)

    lbf = lower_bound(lbf_ref)
    lbb = lower_bound(lbb_ref)

    def rows(ci, tok=0):
        return pl.ds(pl.multiple_of(tok + ci * c, c), c)

    def group_rows(x, r):
        return jnp.concatenate(
            [jnp.broadcast_to(x[grp * j + r:grp * j + r + 1], (grp, V7X_LANES))
             for j in range(c // grp)], axis=0)

    def repeat_group(xg, n_rows):
        return xg if n_rows == grp else jnp.concatenate([xg] * (n_rows // grp), axis=0)

    def boundary_rows(x, s, offset):
        parts = [group_rows(x, 2 * s * blk + s - 1 + offset) for blk in range(grp // (2 * s))]
        assert len(parts) <= 2
        return parts[0] if len(parts) == 1 else jnp.where(sub < grp // 2, parts[0], parts[1])

    def gates(f_ref, sl, lb):
        hf = f_ref[sl, :].astype(F32)
        f = lb + (1.0 - lb) * (1.0 / (1.0 + jnp.exp(-hf)))
        hi, lo = _split_bf16(jnp.log2(f))
        return f, 1.0 - f, jnp.concatenate([hi, lo], axis=0)

    def stage_gates(ci):
        src = rows(ci, tok0)
        st = dict(sl=rows(ci), q=q_ref[src, :].astype(F32), v=v_ref[src, :])
        st["f_f"], st["k_f"], lf_f = gates(ff_ref, src, lbf)
        st["f_b"], st["k_b"], lf_b = gates(fb_ref, src, lbb)
        st["b_f"] = _dot(cum_ref[0], lf_f)
        st["b_b"] = _dot(cum_ref[1], lf_b)
        return st

    def level_operands(st, lvl):
        q, k_f, k_b, b_f, b_b = st["q"], st["k_f"], st["k_b"], st["b_f"], st["b_b"]
        if lvl == HG_LEVELS:
            return q, k_f + k_b
        if lvl == 0:
            odd = (sub & 1) == 1
            return q * jnp.where(odd, st["f_f"], st["f_b"]), jnp.where(odd, k_b, k_f)
        if (1 << lvl) < grp:
            later = (sub & (1 << lvl)) != 0
            d_f = b_f - boundary_rows(b_f, 1 << lvl, 0)
            d_b = b_b - boundary_rows(b_b, 1 << lvl, 1)
            return (q * jnp.exp2(jnp.where(later, d_f, d_b)),
                    jnp.where(later, k_b, k_f) * jnp.exp2(-jnp.where(later, d_b, d_f)))
        s = 1 << lvl
        xp, yp = [], []
        for r0 in range(0, c, 2 * s):
            m = r0 + s
            rf = repeat_group(st["last_f"][m - grp:m], s)
            rb = repeat_group(st["first_b"][m:m + grp], s)
            early, late = slice(r0, m), slice(m, m + s)
            xp += [q[early] * jnp.exp2(b_b[early] - rb), q[late] * jnp.exp2(b_f[late] - rf)]
            yp += [k_f[early] * jnp.exp2(rf - b_f[early]), k_b[late] * jnp.exp2(rb - b_b[late])]
        return jnp.concatenate(xp, axis=0), jnp.concatenate(yp, axis=0)

    def stage_operands(ci, st):
        q, k_f, k_b, b_f, b_b = st["q"], st["k_f"], st["k_b"], st["b_f"], st["b_b"]
        last_f = st["last_f"] = group_rows(b_f, grp - 1)
        first_b = st["first_b"] = group_rows(b_b, 0)
        sl = st["sl"]
        tot_f = last_f[c - grp:c]
        tot_b = first_b[0:grp]
        st["qe_f"] = (q * jnp.exp2(b_f)).astype(BF16)
        st["kd_f"] = (k_f * jnp.exp2(repeat_group(tot_f, c) - b_f)).astype(BF16)
        st["tot_f"] = jnp.exp2(tot_f[0:1])
        qb_s[sl, :] = (q * jnp.exp2(b_b)).astype(BF16)
        st["kd_b"] = (k_b * jnp.exp2(repeat_group(tot_b, c) - b_b)).astype(BF16)
        etb_s[pl.ds(pl.multiple_of(ci * grp, grp), grp), :] = jnp.exp2(tot_b)
        return st

    def fwd_body(i, state_t):
        cis = [i * n_step + j for j in range(n_step)]
        sts = [stage_gates(ci) for ci in cis]
        sts = [stage_operands(ci, st) for ci, st in zip(cis, sts)]
        for st in sts:
            for lvl in range(HG_LEVELS + 1):
                x, y = level_operands(st, lvl)
                term = mask_ref[lvl] * _nt_dot(x.astype(BF16), y.astype(BF16))
                st["a"] = term if lvl == 0 else st["a"] + term
        intra = [_dot(st["a"].astype(BF16), st["v"]) for st in sts]
        incs = [_dot(st["v"].T, jnp.concatenate([st["kd_f"], st["kd_b"]], axis=1)) for st in sts]
        for st, o, inc in zip(sts, intra, incs):
            acc_s[st["sl"], :] = o + _nt_dot(st["qe_f"], state_t.astype(BF16))
            state_t = state_t * st["tot_f"] + inc[:, 0:HG_DK]
            inc_s[st["sl"], :] = inc[:, HG_DK:2 * HG_DK]
        return state_t

    zero = jnp.zeros((HG_DK, HG_DK), F32)
    lax.fori_loop(0, n_chunks // n_step, fwd_body, zero)

    n_out = HG_OUT_CHUNKS_PER_STEP
    assert n_chunks % n_out == 0

    def bwd_body(i, state_t):
        cis = [n_chunks - 1 - (i * n_out + j) for j in range(n_out)]
        inter = []
        for ci in cis:
            inter.append(_nt_dot(qb_s[rows(ci), :], state_t.astype(BF16)))
            tot_b = etb_s[pl.ds(pl.multiple_of(ci * grp, grp), 1), :]
            state_t = state_t * tot_b + inc_s[rows(ci), :]
        for ci, o_inter in zip(cis, inter):
            o_ref[rows(ci, tok0), :] = (acc_s[rows(ci), :] + o_inter).astype(o_ref.dtype)
        return state_t

    lax.fori_loop(0, n_chunks // n_out, bwd_body, zero)


def _hgrn2(proj, lb_fwd, lb_bwd, batch, seq):
    assert HG_DK == V7X_LANES and lb_fwd.shape[0] == 2
    cums, masks = _hg_tables()
    first = 3 * NA_WIDTH // V7X_LANES
    nb = HG_BATCH_PER_STEP
    assert batch % nb == 0
    col = lambda k: pl.BlockSpec((None, nb * seq, V7X_LANES),
                                 lambda b, h, k=k: (first + k * HG_HEADS + h, b, 0))
    lbs = pl.BlockSpec((2, V7X_LANES), lambda b, h: (0, h))
    const = lambda a: pl.BlockSpec(a.shape, lambda b, h, nd=a.ndim: (0,) * nd)
    return pl.pallas_call(
        functools.partial(_hgrn_kernel, n_batch=nb, seq=seq),
        grid=(batch // nb, HG_HEADS),
        in_specs=[col(0), col(1), col(2), col(3), lbs, lbs, const(cums), const(masks)],
        out_specs=pl.BlockSpec((None, nb * seq, V7X_LANES), lambda b, h: (h, b, 0)),
        out_shape=jax.ShapeDtypeStruct((HG_HEADS, batch * seq, V7X_LANES), BF16),
        scratch_shapes=[pltpu.VMEM((seq, HG_DK), F32), pltpu.VMEM((seq, HG_DK), BF16),
                        pltpu.VMEM((seq, HG_DK), F32),
                        pltpu.VMEM((V7X_SUBLANES * seq // HG_CHUNK, HG_DK), F32)],
        compiler_params=pltpu.CompilerParams(
            dimension_semantics=("arbitrary", "arbitrary"),
            vmem_limit_bytes=V7X_VMEM_LIMIT_BYTES),
        name="hgrn2",
    )(proj, proj, proj, proj, lb_fwd.astype(F32), lb_bwd.astype(F32), cums, masks)


def _out_mlp_kernel(x_ref, na_ref, hg_ref, g_ref, gw_ref, wo_na_ref, wo_hg_ref, nw_ref, wu_ref,
                    wd_ref, o_ref):
    def recurrence_head(j):
        o = hg_ref[j].astype(F32)
        ms = jnp.mean(o * o, axis=-1, keepdims=True)
        g = g_ref[j].astype(F32)
        return (o * lax.rsqrt(ms + RMS_EPS) * gw_ref[...]
                * (g * (1.0 / (1.0 + jnp.exp(-g))))).astype(BF16)

    y_na = jnp.concatenate([na_ref[j] for j in range(na_ref.shape[0])], axis=1)
    y_hg = jnp.concatenate([recurrence_head(j) for j in range(HG_HEADS)], axis=1)
    h = x_ref[...] + _dot(y_na, wo_na_ref[...]) + _dot(y_hg, wo_hg_ref[...])
    ms = jnp.mean(h * h, axis=-1, keepdims=True)
    u = (h * lax.rsqrt(ms + RMS_EPS) * nw_ref[...]).astype(BF16)
    mlp = None
    for j in range(D_FF // FF_CHUNK):
        cols = slice(j * FF_CHUNK, (j + 1) * FF_CHUNK)
        a = jnp.maximum(_dot(u, wu_ref[:, cols]), 0.0)
        part = _dot((a * a).astype(BF16), wd_ref[cols, :])
        mlp = part if mlp is None else mlp + part
    o_ref[...] = h + mlp


def _out_mlp(x2, y_na, o_hg, proj, hg_norm_w, w_out, norm_w, w_up, w_down):
    m = x2.shape[0]
    first_g = (3 * NA_WIDTH + 4 * HG_WIDTH) // V7X_LANES
    assert first_g % HG_HEADS == 0
    gate = pl.BlockSpec((HG_HEADS, TOKEN_TILE, V7X_LANES), lambda i: (first_g // HG_HEADS, i, 0))
    gw = hg_norm_w.astype(F32).reshape(1, HG_DK)
    tile = lambda width: pl.BlockSpec((TOKEN_TILE, width), lambda i: (i, 0))
    blocks = lambda a: pl.BlockSpec((a.shape[0], TOKEN_TILE, V7X_LANES), lambda i: (0, i, 0))
    resident = lambda a: pl.BlockSpec(a.shape, lambda i: (0, 0), pipeline_mode=pl.Buffered(1))
    wo_na, wo_hg = w_out[:NA_WIDTH], w_out[NA_WIDTH:]
    return pl.pallas_call(
        _out_mlp_kernel,
        grid=(m // TOKEN_TILE,),
        in_specs=[tile(D_MODEL), blocks(y_na), blocks(o_hg), gate,
                  pl.BlockSpec((1, HG_DK), lambda i: (0, 0)), resident(wo_na),
                  resident(wo_hg), pl.BlockSpec((1, D_MODEL), lambda i: (0, 0)),
                  resident(w_up), resident(w_down)],
        out_specs=tile(D_MODEL),
        out_shape=jax.ShapeDtypeStruct((m, D_MODEL), F32),
        compiler_params=pltpu.CompilerParams(
            dimension_semantics=("arbitrary",), vmem_limit_bytes=V7X_VMEM_LIMIT_BYTES),
        name="out_mlp",
    )(x2, y_na, o_hg, proj, gw, wo_na, wo_hg, norm_w, w_up, w_down)


def kernel(x, w_in, w_out, attn_norm_w, mlp_norm_w, q_norm_w, k_norm_w, rpb, hg_norm_w,
           lb_fwd, lb_bwd, w_up, w_down):
    batch, seq, d = x.shape
    assert d == D_MODEL and w_in.shape[0] == 1 and seq % GRID_W == 0 and seq % HG_CHUNK == 0
    x2 = x.reshape(batch * seq, d)
    proj = _in_proj(x2, attn_norm_w[0].astype(F32).reshape(1, d), w_in[0].astype(BF16),
                    q_norm_w[0], k_norm_w[0])
    y_na = _natten(proj, rpb[0], batch, seq)
    o_hg = _hgrn2(proj, lb_fwd, lb_bwd, batch, seq)
    out = _out_mlp(x2, y_na, o_hg, proj, hg_norm_w[0], w_out[0].astype(BF16),
                   mlp_norm_w[0].astype(F32).reshape(1, d),
                   w_up[0].astype(BF16), w_down[0].astype(BF16))
    return out.reshape(batch, seq, d)
```

```python
import functools

import numpy as np
import jax
import jax.numpy as jnp
from jax import lax
from jax.experimental import pallas as pl
from jax.experimental.pallas import tpu as pltpu

F32 = jnp.float32
BF16 = jnp.bfloat16

D_MODEL = 1024
GRID_W = 64
NA_HEADS = 8
NA_HEAD_DIM = 64
NA_WIDTH = NA_HEADS * NA_HEAD_DIM
NA_WIN_R = 8
NA_WIN_C = 16
HG_HEADS = 4
HG_DK = 128
HG_WIDTH = HG_HEADS * HG_DK
D_FF = 4 * D_MODEL
D_IN_PROJ = 3 * NA_WIDTH + 5 * HG_WIDTH
RMS_EPS = 1e-6

V7X_LANES = 128
V7X_SUBLANES = 8
V7X_VMEM_LIMIT_BYTES = 56 * 1024 * 1024

TOKEN_TILE = 1024
OUT_ROW_SPLITS = 2
FF_CHUNK = 1024
HG_CHUNK = 128
HG_LEVELS = 7
HG_CHUNKS_PER_STEP = 16
HG_OUT_CHUNKS_PER_STEP = 16
NA_QT = 16
NA_KT = 32
NA_TILES = GRID_W // NA_QT
NA_GROUP_ROWS = 4
NA_BAND_ROWS = 12
NA_GROUPS_PER_STEP = 2
NA_BATCH_PER_STEP = 2
HG_BATCH_PER_STEP = 2
MASK_NEG = -1e30

NA_EDGE = NA_WIN_C // 2
_NA_QCOLS = [list(range(0, NA_EDGE)) + list(range(GRID_W - NA_EDGE, GRID_W))] + [
    list(range(NA_EDGE + NA_QT * t, NA_EDGE + NA_QT * (t + 1))) for t in range(NA_TILES - 1)]
_NA_KSEGS = [[(0, NA_WIN_C), (GRID_W - NA_WIN_C, NA_WIN_C)]] + [
    [(NA_QT * t, NA_KT)] for t in range(NA_TILES - 1)]


def _nt_dot(a, b):
    return lax.dot_general(a, b, (((1,), (1,)), ((), ())), preferred_element_type=F32)


def _dot(a, b):
    return jnp.dot(a, b, preferred_element_type=F32)


def _split_bf16(x):
    hi = x.astype(BF16)
    lo = (x - hi.astype(F32)).astype(BF16)
    return hi, lo


def _in_proj_kernel(x_ref, nw_ref, w_ref, qkw_ref, o_ref):
    x = x_ref[...]
    ms = jnp.mean(x * x, axis=-1, keepdims=True)
    u = (x * lax.rsqrt(ms + RMS_EPS) * nw_ref[...]).astype(BF16)
    res = _dot(u, w_ref[...])
    first_head = lax.broadcasted_iota(jnp.int32, (1, V7X_LANES), 1) < NA_HEAD_DIM
    for j in range(D_IN_PROJ // V7X_LANES):
        blk = res[:, j * V7X_LANES:(j + 1) * V7X_LANES]
        if j < qkw_ref.shape[0]:
            sq = blk * blk
            s_all = jnp.sum(sq, axis=-1, keepdims=True)
            s_one = jnp.sum(jnp.where(first_head, sq, 0.0), axis=-1, keepdims=True)
            ms_h = jnp.where(first_head, s_one, s_all - s_one) * (1.0 / NA_HEAD_DIM)
            blk = blk * lax.rsqrt(ms_h + RMS_EPS) * qkw_ref[j]
        o_ref[j] = blk.astype(o_ref.dtype)


def _in_proj(x2, norm_w, w_in, q_norm_w, k_norm_w):
    m = x2.shape[0]
    pair_w = lambda w, mult: jnp.tile(w.astype(F32) * mult, 2).reshape(1, 1, V7X_LANES)
    n_blk = NA_WIDTH // V7X_LANES
    qkw = jnp.concatenate([jnp.tile(pair_w(q_norm_w, NA_HEAD_DIM ** -0.5), (n_blk, 1, 1)),
                           jnp.tile(pair_w(k_norm_w, 1.0), (n_blk, 1, 1))])
    return pl.pallas_call(
        _in_proj_kernel,
        grid=(m // TOKEN_TILE,),
        in_specs=[
            pl.BlockSpec((TOKEN_TILE, D_MODEL), lambda i: (i, 0)),
            pl.BlockSpec((1, D_MODEL), lambda i: (0, 0)),
            pl.BlockSpec((D_MODEL, D_IN_PROJ), lambda i: (0, 0), pipeline_mode=pl.Buffered(1)),
            pl.BlockSpec(qkw.shape, lambda i: (0, 0, 0)),
        ],
        out_specs=pl.BlockSpec((D_IN_PROJ // V7X_LANES, TOKEN_TILE, V7X_LANES), lambda i: (0, i, 0)),
        out_shape=jax.ShapeDtypeStruct((D_IN_PROJ // V7X_LANES, m, V7X_LANES), BF16),
        compiler_params=pltpu.CompilerParams(
            dimension_semantics=("arbitrary",), vmem_limit_bytes=V7X_VMEM_LIMIT_BYTES),
        name="in_proj",
    )(x2, norm_w, w_in, qkw)


def _na_group_plan(rows):
    wr = min(NA_WIN_R, rows)
    bases, types, sigs = [], [], []
    for g in range(rows // NA_GROUP_ROWS):
        r0 = g * NA_GROUP_ROWS
        starts = [min(max(r0 + j - wr // 2, 0), rows - wr) for j in range(NA_GROUP_ROWS)]
        base = min(max(starts[0], 0), rows - NA_BAND_ROWS)
        assert base <= starts[0] and starts[-1] + wr <= base + NA_BAND_ROWS
        sig = tuple((starts[j] - base, r0 + j - base) for j in range(NA_GROUP_ROWS))
        if sig not in sigs:
            sigs.append(sig)
        bases.append(base)
        types.append(sigs.index(sig))
    return bases, types, sigs


def _na_bias_table(rpb, sigs):
    n_ri, n_ci = 2 * NA_WIN_R - 1, 2 * NA_WIN_C - 1
    csel = np.zeros((n_ci, NA_TILES, NA_QT, NA_KT), np.float32)
    for t in range(NA_TILES):
        qc = np.array(_NA_QCOLS[t])
        kc = np.concatenate([np.arange(c0, c0 + n) for c0, n in _NA_KSEGS[t]])
        cs = np.clip(qc - NA_WIN_C // 2, 0, GRID_W - NA_WIN_C)
        valid = (kc[None, :] >= cs[:, None]) & (kc[None, :] < cs[:, None] + NA_WIN_C)
        cidx = kc[None, :] - qc[:, None] + NA_WIN_C - 1
        qi, ki = np.nonzero(valid)
        csel[cidx[qi, ki], t, qi, ki] = 1.0
    cols = jnp.einsum("hrc,ctqk->htqrk", rpb.astype(F32), jnp.asarray(csel),
                      precision=lax.Precision.HIGHEST)
    cols = cols + jnp.asarray(np.where(csel.sum(0) > 0, 0.0, MASK_NEG)[None, :, :, None, :], F32)
    cols = cols.reshape(NA_HEADS // 2, 2, NA_TILES, NA_QT, n_ri * NA_KT)
    cols = jnp.transpose(cols, (0, 2, 1, 3, 4))
    tabs = []
    for sig in sigs:
        per_row = []
        for first, qrow in sig:
            ri0 = first - qrow + NA_WIN_R - 1
            after = NA_BAND_ROWS - first - NA_WIN_R
            window = cols[..., ri0 * NA_KT:(ri0 + NA_WIN_R) * NA_KT]
            per_row.append(jnp.pad(window, [(0, 0)] * 4 + [(first * NA_KT, after * NA_KT)],
                                   constant_values=MASK_NEG))
        tabs.append(jnp.stack(per_row, axis=2))
    return jnp.stack(tabs).reshape(len(sigs), NA_HEADS // 2, NA_TILES,
                                   NA_GROUP_ROWS * 2 * NA_QT, NA_BAND_ROWS * NA_KT)


def _natten_kernel(*refs, n_batch, rows, bases, types):
    for bb in range(n_batch):
        _natten_one(*refs, bb * rows * GRID_W, rows=rows, bases=bases, types=types)


def _natten_one(q_ref, k_ref, v_ref, bias_ref, o_ref, tok0, *, rows, bases, types):
    lane = lax.broadcasted_iota(jnp.int32, (NA_QT, V7X_LANES), 1)
    first_head = lane < NA_HEAD_DIM
    gr = NA_GROUP_ROWS

    def row_slice(grid_row):
        start = tok0 + grid_row * GRID_W
        if isinstance(start, int):
            return slice(start, start + GRID_W)
        return pl.ds(pl.multiple_of(start, GRID_W), GRID_W)

    def row_block(ref, grid_row):
        return ref[row_slice(grid_row), :]

    def band(blocks, t):
        return jnp.concatenate(
            [blk[c0:c0 + n] for blk in blocks for (c0, n) in _NA_KSEGS[t]], axis=0)

    def step(groups):
        scores = []
        for row0, base, y in groups:
            kblocks = [row_block(k_ref, base + u) for u in range(NA_BAND_ROWS)]
            qrows = [row_block(q_ref, row0 + j).astype(F32) for j in range(gr)]
            for t in range(NA_TILES):
                qc = _NA_QCOLS[t]
                q2 = []
                for j in range(gr):
                    if t == 0:
                        qt = jnp.concatenate([qrows[j][0:NA_EDGE], qrows[j][GRID_W - NA_EDGE:]], axis=0)
                    else:
                        qt = qrows[j][qc[0]:qc[0] + NA_QT]
                    q2 += [jnp.where(first_head, qt, 0.0), jnp.where(first_head, 0.0, qt)]
                s = _nt_dot(jnp.concatenate(q2, axis=0).astype(BF16), band(kblocks, t))
                scores.append(s + bias_ref[y, t])
        probs, sums = [], []
        for s in scores:
            p = jnp.exp(s - jnp.max(s, axis=-1, keepdims=True))
            sums.append(jnp.sum(p, axis=-1, keepdims=True))
            probs.append(p.astype(BF16))
        for g, (row0, base, y) in enumerate(groups):
            vblocks = [row_block(v_ref, base + u) for u in range(NA_BAND_ROWS)]
            o_tiles = []
            for t in range(NA_TILES):
                o2 = _dot(probs[g * NA_TILES + t], band(vblocks, t)) / sums[g * NA_TILES + t]
                o_tiles.append(o2)
            for j in range(gr):
                sel = [jnp.where(first_head, o[2 * NA_QT * j:2 * NA_QT * j + NA_QT],
                                 o[2 * NA_QT * j + NA_QT:2 * NA_QT * (j + 1)]) for o in o_tiles]
                orow = jnp.concatenate([sel[0][0:NA_EDGE], *sel[1:], sel[0][NA_EDGE:]], axis=0)
                o_ref[row_slice(row0 + j), :] = orow.astype(o_ref.dtype)

    n_groups = rows // gr
    shift = min(NA_WIN_R, rows) // 2
    regular = [bases[g] == g * gr - shift for g in range(n_groups)]
    lo = regular.index(True)
    hi = n_groups - regular[::-1].index(True)
    per = NA_GROUPS_PER_STEP
    assert all(regular[lo:hi]) and (hi - lo) % per == 0
    assert len({types[g] for g in range(lo, hi)}) == 1
    edge = [(g * gr, bases[g], types[g]) for g in list(range(lo)) + list(range(hi, n_groups))]
    for i in range(0, len(edge), per):
        step(edge[i:i + per])

    for g0 in range(lo, hi, per):
        step([((g0 + n) * gr, (g0 + n) * gr - shift, types[lo]) for n in range(per)])


def _natten(proj, rpb, batch, seq):
    rows = seq // GRID_W
    assert rows >= NA_BAND_ROWS and rows % NA_GROUP_ROWS == 0
    assert GRID_W == 64 and NA_HEAD_DIM * 2 == V7X_LANES
    pairs = NA_HEADS // 2
    bases, types, sigs = _na_group_plan(rows)
    bias = _na_bias_table(rpb, sigs)
    nb = NA_BATCH_PER_STEP
    assert batch % nb == 0
    col = lambda off: pl.BlockSpec((None, nb * seq, V7X_LANES),
                                   lambda p, b, off=off: (off + p, b, 0))
    return pl.pallas_call(
        functools.partial(_natten_kernel, n_batch=nb, rows=rows, bases=tuple(bases),
                          types=tuple(types)),
        grid=(pairs, batch // nb),
        in_specs=[
            col(0), col(pairs), col(2 * pairs),
            pl.BlockSpec((len(sigs), None, NA_TILES, NA_GROUP_ROWS * 2 * NA_QT,
                          NA_BAND_ROWS * NA_KT), lambda p, b: (0, p, 0, 0, 0)),
        ],
        out_specs=pl.BlockSpec((None, nb * seq, V7X_LANES), lambda p, b: (p, b, 0)),
        out_shape=jax.ShapeDtypeStruct((pairs, batch * seq, V7X_LANES), BF16),
        compiler_params=pltpu.CompilerParams(
            dimension_semantics=("arbitrary", "arbitrary"),
            vmem_limit_bytes=V7X_VMEM_LIMIT_BYTES),
        name="natten",
    )(proj, proj, proj, bias)


def _hg_tables():
    c = HG_CHUNK
    idx = np.arange(c)
    lower = (idx[None, :] <= idx[:, None]).astype(np.float32)
    cums = np.stack([np.concatenate([lower, lower], axis=1),
                     np.concatenate([lower.T, lower.T], axis=1)])
    masks = []
    for lvl in range(HG_LEVELS):
        s = 1 << lvl
        same_block = (idx[:, None] // (2 * s)) == (idx[None, :] // (2 * s))
        half = (idx // s) % 2
        masks.append(same_block & (half[:, None] != half[None, :]))
    masks.append(np.eye(c, dtype=bool))
    return jnp.asarray(cums, BF16), jnp.asarray(np.stack(masks), F32)


def _hgrn_kernel(*refs, n_batch, seq):
    for bb in range(n_batch):
        _hgrn_one(*refs, bb * seq, seq=seq)


def _hgrn_one(q_ref, ff_ref, fb_ref, v_ref, lbf_ref, lbb_ref,
              cum_ref, mask_ref, o_ref, acc_s, qb_s, inc_s, etb_s, tok0, *, seq):
    c = HG_CHUNK
    n_chunks = seq // c
    n_step = HG_CHUNKS_PER_STEP
    assert n_chunks % n_step == 0
    grp = V7X_SUBLANES
    sub = lax.broadcasted_iota(jnp.int32, (c, V7X_LANES), 0) & (grp - 1)

    def lower_bound(lb_ref):
        a = lb_ref[...].astype(F32)
        a0, a1 = a[0:1], a[1:2]
        m = jnp.maximum(a0, a1)
        e0, e1 = jnp.exp(a0 - m), jnp.exp(a1 - m)
        return e0 / (e0 + e1)

    lbf = lower_bound(lbf_ref)
    lbb = lower_bound(lbb_ref)

    def rows(ci, tok=0):
        return pl.ds(pl.multiple_of(tok + ci * c, c), c)

    def group_rows(x, r):
        return jnp.concatenate(
            [jnp.broadcast_to(x[grp * j + r:grp * j + r + 1], (grp, V7X_LANES))
             for j in range(c // grp)], axis=0)

    def repeat_group(xg, n_rows):
        return xg if n_rows == grp else jnp.concatenate([xg] * (n_rows // grp), axis=0)

    def boundary_rows(x, s, offset):
        parts = [group_rows(x, 2 * s * blk + s - 1 + offset) for blk in range(grp // (2 * s))]
        assert len(parts) <= 2
        return parts[0] if len(parts) == 1 else jnp.where(sub < grp // 2, parts[0], parts[1])

    def gates(f_ref, sl, lb):
        hf = f_ref[sl, :].astype(F32)
        f = lb + (1.0 - lb) * (1.0 / (1.0 + jnp.exp(-hf)))
        hi, lo = _split_bf16(jnp.log2(f))
        return f, 1.0 - f, jnp.concatenate([hi, lo], axis=0)

    def stage_gates(ci):
        src = rows(ci, tok0)
        st = dict(sl=rows(ci), q=q_ref[src, :].astype(F32), v=v_ref[src, :])
        st["f_f"], st["k_f"], lf_f = gates(ff_ref, src, lbf)
        st["f_b"], st["k_b"], lf_b = gates(fb_ref, src, lbb)
        st["b_f"] = _dot(cum_ref[0], lf_f)
        st["b_b"] = _dot(cum_ref[1], lf_b)
        return st

    def level_operands(st, lvl):
        q, k_f, k_b, b_f, b_b = st["q"], st["k_f"], st["k_b"], st["b_f"], st["b_b"]
        if lvl == HG_LEVELS:
            return q, k_f + k_b
        if lvl == 0:
            odd = (sub & 1) == 1
            return q * jnp.where(odd, st["f_f"], st["f_b"]), jnp.where(odd, k_b, k_f)
        if (1 << lvl) < grp:
            later = (sub & (1 << lvl)) != 0
            d_f = b_f - boundary_rows(b_f, 1 << lvl, 0)
            d_b = b_b - boundary_rows(b_b, 1 << lvl, 1)
            return (q * jnp.exp2(jnp.where(later, d_f, d_b)),
                    jnp.where(later, k_b, k_f) * jnp.exp2(-jnp.where(later, d_b, d_f)))
        s = 1 << lvl
        xp, yp = [], []
        for r0 in range(0, c, 2 * s):
            m = r0 + s
            rf = repeat_group(st["last_f"][m - grp:m], s)
            rb = repeat_group(st["first_b"][m:m + grp], s)
            early, late = slice(r0, m), slice(m, m + s)
            xp += [q[early] * jnp.exp2(b_b[early] - rb), q[late] * jnp.exp2(b_f[late] - rf)]
            yp += [k_f[early] * jnp.exp2(rf - b_f[early]), k_b[late] * jnp.exp2(rb - b_b[late])]
        return jnp.concatenate(xp, axis=0), jnp.concatenate(yp, axis=0)

    def stage_operands(ci, st):
        q, k_f, k_b, b_f, b_b = st["q"], st["k_f"], st["k_b"], st["b_f"], st["b_b"]
        last_f = st["last_f"] = group_rows(b_f, grp - 1)
        first_b = st["first_b"] = group_rows(b_b, 0)
        sl = st["sl"]
        tot_f = last_f[c - grp:c]
        tot_b = first_b[0:grp]
        st["qe_f"] = (q * jnp.exp2(b_f)).astype(BF16)
        st["kd_f"] = (k_f * jnp.exp2(repeat_group(tot_f, c) - b_f)).astype(BF16)
        st["tot_f"] = jnp.exp2(tot_f[0:1])
        qb_s[sl, :] = (q * jnp.exp2(b_b)).astype(BF16)
        st["kd_b"] = (k_b * jnp.exp2(repeat_group(tot_b, c) - b_b)).astype(BF16)
        etb_s[pl.ds(pl.multiple_of(ci * grp, grp), grp), :] = jnp.exp2(tot_b)
        return st

    def fwd_body(i, state_t):
        cis = [i * n_step + j for j in range(n_step)]
        sts = [stage_gates(ci) for ci in cis]
        sts = [stage_operands(ci, st) for ci, st in zip(cis, sts)]
        for st in sts:
            for lvl in range(HG_LEVELS + 1):
                x, y = level_operands(st, lvl)
                term = mask_ref[lvl] * _nt_dot(x.astype(BF16), y.astype(BF16))
                st["a"] = term if lvl == 0 else st["a"] + term
        intra = [_dot(st["a"].astype(BF16), st["v"]) for st in sts]
        incs = [_dot(st["v"].T, jnp.concatenate([st["kd_f"], st["kd_b"]], axis=1)) for st in sts]
        for st, o, inc in zip(sts, intra, incs):
            acc_s[st["sl"], :] = o + _nt_dot(st["qe_f"], state_t.astype(BF16))
            state_t = state_t * st["tot_f"] + inc[:, 0:HG_DK]
            inc_s[st["sl"], :] = inc[:, HG_DK:2 * HG_DK]
        return state_t

    zero = jnp.zeros((HG_DK, HG_DK), F32)
    lax.fori_loop(0, n_chunks // n_step, fwd_body, zero)

    n_out = HG_OUT_CHUNKS_PER_STEP
    assert n_chunks % n_out == 0

    def bwd_body(i, state_t):
        cis = [n_chunks - 1 - (i * n_out + j) for j in range(n_out)]
        inter = []
        for ci in cis:
            inter.append(_nt_dot(qb_s[rows(ci), :], state_t.astype(BF16)))
            tot_b = etb_s[pl.ds(pl.multiple_of(ci * grp, grp), 1), :]
            state_t = state_t * tot_b + inc_s[rows(ci), :]
        for ci, o_inter in zip(cis, inter):
            o_ref[rows(ci, tok0), :] = (acc_s[rows(ci), :] + o_inter).astype(o_ref.dtype)
        return state_t

    lax.fori_loop(0, n_chunks // n_out, bwd_body, zero)


def _hgrn2(proj, lb_fwd, lb_bwd, batch, seq):
    assert HG_DK == V7X_LANES and lb_fwd.shape[0] == 2
    cums, masks = _hg_tables()
    first = 3 * NA_WIDTH // V7X_LANES
    nb = HG_BATCH_PER_STEP
    assert batch % nb == 0
    col = lambda k: pl.BlockSpec((None, nb * seq, V7X_LANES),
                                 lambda b, h, k=k: (first + k * HG_HEADS + h, b, 0))
    lbs = pl.BlockSpec((2, V7X_LANES), lambda b, h: (0, h))
    const = lambda a: pl.BlockSpec(a.shape, lambda b, h, nd=a.ndim: (0,) * nd)
    return pl.pallas_call(
        functools.partial(_hgrn_kernel, n_batch=nb, seq=seq),
        grid=(batch // nb, HG_HEADS),
        in_specs=[col(0), col(1), col(2), col(3), lbs, lbs, const(cums), const(masks)],
        out_specs=pl.BlockSpec((None, nb * seq, V7X_LANES), lambda b, h: (h, b, 0)),
        out_shape=jax.ShapeDtypeStruct((HG_HEADS, batch * seq, V7X_LANES), BF16),
        scratch_shapes=[pltpu.VMEM((seq, HG_DK), F32), pltpu.VMEM((seq, HG_DK), BF16),
                        pltpu.VMEM((seq, HG_DK), F32),
                        pltpu.VMEM((V7X_SUBLANES * seq // HG_CHUNK, HG_DK), F32)],
        compiler_params=pltpu.CompilerParams(
            dimension_semantics=("arbitrary", "arbitrary"),
            vmem_limit_bytes=V7X_VMEM_LIMIT_BYTES),
        name="hgrn2",
    )(proj, proj, proj, proj, lb_fwd.astype(F32), lb_bwd.astype(F32), cums, masks)


def _out_mlp_kernel(x_ref, na_ref, hg_ref, g_ref, gw_ref, wo_na_ref, wo_hg_ref, nw_ref, wu_ref,
                    wd_ref, o_ref):
    def recurrence_head(j, rows):
        o = hg_ref[j, rows, :].astype(F32)
        ms = jnp.mean(o * o, axis=-1, keepdims=True)
        g = g_ref[j, rows, :].astype(F32)
        return (o * lax.rsqrt(ms + RMS_EPS) * gw_ref[...]
                * (g * (1.0 / (1.0 + jnp.exp(-g))))).astype(BF16)

    n_rows = x_ref.shape[0]
    for r0 in range(0, n_rows, n_rows // OUT_ROW_SPLITS):
        rows = slice(r0, r0 + n_rows // OUT_ROW_SPLITS)
        y_na = jnp.concatenate([na_ref[j, rows, :] for j in range(na_ref.shape[0])], axis=1)
        y_hg = jnp.concatenate([recurrence_head(j, rows) for j in range(HG_HEADS)], axis=1)
        h = x_ref[rows, :] + _dot(y_na, wo_na_ref[...]) + _dot(y_hg, wo_hg_ref[...])
        ms = jnp.mean(h * h, axis=-1, keepdims=True)
        u = (h * lax.rsqrt(ms + RMS_EPS) * nw_ref[...]).astype(BF16)
        mlp = None
        for j in range(D_FF // FF_CHUNK):
            cols = slice(j * FF_CHUNK, (j + 1) * FF_CHUNK)
            a = jnp.maximum(_dot(u, wu_ref[:, cols]), 0.0)
            part = _dot((a * a).astype(BF16), wd_ref[cols, :])
            mlp = part if mlp is None else mlp + part
        o_ref[rows, :] = h + mlp


def _out_mlp(x2, y_na, o_hg, proj, hg_norm_w, w_out, norm_w, w_up, w_down):
    m = x2.shape[0]
    first_g = (3 * NA_WIDTH + 4 * HG_WIDTH) // V7X_LANES
    assert first_g % HG_HEADS == 0
    gate = pl.BlockSpec((HG_HEADS, TOKEN_TILE, V7X_LANES), lambda i: (first_g // HG_HEADS, i, 0))
    gw = hg_norm_w.astype(F32).reshape(1, HG_DK)
    tile = lambda width: pl.BlockSpec((TOKEN_TILE, width), lambda i: (i, 0))
    blocks = lambda a: pl.BlockSpec((a.shape[0], TOKEN_TILE, V7X_LANES), lambda i: (0, i, 0))
    resident = lambda a: pl.BlockSpec(a.shape, lambda i: (0, 0), pipeline_mode=pl.Buffered(1))
    wo_na, wo_hg = w_out[:NA_WIDTH], w_out[NA_WIDTH:]
    return pl.pallas_call(
        _out_mlp_kernel,
        grid=(m // TOKEN_TILE,),
        in_specs=[tile(D_MODEL), blocks(y_na), blocks(o_hg), gate,
                  pl.BlockSpec((1, HG_DK), lambda i: (0, 0)), resident(wo_na),
                  resident(wo_hg), pl.BlockSpec((1, D_MODEL), lambda i: (0, 0)),
                  resident(w_up), resident(w_down)],
        out_specs=tile(D_MODEL),
        out_shape=jax.ShapeDtypeStruct((m, D_MODEL), F32),
        compiler_params=pltpu.CompilerParams(
            dimension_semantics=("arbitrary",), vmem_limit_bytes=V7X_VMEM_LIMIT_BYTES),
        name="out_mlp",
    )(x2, y_na, o_hg, proj, gw, wo_na, wo_hg, norm_w, w_up, w_down)


def kernel(x, w_in, w_out, attn_norm_w, mlp_norm_w, q_norm_w, k_norm_w, rpb, hg_norm_w,
           lb_fwd, lb_bwd, w_up, w_down):
    batch, seq, d = x.shape
    assert d == D_MODEL and w_in.shape[0] == 1 and seq % GRID_W == 0 and seq % HG_CHUNK == 0
    x2 = x.reshape(batch * seq, d)
    proj = _in_proj(x2, attn_norm_w[0].astype(F32).reshape(1, d), w_in[0].astype(BF16),
                    q_norm_w[0], k_norm_w[0])
    y_na = _natten(proj, rpb[0], batch, seq)
    o_hg = _hgrn2(proj, lb_fwd, lb_bwd, batch, seq)
    out = _out_mlp(x2, y_na, o_hg, proj, hg_norm_w[0], w_out[0].astype(BF16),
                   mlp_norm_w[0].astype(F32).reshape(1, d),
                   w_up[0].astype(BF16), w_down[0].astype(BF16))
    return out.reshape(batch, seq, d)
```
